```python
import math
import jax
import jax.numpy as jnp
from jax import lax
import numpy as np

D_MODEL = 1024
BATCH = 8
SEQ = 8192
DEPTH = 2
DEC_BATCH = 16
DEC_SEQ = 2048
PAST_LEN = 128

HEAD_DIM = 64
GRID_W = 64
NA_HEADS = 8
NA_WIN_ROWS = 8
NA_WIN_COLS = 16
NA_WIDTH = NA_HEADS * HEAD_DIM
HY_WIDTH = D_MODEL // 2
HY_ORDER = 2
HY_DIRS = 2
HY_SHORT_CONV = 3
HY_POS_BANDS = 16
HY_EMB_DIM = 1 + 2 * HY_POS_BANDS
HY_FILTER_HIDDEN = 64
HY_FAST_DECAY_PCT = 0.3
HY_SLOW_DECAY_PCT = 1.5
HY_DECAY_TARGET = 1e-2
SWA_HEADS = 8
SWA_KV_HEADS = 2
SWA_WIDTH = SWA_HEADS * HEAD_DIM
SWA_KV_WIDTH = SWA_KV_HEADS * HEAD_DIM
SWA_WINDOW = 128
SWA_BLOCK = 128
N_BRANCH = 3
D_FF = 4 * D_MODEL
OFF_HY = 3 * NA_WIDTH
OFF_SWA = OFF_HY + 3 * HY_WIDTH
OFF_GATE = OFF_SWA + SWA_WIDTH + 2 * SWA_KV_WIDTH
D_IN = OFF_GATE + N_BRANCH * D_MODEL
DEEPNORM_ALPHA = (2 * DEPTH) ** 0.25
DEEPNORM_BETA = (8 * DEPTH) ** -0.25
LN_EPS = 1e-5
NEG_BIG = -1e30

kernel_name = "hybrid_bidir_encoder_two_groups"


def _layernorm(x, g, b):
    xf = x.astype(jnp.float32)
    mu = jnp.mean(xf, axis=-1, keepdims=True)
    var = jnp.mean(jnp.square(xf - mu), axis=-1, keepdims=True)
    return ((xf - mu) * lax.rsqrt(var + LN_EPS) * g + b).astype(x.dtype)


def _neighbourhood_attention(q, k, v, rpb):
    b_sz, seq_len, heads, dh = q.shape
    rows = seq_len // GRID_W
    kr = min(NA_WIN_ROWS, rows)
    kc = NA_WIN_COLS
    qg = q.reshape(b_sz * rows, GRID_W, heads, dh)
    kg = k.reshape(b_sz, rows, GRID_W, heads, dh)
    vg = v.reshape(b_sz, rows, GRID_W, heads, dh)
    cols = jnp.arange(GRID_W)
    col_start = jnp.clip(cols - kc // 2, 0, GRID_W - kc)
    col_idx = col_start[:, None] + jnp.arange(kc)[None, :]
    rpb_cols = rpb[:, :, col_idx - cols[:, None] + (NA_WIN_COLS - 1)]
    scale = dh ** -0.5

    def row_block(idx):
        bi = idx // rows
        r = idx % rows
        rs = jnp.clip(r - kr // 2, 0, rows - kr)
        q_r = qg[idx]
        k_r = lax.dynamic_slice(kg, (bi, rs, 0, 0, 0), (1, kr, GRID_W, heads, dh))[0]
        v_r = lax.dynamic_slice(vg, (bi, rs, 0, 0, 0), (1, kr, GRID_W, heads, dh))[0]
        k_n = k_r[:, col_idx]
        v_n = v_r[:, col_idx]
        bias = jnp.transpose(rpb_cols[:, rs + jnp.arange(kr) - r + (NA_WIN_ROWS - 1)], (0, 2, 1, 3))
        s = jnp.einsum('whd,rwkhd->hwrk', q_r, k_n, preferred_element_type=jnp.float32) * scale + bias
        p = jax.nn.softmax(s, axis=(-2, -1))
        return jnp.einsum('hwrk,rwkhd->whd', p.astype(v.dtype), v_n)

    out = lax.map(row_block, jnp.arange(b_sz * rows))
    return out.reshape(b_sz, seq_len, heads * dh)


def _hyena_filters(seq_len, w1, b1, w2, b2, freq, w3):
    f32 = jnp.float32
    t = jnp.linspace(0.0, 1.0, seq_len, dtype=f32)[:, None]
    w = 2.0 * math.pi * jnp.arange(seq_len, dtype=f32) / seq_len
    bands = jnp.linspace(1e-4, HY_POS_BANDS - 1, HY_POS_BANDS, dtype=f32)
    ang = w[:, None] * bands[None, :]
    z = jnp.concatenate([t, jnp.cos(ang), -jnp.sin(ang)], axis=-1)
    fr = freq.astype(f32)
    h = jnp.sin(fr * (z @ w1.astype(f32) + b1.astype(f32)))
    h = jnp.sin(fr * (h @ w2.astype(f32) + b2.astype(f32)))
    h = h @ w3.astype(f32)
    min_decay = math.log(HY_DECAY_TARGET) / HY_SLOW_DECAY_PCT
    max_decay = math.log(HY_DECAY_TARGET) / HY_FAST_DECAY_PCT
    deltas = jnp.linspace(min_decay, max_decay, HY_WIDTH, dtype=f32)
    window = jnp.exp(-t * jnp.abs(deltas)[None, :])
    return h.reshape(seq_len, HY_ORDER, HY_DIRS, HY_WIDTH) * window[:, None, None, :]


def _bidir_fftconv(u, h_fwd, h_bwd):
    seq_len, ch = h_fwd.shape
    n = 2 * seq_len
    kern = jnp.concatenate([h_fwd, jnp.zeros((1, ch), jnp.float32), h_bwd[:0:-1]], axis=0)
    kf = jnp.fft.rfft(kern, n=n, axis=0)
    uf = jnp.fft.rfft(u.astype(jnp.float32), n=n, axis=1)
    y = jnp.fft.irfft(uf * kf[None], n=n, axis=1)[:, :seq_len]
    return y.astype(u.dtype)


def _hyena(proj, conv_w, conv_b, filters, bias):
    seq_len = proj.shape[1]
    half = HY_SHORT_CONV // 2
    xp = jnp.pad(proj, ((0, 0), (half, half), (0, 0)))
    u = sum(xp[:, j:j + seq_len] * conv_w[j] for j in range(HY_SHORT_CONV)) + conv_b
    v, x1, x2 = jnp.split(u, 3, axis=-1)
    z = v
    for n, gate in enumerate((x1, x2)):
        z = gate * (_bidir_fftconv(z, filters[:, n, 0], filters[:, n, 1]) + bias[n] * z)
    return z


def _window_gqa(q, k, v, sink):
    b_sz, seq_len, hq, dh = q.shape
    hkv = k.shape[2]
    grp = hq // hkv
    nb = seq_len // SWA_BLOCK
    span = SWA_BLOCK + 2 * SWA_WINDOW
    pad = ((0, 0), (SWA_WINDOW, SWA_WINDOW), (0, 0), (0, 0))
    kp = jnp.pad(k, pad)
    vp = jnp.pad(v, pad)
    slopes = 2.0 ** (-8.0 * (jnp.arange(hq, dtype=jnp.float32) + 1.0) / hq)
    q_off = jnp.arange(SWA_BLOCK)[:, None]
    k_off = jnp.arange(span) - SWA_WINDOW
    rel = jnp.abs(k_off[None, :] - q_off)
    band = rel <= SWA_WINDOW
    alibi = (-slopes[:, None, None] * rel.astype(jnp.float32)).reshape(hkv, grp, SWA_BLOCK, span)
    sink_l = sink.astype(jnp.float32).reshape(hkv, grp)[None, :, :, None, None]
    scale = dh ** -0.5

    def block(i):
        start = i * SWA_BLOCK
        q_b = lax.dynamic_slice_in_dim(q, start, SWA_BLOCK, axis=1).reshape(b_sz, SWA_BLOCK, hkv, grp, dh)
        k_b = lax.dynamic_slice_in_dim(kp, start, span, axis=1)
        v_b = lax.dynamic_slice_in_dim(vp, start, span, axis=1)
        key_pos = start + k_off
        valid = band & ((key_pos >= 0) & (key_pos < seq_len))[None, :]
        s = jnp.einsum('bqgjd,bkgd->bgjqk', q_b, k_b, preferred_element_type=jnp.float32) * scale + alibi
        s = jnp.where(valid, s, NEG_BIG)
        m = jnp.maximum(jnp.max(s, axis=-1, keepdims=True), sink_l)
        p = jnp.exp(s - m)
        denom = jnp.sum(p, axis=-1, keepdims=True) + jnp.exp(sink_l - m)
        o = jnp.einsum('bgjqk,bkgd->bqgjd', (p / denom).astype(v.dtype), v_b)
        return o.reshape(b_sz, SWA_BLOCK, hq * dh)

    out = lax.map(block, jnp.arange(nb))
    return jnp.transpose(out, (1, 0, 2, 3)).reshape(b_sz, seq_len, hq * dh)


def _encoder_block(x, w_in, b_in, hy_conv_w, hy_conv_b, hy_filt_w1, hy_filt_b1, hy_filt_w2, hy_filt_b2,
                   hy_filt_freq, hy_filt_w3, hy_bias, na_rpb, swa_sink, w_branch_a, w_branch_b, w_branch_c,
                   w_out, ln1_g, ln1_b, w_up, b_up, w_down, b_down, ln2_g, ln2_b):
    b_sz, seq_len, _ = x.shape
    h = jnp.einsum('bld,de->ble', x, w_in) + b_in
    splits = [NA_WIDTH, 2 * NA_WIDTH, OFF_HY, OFF_SWA, OFF_SWA + SWA_WIDTH,
              OFF_SWA + SWA_WIDTH + SWA_KV_WIDTH, OFF_GATE]
    na_q, na_k, na_v, hy_in, sw_q, sw_k, sw_v, gates = jnp.split(h, splits, axis=-1)
    a = _neighbourhood_attention(na_q.reshape(b_sz, seq_len, NA_HEADS, HEAD_DIM),
                                 na_k.reshape(b_sz, seq_len, NA_HEADS, HEAD_DIM),
                                 na_v.reshape(b_sz, seq_len, NA_HEADS, HEAD_DIM), na_rpb)
    filters = _hyena_filters(seq_len, hy_filt_w1, hy_filt_b1, hy_filt_w2, hy_filt_b2, hy_filt_freq, hy_filt_w3)
    hb = _hyena(hy_in, hy_conv_w, hy_conv_b, filters, hy_bias)
    c = _window_gqa(sw_q.reshape(b_sz, seq_len, SWA_HEADS, HEAD_DIM),
                    sw_k.reshape(b_sz, seq_len, SWA_KV_HEADS, HEAD_DIM),
                    sw_v.reshape(b_sz, seq_len, SWA_KV_HEADS, HEAD_DIM), swa_sink)
    g = jax.nn.sigmoid(gates.astype(jnp.float32)).astype(x.dtype).reshape(b_sz, seq_len, N_BRANCH, D_MODEL)
    merged = (g[:, :, 0] * (a @ w_branch_a) + g[:, :, 1] * (hb @ w_branch_b) + g[:, :, 2] * (c @ w_branch_c))
    mix = merged @ w_out
    x = _layernorm(DEEPNORM_ALPHA * x + mix, ln1_g, ln1_b)
    ff = jnp.square(jax.nn.relu(x @ w_up + b_up)) @ w_down + b_down
    return _layernorm(DEEPNORM_ALPHA * x + ff, ln2_g, ln2_b)


def setup_inputs(seed: int = 0) -> dict:
    key = jax.random.key(seed)
    ks = jax.random.split(key, 32)
    f32 = jnp.float32

    def nrm(k, shape, scale):
        return jax.random.normal(k, shape, f32) * scale

    return {
        'x_prompt': nrm(ks[0], (BATCH, SEQ, D_MODEL), 1.0),
        'x_sample': nrm(ks[1], (DEC_BATCH, DEC_SEQ, D_MODEL), 1.0),
        'w_in': nrm(ks[2], (DEPTH, D_MODEL, D_IN), D_MODEL ** -0.5),
        'b_in': nrm(ks[3], (DEPTH, D_IN), 0.02),
        'hy_conv_w': nrm(ks[4], (DEPTH, HY_SHORT_CONV, 3 * HY_WIDTH), HY_SHORT_CONV ** -0.5),
        'hy_conv_b': nrm(ks[5], (DEPTH, 3 * HY_WIDTH), 0.02),
        'hy_filt_w1': nrm(ks[6], (DEPTH, HY_EMB_DIM, HY_FILTER_HIDDEN), HY_EMB_DIM ** -0.5),
        'hy_filt_b1': nrm(ks[7], (DEPTH, HY_FILTER_HIDDEN), 0.02),
        'hy_filt_w2': nrm(ks[8], (DEPTH, HY_FILTER_HIDDEN, HY_FILTER_HIDDEN), HY_FILTER_HIDDEN ** -0.5),
        'hy_filt_b2': nrm(ks[9], (DEPTH, HY_FILTER_HIDDEN), 0.02),
        'hy_filt_freq': 1.0 + nrm(ks[10], (DEPTH, HY_FILTER_HIDDEN), 0.01),
        'hy_filt_w3': nrm(ks[11], (DEPTH, HY_FILTER_HIDDEN, HY_ORDER * HY_DIRS * HY_WIDTH), 0.08 * HY_FILTER_HIDDEN ** -0.5),
        'hy_bias': nrm(ks[12], (DEPTH, HY_ORDER, HY_WIDTH), 0.5),
        'na_rpb': nrm(ks[13], (DEPTH, NA_HEADS, 2 * NA_WIN_ROWS - 1, 2 * NA_WIN_COLS - 1), 0.02),
        'swa_sink': nrm(ks[14], (DEPTH, SWA_HEADS), 0.5),
        'w_branch_a': nrm(ks[15], (DEPTH, NA_WIDTH, D_MODEL), DEEPNORM_BETA * NA_WIDTH ** -0.5),
        'w_branch_b': nrm(ks[16], (DEPTH, HY_WIDTH, D_MODEL), DEEPNORM_BETA * HY_WIDTH ** -0.5),
        'w_branch_c': nrm(ks[17], (DEPTH, SWA_WIDTH, D_MODEL), DEEPNORM_BETA * SWA_WIDTH ** -0.5),
        'w_out': nrm(ks[18], (DEPTH, D_MODEL, D_MODEL), DEEPNORM_BETA * D_MODEL ** -0.5),
        'ln1_g': 1.0 + nrm(ks[19], (DEPTH, D_MODEL), 0.02),
        'ln1_b': nrm(ks[20], (DEPTH, D_MODEL), 0.02),
        'w_up': nrm(ks[21], (DEPTH, D_MODEL, D_FF), D_MODEL ** -0.5),
        'b_up': nrm(ks[22], (DEPTH, D_FF), 0.02),
        'w_down': nrm(ks[23], (DEPTH, D_FF, D_MODEL), DEEPNORM_BETA * D_FF ** -0.5),
        'b_down': nrm(ks[24], (DEPTH, D_MODEL), 0.02),
        'ln2_g': 1.0 + nrm(ks[25], (DEPTH, D_MODEL), 0.02),
        'ln2_b': nrm(ks[26], (DEPTH, D_MODEL), 0.02),
    }


def reference(x_prompt, x_sample, w_in, b_in, hy_conv_w, hy_conv_b, hy_filt_w1, hy_filt_b1, hy_filt_w2,
              hy_filt_b2, hy_filt_freq, hy_filt_w3, hy_bias, na_rpb, swa_sink, w_branch_a, w_branch_b,
              w_branch_c, w_out, ln1_g, ln1_b, w_up, b_up, w_down, b_down, ln2_g, ln2_b):
    y_prompt = x_prompt
    y_sample = x_sample
    for l in range(DEPTH):
        layer = (w_in[l], b_in[l], hy_conv_w[l], hy_conv_b[l], hy_filt_w1[l], hy_filt_b1[l], hy_filt_w2[l],
                 hy_filt_b2[l], hy_filt_freq[l], hy_filt_w3[l], hy_bias[l], na_rpb[l], swa_sink[l],
                 w_branch_a[l], w_branch_b[l], w_branch_c[l], w_out[l], ln1_g[l], ln1_b[l], w_up[l], b_up[l],
                 w_down[l], b_down[l], ln2_g[l], ln2_b[l])
        y_prompt = _encoder_block(y_prompt, *layer)
        y_sample = _encoder_block(y_sample, *layer)
    return (y_prompt, y_sample)
```

```python
import functools
import math

import numpy as np
import jax
import jax.numpy as jnp
from jax import lax
from jax.experimental import pallas as pl
from jax.experimental.pallas import tpu as pltpu

F32 = jnp.float32
BF16 = jnp.bfloat16

D_MODEL = 1024
DEPTH = 2
HEAD_DIM = 64
GRID_W = 64
NA_HEADS = 8
NA_WIN_ROWS = 8
NA_WIN_COLS = 16
NA_WIDTH = NA_HEADS * HEAD_DIM
HY_WIDTH = D_MODEL // 2
HY_ORDER = 2
HY_SHORT_CONV = 3
HY_POS_BANDS = 16
HY_EMB_DIM = 1 + 2 * HY_POS_BANDS
HY_EMB_PAD = 128
HY_FILTER_HIDDEN = 64
HY_FAST_DECAY_PCT = 0.3
HY_SLOW_DECAY_PCT = 1.5
HY_DECAY_TARGET = 1e-2
SWA_HEADS = 8
SWA_KV_HEADS = 2
SWA_GROUP = SWA_HEADS // SWA_KV_HEADS
SWA_WIDTH = SWA_HEADS * HEAD_DIM
SWA_KV_WIDTH = SWA_KV_HEADS * HEAD_DIM
SWA_WINDOW = 128
SWA_BLOCK = 128
N_BRANCH = 3
D_FF = 4 * D_MODEL
OFF_HY = 3 * NA_WIDTH
OFF_SWA = OFF_HY + 3 * HY_WIDTH
OFF_GATE = OFF_SWA + SWA_WIDTH + 2 * SWA_KV_WIDTH
D_IN = OFF_GATE + N_BRANCH * D_MODEL
DEEPNORM_ALPHA = (2 * DEPTH) ** 0.25
LN_EPS = 1e-5
NEG_BIG = -1e30

P_GATE = 0
P_NA = N_BRANCH * D_MODEL
P_HY = P_NA + 3 * NA_WIDTH
P_SWA = P_HY + 3 * HY_WIDTH

LANE = 128
VMEM_LIMIT_BYTES = 56 * 1024 * 1024

INPROJ_TM = 2048
INPROJ_TN = 768
MERGE_TM = 512
MLP_TM = 512
MLP_FF_CHUNK = 1024
CONV_TL = 256
FILT_TL = 512
FFT_LANE_TILE = 2048


def _cparams(*sem):
    return pltpu.CompilerParams(dimension_semantics=sem, vmem_limit_bytes=VMEM_LIMIT_BYTES)


def _resident(shape):
    nd = len(shape)
    return pl.BlockSpec(shape, lambda *_: (0,) * nd, pipeline_mode=pl.Buffered(1))


def _inproj_kernel(x_ref, w_ref, b_ref, o_ref, xb_ref, *, n_gate_tiles):
    j = pl.program_id(1)

    @pl.when(j == 0)
    def _():
        xb_ref[...] = x_ref[...].astype(BF16)

    def tile(apply_sigmoid):
        acc = jnp.dot(xb_ref[...], w_ref[...], preferred_element_type=F32) + b_ref[...]
        if apply_sigmoid:
            acc = 1.0 / (1.0 + jnp.exp(-acc))
        o_ref[...] = acc.astype(o_ref.dtype)

    pl.when(j < n_gate_tiles)(lambda: tile(True))
    pl.when(j >= n_gate_tiles)(lambda: tile(False))


def _inproj(x2d, w_bf, b_row):
    m = x2d.shape[0]
    tm = min(INPROJ_TM, m)
    tn = INPROJ_TN
    assert m % tm == 0 and D_IN % tn == 0 and P_NA % tn == 0
    return pl.pallas_call(
        functools.partial(_inproj_kernel, n_gate_tiles=P_NA // tn),
        grid=(m // tm, D_IN // tn),
        in_specs=[
            pl.BlockSpec((tm, D_MODEL), lambda i, j: (i, 0)),
            pl.BlockSpec((D_MODEL, tn), lambda i, j: (0, j)),
            pl.BlockSpec((1, tn), lambda i, j: (0, j)),
        ],
        out_specs=pl.BlockSpec((tm, tn), lambda i, j: (i, j)),
        out_shape=jax.ShapeDtypeStruct((m, D_IN), BF16),
        scratch_shapes=[pltpu.VMEM((tm, D_MODEL), BF16)],
        compiler_params=_cparams("parallel", "arbitrary"),
        name="inproj",
    )(x2d, w_bf, b_row)


def _na_bias_table(rpb):
    kr = NA_WIN_ROWS
    delta = np.arange(kr)[:, None]
    i = np.arange(kr)[None, :]
    row_idx = i - delta + (NA_WIN_ROWS - 1)
    w = np.arange(GRID_W)[:, None]
    kc = np.arange(GRID_W)[None, :]
    col_start = np.clip(w - NA_WIN_COLS // 2, 0, GRID_W - NA_WIN_COLS)
    valid = (kc >= col_start) & (kc < col_start + NA_WIN_COLS)
    col_idx = np.clip(kc - w + (NA_WIN_COLS - 1), 0, 2 * NA_WIN_COLS - 2)
    t = rpb.astype(F32)[:, row_idx[:, :, None, None], col_idx[None, None, :, :]]
    t = jnp.where(valid[None, None, None], t, NEG_BIG)
    t = jnp.transpose(t, (1, 0, 3, 2, 4))
    return t.reshape(kr, NA_HEADS, GRID_W, kr * GRID_W)


def _na_kernel(q_ref, k_ref, v_ref, bias_ref, o_ref, *, rows):
    r = pl.program_id(1)
    span = NA_WIN_ROWS * GRID_W
    rs = jnp.clip(r - NA_WIN_ROWS // 2, 0, rows - NA_WIN_ROWS)
    start = pl.multiple_of(rs * GRID_W, GRID_W)
    q = q_ref[0]
    k = k_ref[0, pl.ds(start, span), :]
    v = v_ref[0, pl.ds(start, span), :]
    scale = HEAD_DIM ** -0.5
    for h in range(NA_HEADS):
        sl = slice(h * HEAD_DIM, (h + 1) * HEAD_DIM)
        s = lax.dot_general(q[:, sl], k[:, sl], (((1,), (1,)), ((), ())), preferred_element_type=F32)
        s = s * scale + bias_ref[0, h]
        m = jnp.max(s, axis=-1, keepdims=True)
        p = jnp.exp(s - m)
        l = jnp.sum(p, axis=-1, keepdims=True)
        o = jnp.dot(p.astype(BF16), v[:, sl], preferred_element_type=F32) / l
        o_ref[0, :, sl] = o.astype(o_ref.dtype)


def _na_attention(h3d, bias_table):
    b_sz, seq_len, _ = h3d.shape
    rows = seq_len // GRID_W
    assert rows >= NA_WIN_ROWS
    cb = P_NA // NA_WIDTH
    span = NA_WIN_ROWS * GRID_W

    def delta_of(r):
        return r - jnp.clip(r - NA_WIN_ROWS // 2, 0, rows - NA_WIN_ROWS)

    return pl.pallas_call(
        functools.partial(_na_kernel, rows=rows),
        grid=(b_sz, rows),
        in_specs=[
            pl.BlockSpec((1, GRID_W, NA_WIDTH), lambda b, r: (b, r, cb)),
            pl.BlockSpec((1, seq_len, NA_WIDTH), lambda b, r: (b, 0, cb + 1), pipeline_mode=pl.Buffered(1)),
            pl.BlockSpec((1, seq_len, NA_WIDTH), lambda b, r: (b, 0, cb + 2), pipeline_mode=pl.Buffered(1)),
            pl.BlockSpec((1, NA_HEADS, GRID_W, span), lambda b, r: (delta_of(r), 0, 0, 0)),
        ],
        out_specs=pl.BlockSpec((1, GRID_W, NA_WIDTH), lambda b, r: (b, r, 0)),
        out_shape=jax.ShapeDtypeStruct((b_sz, seq_len, NA_WIDTH), BF16),
        compiler_params=_cparams("parallel", "arbitrary"),
        name="na_attention",
    )(h3d, h3d, h3d, bias_table)


def _swa_table():
    span = SWA_BLOCK + 2 * SWA_WINDOW
    slopes = 2.0 ** (-8.0 * (np.arange(SWA_HEADS, dtype=np.float64) + 1.0) / SWA_HEADS)
    q_off = np.arange(SWA_BLOCK)[:, None]
    k_off = np.arange(span)[None, :] - SWA_WINDOW
    rel = np.abs(k_off - q_off)
    t = -slopes[:, None, None] * rel[None].astype(np.float64)
    t = np.where((rel <= SWA_WINDOW)[None], t, NEG_BIG)
    return jnp.asarray(t, F32)


def _swa_kernel(sink_ref, q_ref, k_ref, v_ref, tab_ref, o_ref, *, nb):
    i = pl.program_id(1)
    blk = SWA_BLOCK
    span = blk + 2 * SWA_WINDOW

    def rows(ref, j):
        return ref[0, pl.ds(pl.multiple_of(j * blk, blk), blk), :]

    ip = jnp.maximum(i - 1, 0)
    inx = jnp.minimum(i + 1, nb - 1)
    k = jnp.concatenate([rows(k_ref, ip), rows(k_ref, i), rows(k_ref, inx)], axis=0)
    v = jnp.concatenate([rows(v_ref, ip), rows(v_ref, i), rows(v_ref, inx)], axis=0)
    col = lax.broadcasted_iota(jnp.int32, (1, span), 1)
    lo = jnp.where(i == 0, SWA_WINDOW, 0)
    hi = jnp.where(i == nb - 1, blk + SWA_WINDOW, span)
    edge = jnp.where((col < lo) | (col >= hi), NEG_BIG, 0.0).astype(F32)
    q = q_ref[0]
    scale = HEAD_DIM ** -0.5
    for h in range(SWA_HEADS):
        g = h // SWA_GROUP
        sl = slice(h * HEAD_DIM, (h + 1) * HEAD_DIM)
        gl = slice(g * HEAD_DIM, (g + 1) * HEAD_DIM)
        s = lax.dot_general(q[:, sl], k[:, gl], (((1,), (1,)), ((), ())), preferred_element_type=F32)
        s = s * scale + tab_ref[h] + edge
        sink = sink_ref[h]
        m = jnp.maximum(jnp.max(s, axis=-1, keepdims=True), sink)
        p = jnp.exp(s - m)
        denom = jnp.sum(p, axis=-1, keepdims=True) + jnp.exp(sink - m)
        o = jnp.dot(p.astype(BF16), v[:, gl], preferred_element_type=F32) / denom
        o_ref[0, :, sl] = o.astype(o_ref.dtype)


def _swa_attention(h3d, sink, table):
    b_sz, seq_len, _ = h3d.shape
    nb = seq_len // SWA_BLOCK
    span = SWA_BLOCK + 2 * SWA_WINDOW
    qb = P_SWA // SWA_WIDTH
    kb = (P_SWA + SWA_WIDTH) // SWA_KV_WIDTH
    return pl.pallas_call(
        functools.partial(_swa_kernel, nb=nb),
        grid=(b_sz, nb),
        in_specs=[
            pl.BlockSpec(memory_space=pltpu.SMEM),
            pl.BlockSpec((1, SWA_BLOCK, SWA_WIDTH), lambda b, i: (b, i, qb)),
            pl.BlockSpec((1, seq_len, SWA_KV_WIDTH), lambda b, i: (b, 0, kb)),
            pl.BlockSpec((1, seq_len, SWA_KV_WIDTH), lambda b, i: (b, 0, kb + 1)),
            _resident((SWA_HEADS, SWA_BLOCK, span)),
        ],
        out_specs=pl.BlockSpec((1, SWA_BLOCK, SWA_WIDTH), lambda b, i: (b, i, 0)),
        out_shape=jax.ShapeDtypeStruct((b_sz, seq_len, SWA_WIDTH), BF16),
        compiler_params=_cparams("parallel", "arbitrary"),
        name="swa_attention",
    )(sink.astype(F32), h3d, h3d, h3d, table)


def _conv_kernel(x_ref, prev_ref, next_ref, w_ref, b_ref, v_ref, x1_ref, x2_ref, *, n_tiles):
    i = pl.program_id(1)
    tl = x_ref.shape[1]
    halo = prev_ref.shape[1]
    row = lax.broadcasted_iota(jnp.int32, (tl, 1), 0)
    outs = (v_ref, x1_ref, x2_ref)
    for c in range(3):
        sl = slice(c * HY_WIDTH, (c + 1) * HY_WIDTH)
        x = x_ref[0, :, sl].astype(F32)
        prev_row = jnp.where(i > 0, prev_ref[0, halo - 1:halo, sl].astype(F32), 0.0)
        next_row = jnp.where(i < n_tiles - 1, next_ref[0, 0:1, sl].astype(F32), 0.0)
        xm = jnp.where(row == 0, prev_row, pltpu.roll(x, 1, axis=0))
        xp = jnp.where(row == tl - 1, next_row, pltpu.roll(x, tl - 1, axis=0))
        u = xm * w_ref[0:1, sl] + x * w_ref[1:2, sl] + xp * w_ref[2:3, sl] + b_ref[:, sl]
        outs[c][0] = u.astype(BF16)


def _short_conv(h3d, conv_w, conv_b):
    b_sz, seq_len, _ = h3d.shape
    tl = CONV_TL
    halo = 16
    n_tiles = seq_len // tl
    width = 3 * HY_WIDTH
    cb = P_HY // width
    assert P_HY % width == 0 and seq_len % tl == 0
    per = tl // halo
    out = jax.ShapeDtypeStruct((b_sz, seq_len, HY_WIDTH), BF16)
    ospec = pl.BlockSpec((1, tl, HY_WIDTH), lambda b, i: (b, i, 0))
    return pl.pallas_call(
        functools.partial(_conv_kernel, n_tiles=n_tiles),
        grid=(b_sz, n_tiles),
        in_specs=[
            pl.BlockSpec((1, tl, width), lambda b, i: (b, i, cb)),
            pl.BlockSpec((1, halo, width), lambda b, i: (b, jnp.maximum(i * per - 1, 0), cb)),
            pl.BlockSpec((1, halo, width), lambda b, i: (b, jnp.minimum((i + 1) * per, seq_len // halo - 1), cb)),
            _resident((HY_SHORT_CONV, width)),
            _resident((1, width)),
        ],
        out_specs=[ospec, ospec, ospec],
        out_shape=[out, out, out],
        compiler_params=_cparams("parallel", "arbitrary"),
        name="hyena_short_conv",
    )(h3d, h3d, h3d, conv_w.astype(F32), conv_b.astype(F32).reshape(1, width))


def _filter_embedding(seq_len):
    t = np.linspace(0.0, 1.0, seq_len, dtype=np.float32).astype(np.float64)[:, None]
    w = (2.0 * math.pi * np.arange(seq_len, dtype=np.float32) / seq_len).astype(np.float32)
    bands = np.linspace(1e-4, HY_POS_BANDS - 1, HY_POS_BANDS, dtype=np.float32)
    ang = (w[:, None] * bands[None, :]).astype(np.float32).astype(np.float64)
    z = np.concatenate([t, np.cos(ang), -np.sin(ang)], axis=-1)
    zp = np.zeros((seq_len, HY_EMB_PAD), np.float32)
    zp[:, :HY_EMB_DIM] = z
    return zp


def _filter_deltas():
    min_decay = math.log(HY_DECAY_TARGET) / HY_SLOW_DECAY_PCT
    max_decay = math.log(HY_DECAY_TARGET) / HY_FAST_DECAY_PCT
    return np.abs(np.linspace(min_decay, max_decay, HY_WIDTH, dtype=np.float32))[None, :]


def _filter_kernel(z_ref, w1_ref, b1_ref, w2_ref, b2_ref, fr_ref, w3_ref, dl_ref, o_ref):
    hi = lax.Precision.HIGHEST
    tl = z_ref.shape[0]
    z = z_ref[...]
    fr = fr_ref[...]
    h = jnp.sin(fr * (jnp.dot(z, w1_ref[...], precision=hi, preferred_element_type=F32) + b1_ref[...]))
    h = jnp.sin(fr * (jnp.dot(h, w2_ref[...], precision=hi, preferred_element_type=F32) + b2_ref[...]))
    t = z[:, 0:1]
    window = jnp.exp(-t * dl_ref[...])
    row = pl.program_id(0) * tl + lax.broadcasted_iota(jnp.int32, (tl, 1), 0)
    for s in range(2 * HY_ORDER):
        sl = slice(s * HY_WIDTH, (s + 1) * HY_WIDTH)
        f = jnp.dot(h, w3_ref[:, sl], precision=hi, preferred_element_type=F32) * window
        if s % 2 == 1:
            f = jnp.where(row > 0, f, 0.0)
        o_ref[s] = f.astype(o_ref.dtype)


def _hyena_filters(seq_len, w1, b1, w2, b2, freq, w3):
    tl = min(FILT_TL, seq_len)
    z = jnp.asarray(_filter_embedding(seq_len))
    w1p = jnp.zeros((HY_EMB_PAD, HY_FILTER_HIDDEN), F32).at[:HY_EMB_DIM].set(w1.astype(F32))
    hid = HY_FILTER_HIDDEN
    n_f = 2 * HY_ORDER
    return pl.pallas_call(
        _filter_kernel,
        grid=(seq_len // tl,),
        in_specs=[
            pl.BlockSpec((tl, HY_EMB_PAD), lambda i: (i, 0)),
            _resident((HY_EMB_PAD, hid)), _resident((1, hid)),
            _resident((hid, hid)), _resident((1, hid)), _resident((1, hid)),
            _resident((hid, n_f * HY_WIDTH)), _resident((1, HY_WIDTH)),
        ],
        out_specs=pl.BlockSpec((n_f, tl, HY_WIDTH), lambda i: (0, i, 0)),
        out_shape=jax.ShapeDtypeStruct((n_f, seq_len, HY_WIDTH), BF16),
        compiler_params=_cparams("parallel"),
        name="hyena_filters",
    )(z, w1p, b1.astype(F32).reshape(1, hid), w2.astype(F32), b2.astype(F32).reshape(1, hid),
      freq.astype(F32).reshape(1, hid), w3.astype(F32), jnp.asarray(_filter_deltas()))


def _fft_dims(seq_len):
    n = 2 * seq_len
    n1 = 1 << ((n.bit_length() - 1) // 2)
    n2 = n // n1
    assert n1 * n2 == n and n1 == n2, "sequence length must give a square transform"
    return n1, n2


def _stack_complex(fr, fi):
    return np.block([[fr, -fi], [fi, fr]])


@functools.lru_cache(maxsize=None)
def _dft_constants(seq_len):
    n1, n2 = _fft_dims(seq_len)
    n = n1 * n2
    k1 = np.arange(n1)[:, None].astype(np.float64)
    t1 = np.arange(n1 // 2)[None, :].astype(np.float64)
    a1 = -2.0 * math.pi * k1 * t1 / n1
    f1r, f1i = np.cos(a1), np.sin(a1)
    w1_complex = _stack_complex(f1r, f1i)
    w1_real = np.concatenate([f1r, f1i], axis=0)
    k2 = np.arange(n2)[:, None].astype(np.float64)
    t2 = np.arange(n2)[None, :].astype(np.float64)
    a2 = -2.0 * math.pi * k2 * t2 / n2
    f2r, f2i = np.cos(a2), np.sin(a2)
    w2 = _stack_complex(f2r, f2i)
    w2_inv = _stack_complex(f2r, -f2i)
    at = -2.0 * math.pi * k1 * t2 / n
    tw = np.stack([np.cos(at), np.sin(at)], axis=0)[..., None]
    tw = np.broadcast_to(tw, (2, n1, n2, LANE)).astype(np.float32)
    g1r, g1i = f1r.T / n, -f1i.T / n
    w3 = _stack_complex(g1r, g1i)
    as_bf = lambda a: jnp.asarray(a, F32).astype(BF16)
    return dict(w1_complex=as_bf(w1_complex), w1_real=as_bf(w1_real), w2=as_bf(w2), w2_inv=as_bf(w2_inv),
                tw=jnp.asarray(tw), w3=as_bf(w3))


def _s1_kernel(w_ref, x_ref, o_ref):
    tl = x_ref.shape[-1]
    x = x_ref[...].reshape(-1, tl).astype(BF16)
    o_ref[0] = jnp.dot(w_ref[...], x, preferred_element_type=F32)


def _fft_stage1(u, w1, *, pair):
    s, seq_len, c = u.shape
    n1, n2 = _fft_dims(seq_len)
    per = 2 if pair else 1
    lanes = n2 * c
    tl = min(FFT_LANE_TILE, lanes)
    uv = u.reshape(s, n1 // 2, lanes)
    out = pl.pallas_call(
        _s1_kernel,
        grid=(s // per, lanes // tl),
        in_specs=[
            _resident(w1.shape),
            pl.BlockSpec((per, n1 // 2, tl), lambda p, j: (p, 0, j)),
        ],
        out_specs=pl.BlockSpec((1, 2 * n1, tl), lambda p, j: (p, 0, j)),
        out_shape=jax.ShapeDtypeStruct((s // per, 2 * n1, lanes), F32),
        compiler_params=_cparams("parallel", "arbitrary"),
        name="hyena_dft_stage1",
    )(w1, uv)
    return out.reshape(s // per, 2, n1, n2, c)


def _lane_tile(x, c):
    return jnp.concatenate([x] * (c // LANE), axis=-1)


def _twiddled_stack(a_ref, twr, twi):
    ar = a_ref[0, 0, 0]
    ai = a_ref[0, 1, 0]
    return jnp.concatenate([ar * twr - ai * twi, ar * twi + ai * twr], axis=0).astype(BF16)


def _s2_kernel(a_ref, k_ref, tw_ref, w2_ref, w2i_ref, o_ref):
    n2, c = a_ref.shape[-2:]
    twr = _lane_tile(tw_ref[0, 0], c)
    twi = _lane_tile(tw_ref[1, 0], c)
    spec = jnp.dot(w2_ref[...], _twiddled_stack(a_ref, twr, twi), preferred_element_type=F32)
    sr, si = spec[:n2], spec[n2:]
    kr = k_ref[0, 0]
    ki = k_ref[1, 0]
    prod = jnp.concatenate([sr * kr - si * ki, sr * ki + si * kr], axis=0).astype(BF16)
    back = jnp.dot(w2i_ref[...], prod, preferred_element_type=F32)
    br, bi = back[:n2], back[n2:]
    o_ref[0, 0, 0] = br * twr + bi * twi
    o_ref[0, 1, 0] = bi * twr - br * twi


def _fft_stage2(a, kf, consts):
    p, _, n1, n2, c = a.shape
    blk = pl.BlockSpec((1, 2, 1, n2, c), lambda k, q: (q, 0, k, 0, 0))
    return pl.pallas_call(
        _s2_kernel,
        grid=(n1, p),
        in_specs=[
            blk,
            pl.BlockSpec((2, 1, n2, c), lambda k, q: (0, k, 0, 0)),
            pl.BlockSpec((2, 1, n2, LANE), lambda k, q: (0, k, 0, 0)),
            _resident((2 * n2, 2 * n2)),
            _resident((2 * n2, 2 * n2)),
        ],
        out_specs=blk,
        out_shape=jax.ShapeDtypeStruct(a.shape, F32),
        compiler_params=_cparams("parallel", "arbitrary"),
        name="hyena_dft_stage2",
    )(a, kf, consts["tw"], consts["w2"], consts["w2_inv"])


def _s2f_kernel(af_ref, ab_ref, tw_ref, w2_ref, o_ref):
    n2, c = af_ref.shape[-2:]
    twr = _lane_tile(tw_ref[0, 0], c)
    twi = _lane_tile(tw_ref[1, 0], c)
    hf = jnp.dot(w2_ref[...], _twiddled_stack(af_ref, twr, twi), preferred_element_type=F32)
    hb = jnp.dot(w2_ref[...], _twiddled_stack(ab_ref, twr, twi), preferred_element_type=F32)
    o_ref[0, 0, 0] = hf[:n2] + hb[:n2]
    o_ref[0, 1, 0] = hf[n2:] - hb[n2:]


def _filter_spectrum(a_filt, consts):
    _, _, n1, n2, c = a_filt.shape
    return pl.pallas_call(
        _s2f_kernel,
        grid=(HY_ORDER, n1),
        in_specs=[
            pl.BlockSpec((1, 2, 1, n2, c), lambda o, k: (2 * o, 0, k, 0, 0)),
            pl.BlockSpec((1, 2, 1, n2, c), lambda o, k: (2 * o + 1, 0, k, 0, 0)),
            pl.BlockSpec((2, 1, n2, LANE), lambda o, k: (0, k, 0, 0)),
            _resident((2 * n2, 2 * n2)),
        ],
        out_specs=pl.BlockSpec((1, 2, 1, n2, c), lambda o, k: (o, 0, k, 0, 0)),
        out_shape=jax.ShapeDtypeStruct((HY_ORDER, 2, n1, n2, c), F32),
        compiler_params=_cparams("parallel", "arbitrary"),
        name="hyena_filter_spectrum",
    )(a_filt, a_filt, consts["tw"], consts["w2"])


def _s3_kernel(w_ref, b_ref, z_ref, g_ref, bias_ref, o_ref):
    tl = b_ref.shape[-1]
    y = jnp.dot(w_ref[...], b_ref[0].astype(BF16), preferred_element_type=F32)
    z = z_ref[...].reshape(-1, tl).astype(F32)
    g = g_ref[...].reshape(-1, tl).astype(F32)
    out = g * (y + bias_ref[...] * z)
    o_ref[...] = out.reshape(o_ref.shape).astype(o_ref.dtype)


def _fft_stage3(bv, z, gate, bias, w3):
    s, seq_len, c = z.shape
    n1, n2 = _fft_dims(seq_len)
    lanes = n2 * c
    tl = min(FFT_LANE_TILE, lanes)
    seq_blk = pl.BlockSpec((2, n1 // 2, tl), lambda p, j: (p, 0, j))
    out = pl.pallas_call(
        _s3_kernel,
        grid=(s // 2, lanes // tl),
        in_specs=[
            _resident(w3.shape),
            pl.BlockSpec((1, 2 * n1, tl), lambda p, j: (p, 0, j)),
            seq_blk, seq_blk,
            _resident((1, tl)),
        ],
        out_specs=seq_blk,
        out_shape=jax.ShapeDtypeStruct((s, n1 // 2, lanes), BF16),
        compiler_params=_cparams("parallel", "arbitrary"),
        name="hyena_dft_stage3",
    )(w3, bv.reshape(s // 2, 2 * n1, lanes), z.reshape(s, n1 // 2, lanes), gate.reshape(s, n1 // 2, lanes),
      jnp.tile(bias.astype(F32), tl // c).reshape(1, tl))
    return out.reshape(s, seq_len, c)


def _hyena_spectra(seq_len, w1, b1, w2, b2, freq, w3):
    consts = _dft_constants(seq_len)
    filt = _hyena_filters(seq_len, w1, b1, w2, b2, freq, w3)
    a_filt = _fft_stage1(filt, consts["w1_real"], pair=False)
    return _filter_spectrum(a_filt, consts)


def _hyena(h3d, conv_w, conv_b, kf, hy_bias):
    seq_len = h3d.shape[1]
    consts = _dft_constants(seq_len)
    v, x1, x2 = _short_conv(h3d, conv_w, conv_b)
    z = v
    for n, gate in enumerate((x1, x2)):
        a = _fft_stage1(z, consts["w1_complex"], pair=True)
        bv = _fft_stage2(a, kf[n], consts)
        z = _fft_stage3(bv, z, gate, hy_bias[n], consts["w3"])
    return z


def _layernorm(y, g, b):
    mu = jnp.mean(y, axis=-1, keepdims=True)
    d = y - mu
    var = jnp.mean(d * d, axis=-1, keepdims=True)
    return d * lax.rsqrt(var + LN_EPS) * g + b


def _merge_kernel(g_ref, a_ref, hb_ref, c_ref, x_ref, wa_ref, wb_ref, wc_ref, wo_ref, lg_ref, lb_ref, o_ref):
    d = D_MODEL
    merged = g_ref[:, 0:d].astype(F32) * jnp.dot(a_ref[...], wa_ref[...], preferred_element_type=F32)
    merged += g_ref[:, d:2 * d].astype(F32) * jnp.dot(hb_ref[...], wb_ref[...], preferred_element_type=F32)
    merged += g_ref[:, 2 * d:3 * d].astype(F32) * jnp.dot(c_ref[...], wc_ref[...], preferred_element_type=F32)
    mix = jnp.dot(merged.astype(BF16), wo_ref[...], preferred_element_type=F32)
    o_ref[...] = _layernorm(DEEPNORM_ALPHA * x_ref[...] + mix, lg_ref[...], lb_ref[...])


def _merge(h2d, a, hb, c, x2d, wa, wb, wc, wo, ln_g, ln_b):
    m = x2d.shape[0]
    tm = MERGE_TM
    row = lambda width: pl.BlockSpec((tm, width), lambda i: (i, 0))
    return pl.pallas_call(
        _merge_kernel,
        grid=(m // tm,),
        in_specs=[
            row(N_BRANCH * D_MODEL), row(NA_WIDTH), row(HY_WIDTH), row(SWA_WIDTH), row(D_MODEL),
            _resident((NA_WIDTH, D_MODEL)), _resident((HY_WIDTH, D_MODEL)), _resident((SWA_WIDTH, D_MODEL)),
            _resident((D_MODEL, D_MODEL)), _resident((1, D_MODEL)), _resident((1, D_MODEL)),
        ],
        out_specs=row(D_MODEL),
        out_shape=jax.ShapeDtypeStruct((m, D_MODEL), F32),
        compiler_params=_cparams("parallel"),
        name="merge_ln",
    )(h2d, a, hb, c, x2d, wa, wb, wc, wo, ln_g, ln_b)


def _mlp_kernel(x_ref, wu_ref, bu_ref, wd_ref, bd_ref, lg_ref, lb_ref, o_ref):
    x = x_ref[...]
    xb = x.astype(BF16)
    acc = DEEPNORM_ALPHA * x + bd_ref[...]
    for c in range(D_FF // MLP_FF_CHUNK):
        sl = slice(c * MLP_FF_CHUNK, (c + 1) * MLP_FF_CHUNK)
        up = jnp.dot(xb, wu_ref[:, sl], preferred_element_type=F32) + bu_ref[:, sl]
        up = jnp.square(jnp.maximum(up, 0.0))
        acc += jnp.dot(up.astype(BF16), wd_ref[sl, :], preferred_element_type=F32)
    o_ref[...] = _layernorm(acc, lg_ref[...], lb_ref[...])


def _mlp(x2d, wu, bu, wd, bd, ln_g, ln_b):
    m = x2d.shape[0]
    tm = MLP_TM
    row = pl.BlockSpec((tm, D_MODEL), lambda i: (i, 0))
    return pl.pallas_call(
        _mlp_kernel,
        grid=(m // tm,),
        in_specs=[
            row, _resident((D_MODEL, D_FF)), _resident((1, D_FF)), _resident((D_FF, D_MODEL)),
            _resident((1, D_MODEL)), _resident((1, D_MODEL)), _resident((1, D_MODEL)),
        ],
        out_specs=row,
        out_shape=jax.ShapeDtypeStruct((m, D_MODEL), F32),
        compiler_params=_cparams("parallel"),
        name="mlp_ln",
    )(x2d, wu, bu, wd, bd, ln_g, ln_b)


def _permute_in_columns(a):
    return jnp.concatenate([a[..., OFF_GATE:], a[..., :OFF_GATE]], axis=-1)


def _prepare_layer(l, p):
    row = lambda a: a.astype(F32).reshape(1, -1)
    return dict(
        w_in=_permute_in_columns(p["w_in"][l]).astype(BF16),
        b_in=row(_permute_in_columns(p["b_in"][l])),
        conv_w=p["hy_conv_w"][l], conv_b=p["hy_conv_b"][l], hy_bias=p["hy_bias"][l],
        filt=(p["hy_filt_w1"][l], p["hy_filt_b1"][l], p["hy_filt_w2"][l], p["hy_filt_b2"][l],
              p["hy_filt_freq"][l], p["hy_filt_w3"][l]),
        na_bias=_na_bias_table(p["na_rpb"][l]),
        sink=p["swa_sink"][l],
        wa=p["w_branch_a"][l].astype(BF16), wb=p["w_branch_b"][l].astype(BF16),
        wc=p["w_branch_c"][l].astype(BF16), wo=p["w_out"][l].astype(BF16),
        ln1_g=row(p["ln1_g"][l]), ln1_b=row(p["ln1_b"][l]),
        wu=p["w_up"][l].astype(BF16), bu=row(p["b_up"][l]),
        wd=p["w_down"][l].astype(BF16), bd=row(p["b_down"][l]),
        ln2_g=row(p["ln2_g"][l]), ln2_b=row(p["ln2_b"][l]),
    )


def _encoder_block(x, lp, swa_table):
    b_sz, seq_len, _ = x.shape
    m = b_sz * seq_len
    x2d = x.reshape(m, D_MODEL)
    h2d = _inproj(x2d, lp["w_in"], lp["b_in"])
    h3d = h2d.reshape(b_sz, seq_len, D_IN)
    a = _na_attention(h3d, lp["na_bias"])
    kf = _hyena_spectra(seq_len, *lp["filt"])
    hb = _hyena(h3d, lp["conv_w"], lp["conv_b"], kf, lp["hy_bias"])
    c = _swa_attention(h3d, lp["sink"], swa_table)
    x1 = _merge(h2d, a.reshape(m, NA_WIDTH), hb.reshape(m, HY_WIDTH), c.reshape(m, SWA_WIDTH), x2d,
                lp["wa"], lp["wb"], lp["wc"], lp["wo"], lp["ln1_g"], lp["ln1_b"])
    x2 = _mlp(x1, lp["wu"], lp["bu"], lp["wd"], lp["bd"], lp["ln2_g"], lp["ln2_b"])
    return x2.reshape(b_sz, seq_len, D_MODEL)


def kernel(x_prompt, x_sample, w_in, b_in, hy_conv_w, hy_conv_b, hy_filt_w1, hy_filt_b1, hy_filt_w2,
           hy_filt_b2, hy_filt_freq, hy_filt_w3, hy_bias, na_rpb, swa_sink, w_branch_a, w_branch_b,
           w_branch_c, w_out, ln1_g, ln1_b, w_up, b_up, w_down, b_down, ln2_g, ln2_b):
    params = dict(w_in=w_in, b_in=b_in, hy_conv_w=hy_conv_w, hy_conv_b=hy_conv_b, hy_filt_w1=hy_filt_w1,
                  hy_filt_b1=hy_filt_b1, hy_filt_w2=hy_filt_w2, hy_filt_b2=hy_filt_b2,
                  hy_filt_freq=hy_filt_freq, hy_filt_w3=hy_filt_w3, hy_bias=hy_bias, na_rpb=na_rpb,
                  swa_sink=swa_sink, w_branch_a=w_branch_a, w_branch_b=w_branch_b, w_branch_c=w_branch_c,
                  w_out=w_out, ln1_g=ln1_g, ln1_b=ln1_b, w_up=w_up, b_up=b_up, w_down=w_down,
                  b_down=b_down, ln2_g=ln2_g, ln2_b=ln2_b)
    swa_table = _swa_table()
    y_prompt = x_prompt
    y_sample = x_sample
    for l in range(DEPTH):
        lp = _prepare_layer(l, params)
        y_prompt = _encoder_block(y_prompt, lp, swa_table)
        y_sample = _encoder_block(y_sample, lp, swa_table)
    return (y_prompt, y_sample)
```

```python
import functools
import math

import numpy as np
import jax
import jax.numpy as jnp
from jax import lax
from jax.experimental import pallas as pl
from jax.experimental.pallas import tpu as pltpu

F32 = jnp.float32
BF16 = jnp.bfloat16

D_MODEL = 1024
DEPTH = 2
HEAD_DIM = 64
GRID_W = 64
NA_HEADS = 8
NA_WIN_ROWS = 8
NA_WIN_COLS = 16
NA_WIDTH = NA_HEADS * HEAD_DIM
HY_WIDTH = D_MODEL // 2
HY_ORDER = 2
HY_SHORT_CONV = 3
HY_POS_BANDS = 16
HY_EMB_DIM = 1 + 2 * HY_POS_BANDS
HY_EMB_PAD = 128
HY_FILTER_HIDDEN = 64
HY_FAST_DECAY_PCT = 0.3
HY_SLOW_DECAY_PCT = 1.5
HY_DECAY_TARGET = 1e-2
SWA_HEADS = 8
SWA_KV_HEADS = 2
SWA_GROUP = SWA_HEADS // SWA_KV_HEADS
SWA_WIDTH = SWA_HEADS * HEAD_DIM
SWA_KV_WIDTH = SWA_KV_HEADS * HEAD_DIM
SWA_WINDOW = 128
SWA_BLOCK = 128
N_BRANCH = 3
D_FF = 4 * D_MODEL
OFF_HY = 3 * NA_WIDTH
OFF_SWA = OFF_HY + 3 * HY_WIDTH
OFF_GATE = OFF_SWA + SWA_WIDTH + 2 * SWA_KV_WIDTH
D_IN = OFF_GATE + N_BRANCH * D_MODEL
DEEPNORM_ALPHA = (2 * DEPTH) ** 0.25
LN_EPS = 1e-5
NEG_BIG = -1e30
LOG2E = 1.4426950408889634

P_GATE = 0
P_NA = N_BRANCH * D_MODEL
P_HY = P_NA + 3 * NA_WIDTH
P_SWA = P_HY + 3 * HY_WIDTH

LANE = 128
VMEM_LIMIT_BYTES = 56 * 1024 * 1024

NA_ROWS_PER_STEP = 1
INPROJ_TM = 2048
INPROJ_TN = 768
MERGE_TM = 512
MLP_TM = 512
MLP_FF_CHUNK = 1024
CONV_TL = 256
FILT_TL = 512
FFT_LANE_TILE = 2048


def _cparams(*sem):
    return pltpu.CompilerParams(dimension_semantics=sem, vmem_limit_bytes=VMEM_LIMIT_BYTES)


def _resident(shape):
    nd = len(shape)
    return pl.BlockSpec(shape, lambda *_: (0,) * nd, pipeline_mode=pl.Buffered(1))


def _inproj_kernel(x_ref, w_ref, b_ref, o_ref, xb_ref, *, n_gate_tiles):
    j = pl.program_id(1)

    @pl.when(j == 0)
    def _():
        xb_ref[...] = x_ref[...].astype(BF16)

    def tile(apply_sigmoid):
        acc = jnp.dot(xb_ref[...], w_ref[...], preferred_element_type=F32) + b_ref[...]
        if apply_sigmoid:
            acc = 1.0 / (1.0 + jnp.exp(-acc))
        o_ref[...] = acc.astype(o_ref.dtype)

    pl.when(j < n_gate_tiles)(lambda: tile(True))
    pl.when(j >= n_gate_tiles)(lambda: tile(False))


def _inproj(x2d, w_bf, b_row):
    m = x2d.shape[0]
    tm = min(INPROJ_TM, m)
    tn = INPROJ_TN
    assert m % tm == 0 and D_IN % tn == 0 and P_NA % tn == 0
    return pl.pallas_call(
        functools.partial(_inproj_kernel, n_gate_tiles=P_NA // tn),
        grid=(m // tm, D_IN // tn),
        in_specs=[
            pl.BlockSpec((tm, D_MODEL), lambda i, j: (i, 0)),
            pl.BlockSpec((D_MODEL, tn), lambda i, j: (0, j)),
            pl.BlockSpec((1, tn), lambda i, j: (0, j)),
        ],
        out_specs=pl.BlockSpec((tm, tn), lambda i, j: (i, j)),
        out_shape=jax.ShapeDtypeStruct((m, D_IN), BF16),
        scratch_shapes=[pltpu.VMEM((tm, D_MODEL), BF16)],
        compiler_params=_cparams("parallel", "arbitrary"),
        name="inproj",
    )(x2d, w_bf, b_row)


def _na_bias_table(rpb):
    kr = NA_WIN_ROWS
    pad = GRID_W - NA_WIN_COLS
    p = jnp.pad(rpb.astype(F32) * LOG2E, ((0, 0), (0, 0), (pad, pad)))
    cols = jnp.stack([p[:, :, GRID_W - 1 - w:2 * GRID_W - 1 - w] for w in range(GRID_W)], axis=2)
    t = jnp.stack([cols[:, kr - 1 - d:2 * kr - 1 - d] for d in range(kr)], axis=0)
    w = np.arange(GRID_W)[:, None]
    kc = np.arange(GRID_W)[None, :]
    col_start = np.clip(w - NA_WIN_COLS // 2, 0, GRID_W - NA_WIN_COLS)
    valid = (kc >= col_start) & (kc < col_start + NA_WIN_COLS)
    t = jnp.where(valid[None, None, None], t, NEG_BIG)
    t = jnp.transpose(t, (0, 1, 3, 2, 4))
    return t.reshape(kr, NA_HEADS, GRID_W, kr * GRID_W)


def _softmax_pv(scores, values, extra=None):
    ps, ls = [], []
    for h, s in enumerate(scores):
        m = jnp.max(s, axis=-1, keepdims=True)
        if extra is not None:
            m = jnp.maximum(m, extra[h])
        p = jnp.exp2(s - m)
        l = jnp.sum(p, axis=-1, keepdims=True)
        if extra is not None:
            l = l + jnp.exp2(extra[h] - m)
        ps.append(p.astype(BF16))
        ls.append(l)
    return [jnp.dot(p, v, preferred_element_type=F32) / l for p, v, l in zip(ps, values, ls)]


def _na_kernel(q_ref, k_ref, v_ref, *rest, rows, rows_per_step):
    bias_refs, o_ref = rest[:-1], rest[-1]
    span = NA_WIN_ROWS * GRID_W
    scale = HEAD_DIM ** -0.5 * LOG2E
    heads = [slice(h * HEAD_DIM, (h + 1) * HEAD_DIM) for h in range(NA_HEADS)]
    scores, values = [], []
    for j in range(rows_per_step):
        r = pl.program_id(1) * rows_per_step + j
        rs = jnp.clip(r - NA_WIN_ROWS // 2, 0, rows - NA_WIN_ROWS)
        start = pl.multiple_of(rs * GRID_W, GRID_W)
        q = q_ref[0, j * GRID_W:(j + 1) * GRID_W, :]
        k = k_ref[0, pl.ds(start, span), :]
        v = v_ref[0, pl.ds(start, span), :]
        for h, sl in enumerate(heads):
            s = lax.dot_general(q[:, sl], k[:, sl], (((1,), (1,)), ((), ())), preferred_element_type=F32)
            scores.append(s * scale + bias_refs[j][0, h])
            values.append(v[:, sl])
    outs = _softmax_pv(scores, values)
    for j in range(rows_per_step):
        for h, sl in enumerate(heads):
            o_ref[0, j * GRID_W:(j + 1) * GRID_W, sl] = outs[j * NA_HEADS + h].astype(o_ref.dtype)


def _na_attention(h3d, bias_table):
    b_sz, seq_len, _ = h3d.shape
    rows = seq_len // GRID_W
    assert rows >= NA_WIN_ROWS
    cb = P_NA // NA_WIDTH
    span = NA_WIN_ROWS * GRID_W

    def delta_of(r):
        return r - jnp.clip(r - NA_WIN_ROWS // 2, 0, rows - NA_WIN_ROWS)

    rps = NA_ROWS_PER_STEP
    assert rows % rps == 0
    bias_specs = [pl.BlockSpec((1, NA_HEADS, GRID_W, span), lambda b, r, j=j: (delta_of(r * rps + j), 0, 0, 0))
                  for j in range(rps)]
    return pl.pallas_call(
        functools.partial(_na_kernel, rows=rows, rows_per_step=rps),
        grid=(b_sz, rows // rps),
        in_specs=[
            pl.BlockSpec((1, rps * GRID_W, NA_WIDTH), lambda b, r: (b, r, cb)),
            pl.BlockSpec((1, seq_len, NA_WIDTH), lambda b, r: (b, 0, cb + 1), pipeline_mode=pl.Buffered(1)),
            pl.BlockSpec((1, seq_len, NA_WIDTH), lambda b, r: (b, 0, cb + 2), pipeline_mode=pl.Buffered(1)),
        ] + bias_specs,
        out_specs=pl.BlockSpec((1, rps * GRID_W, NA_WIDTH), lambda b, r: (b, r, 0)),
        out_shape=jax.ShapeDtypeStruct((b_sz, seq_len, NA_WIDTH), BF16),
        compiler_params=_cparams("parallel", "arbitrary"),
        name="na_attention",
    )(h3d, h3d, h3d, *([bias_table] * rps))


def _swa_table():
    span = SWA_BLOCK + 2 * SWA_WINDOW
    slopes = 2.0 ** (-8.0 * (np.arange(SWA_HEADS, dtype=np.float64) + 1.0) / SWA_HEADS)
    q_off = np.arange(SWA_BLOCK)[:, None]
    k_off = np.arange(span)[None, :] - SWA_WINDOW
    rel = np.abs(k_off - q_off)
    t = -slopes[:, None, None] * rel[None].astype(np.float64) * LOG2E
    t = np.where((rel <= SWA_WINDOW)[None], t, NEG_BIG)
    return jnp.asarray(t, F32)


def _swa_kernel(sink_ref, q_ref, k_ref, v_ref, tab_ref, o_ref, *, nb):
    i = pl.program_id(1)
    blk = SWA_BLOCK
    span = blk + 2 * SWA_WINDOW

    def rows(ref, j):
        return ref[0, pl.ds(pl.multiple_of(j * blk, blk), blk), :]

    ip = jnp.maximum(i - 1, 0)
    inx = jnp.minimum(i + 1, nb - 1)
    k = jnp.concatenate([rows(k_ref, ip), rows(k_ref, i), rows(k_ref, inx)], axis=0)
    v = jnp.concatenate([rows(v_ref, ip), rows(v_ref, i), rows(v_ref, inx)], axis=0)
    col = lax.broadcasted_iota(jnp.int32, (1, span), 1)
    lo = jnp.where(i == 0, SWA_WINDOW, 0)
    hi = jnp.where(i == nb - 1, blk + SWA_WINDOW, span)
    edge = jnp.where((col < lo) | (col >= hi), NEG_BIG, 0.0).astype(F32)
    q = q_ref[0]
    scale = HEAD_DIM ** -0.5 * LOG2E
    heads = [slice(h * HEAD_DIM, (h + 1) * HEAD_DIM) for h in range(SWA_HEADS)]
    kv_heads = [slice((h // SWA_GROUP) * HEAD_DIM, (h // SWA_GROUP + 1) * HEAD_DIM) for h in range(SWA_HEADS)]
    scores = [lax.dot_general(q[:, sl], k[:, gl], (((1,), (1,)), ((), ())), preferred_element_type=F32)
              * scale + tab_ref[h] + edge for h, (sl, gl) in enumerate(zip(heads, kv_heads))]
    sinks = [sink_ref[h] * LOG2E for h in range(SWA_HEADS)]
    outs = _softmax_pv(scores, [v[:, gl] for gl in kv_heads], extra=sinks)
    for sl, o in zip(heads, outs):
        o_ref[0, :, sl] = o.astype(o_ref.dtype)


def _swa_attention(h3d, sink, table):
    b_sz, seq_len, _ = h3d.shape
    nb = seq_len // SWA_BLOCK
    span = SWA_BLOCK + 2 * SWA_WINDOW
    qb = P_SWA // SWA_WIDTH
    kb = (P_SWA + SWA_WIDTH) // SWA_KV_WIDTH
    return pl.pallas_call(
        functools.partial(_swa_kernel, nb=nb),
        grid=(b_sz, nb),
        in_specs=[
            pl.BlockSpec(memory_space=pltpu.SMEM),
            pl.BlockSpec((1, SWA_BLOCK, SWA_WIDTH), lambda b, i: (b, i, qb)),
            pl.BlockSpec((1, seq_len, SWA_KV_WIDTH), lambda b, i: (b, 0, kb)),
            pl.BlockSpec((1, seq_len, SWA_KV_WIDTH), lambda b, i: (b, 0, kb + 1)),
            _resident((SWA_HEADS, SWA_BLOCK, span)),
        ],
        out_specs=pl.BlockSpec((1, SWA_BLOCK, SWA_WIDTH), lambda b, i: (b, i, 0)),
        out_shape=jax.ShapeDtypeStruct((b_sz, seq_len, SWA_WIDTH), BF16),
        compiler_params=_cparams("parallel", "arbitrary"),
        name="swa_attention",
    )(sink.astype(F32), h3d, h3d, h3d, table)


def _conv_kernel(x_ref, prev_ref, next_ref, w_ref, b_ref, v_ref, x1_ref, x2_ref, *, n_tiles):
    i = pl.program_id(1)
    tl = x_ref.shape[1]
    halo = prev_ref.shape[1]
    row = lax.broadcasted_iota(jnp.int32, (tl, 1), 0)
    outs = (v_ref, x1_ref, x2_ref)
    for c in range(3):
        sl = slice(c * HY_WIDTH, (c + 1) * HY_WIDTH)
        x = x_ref[0, :, sl].astype(F32)
        prev_row = jnp.where(i > 0, prev_ref[0, halo - 1:halo, sl].astype(F32), 0.0)
        next_row = jnp.where(i < n_tiles - 1, next_ref[0, 0:1, sl].astype(F32), 0.0)
        xm = jnp.where(row == 0, prev_row, pltpu.roll(x, 1, axis=0))
        xp = jnp.where(row == tl - 1, next_row, pltpu.roll(x, tl - 1, axis=0))
        u = xm * w_ref[0:1, sl] + x * w_ref[1:2, sl] + xp * w_ref[2:3, sl] + b_ref[:, sl]
        outs[c][0] = u.astype(BF16)


def _short_conv(h3d, conv_w, conv_b):
    b_sz, seq_len, _ = h3d.shape
    tl = CONV_TL
    halo = 16
    n_tiles = seq_len // tl
    width = 3 * HY_WIDTH
    cb = P_HY // width
    assert P_HY % width == 0 and seq_len % tl == 0
    per = tl // halo
    out = jax.ShapeDtypeStruct((b_sz, seq_len, HY_WIDTH), BF16)
    ospec = pl.BlockSpec((1, tl, HY_WIDTH), lambda b, i: (b, i, 0))
    return pl.pallas_call(
        functools.partial(_conv_kernel, n_tiles=n_tiles),
        grid=(b_sz, n_tiles),
        in_specs=[
            pl.BlockSpec((1, tl, width), lambda b, i: (b, i, cb)),
            pl.BlockSpec((1, halo, width), lambda b, i: (b, jnp.maximum(i * per - 1, 0), cb)),
            pl.BlockSpec((1, halo, width), lambda b, i: (b, jnp.minimum((i + 1) * per, seq_len // halo - 1), cb)),
            _resident((HY_SHORT_CONV, width)),
            _resident((1, width)),
        ],
        out_specs=[ospec, ospec, ospec],
        out_shape=[out, out, out],
        compiler_params=_cparams("parallel", "arbitrary"),
        name="hyena_short_conv",
    )(h3d, h3d, h3d, conv_w.astype(F32), conv_b.astype(F32).reshape(1, width))


def _filter_embedding(seq_len):
    t = np.linspace(0.0, 1.0, seq_len, dtype=np.float32).astype(np.float64)[:, None]
    w = (2.0 * math.pi * np.arange(seq_len, dtype=np.float32) / seq_len).astype(np.float32)
    bands = np.linspace(1e-4, HY_POS_BANDS - 1, HY_POS_BANDS, dtype=np.float32)
    ang = (w[:, None] * bands[None, :]).astype(np.float32).astype(np.float64)
    z = np.concatenate([t, np.cos(ang), -np.sin(ang)], axis=-1)
    zp = np.zeros((seq_len, HY_EMB_PAD), np.float32)
    zp[:, :HY_EMB_DIM] = z
    return zp


def _filter_deltas():
    min_decay = math.log(HY_DECAY_TARGET) / HY_SLOW_DECAY_PCT
    max_decay = math.log(HY_DECAY_TARGET) / HY_FAST_DECAY_PCT
    return np.abs(np.linspace(min_decay, max_decay, HY_WIDTH, dtype=np.float32))[None, :]


def _filter_kernel(z_ref, w1_ref, b1_ref, w2_ref, b2_ref, fr_ref, w3_ref, dl_ref, o_ref):
    hi = lax.Precision.HIGHEST
    tl = z_ref.shape[0]
    z = z_ref[...]
    fr = fr_ref[...]
    h = jnp.sin(fr * (jnp.dot(z, w1_ref[...], precision=hi, preferred_element_type=F32) + b1_ref[...]))
    h = jnp.sin(fr * (jnp.dot(h, w2_ref[...], precision=hi, preferred_element_type=F32) + b2_ref[...]))
    t = z[:, 0:1]
    window = jnp.exp(-t * dl_ref[...])
    row = pl.program_id(0) * tl + lax.broadcasted_iota(jnp.int32, (tl, 1), 0)
    for s in range(2 * HY_ORDER):
        sl = slice(s * HY_WIDTH, (s + 1) * HY_WIDTH)
        f = jnp.dot(h, w3_ref[:, sl], precision=hi, preferred_element_type=F32) * window
        if s % 2 == 1:
            f = jnp.where(row > 0, f, 0.0)
        o_ref[s] = f.astype(o_ref.dtype)


def _hyena_filters(seq_len, w1, b1, w2, b2, freq, w3):
    tl = min(FILT_TL, seq_len)
    z = jnp.asarray(_filter_embedding(seq_len))
    w1p = jnp.zeros((HY_EMB_PAD, HY_FILTER_HIDDEN), F32).at[:HY_EMB_DIM].set(w1.astype(F32))
    hid = HY_FILTER_HIDDEN
    n_f = 2 * HY_ORDER
    return pl.pallas_call(
        _filter_kernel,
        grid=(seq_len // tl,),
        in_specs=[
            pl.BlockSpec((tl, HY_EMB_PAD), lambda i: (i, 0)),
            _resident((HY_EMB_PAD, hid)), _resident((1, hid)),
            _resident((hid, hid)), _resident((1, hid)), _resident((1, hid)),
            _resident((hid, n_f * HY_WIDTH)), _resident((1, HY_WIDTH)),
        ],
        out_specs=pl.BlockSpec((n_f, tl, HY_WIDTH), lambda i: (0, i, 0)),
        out_shape=jax.ShapeDtypeStruct((n_f, seq_len, HY_WIDTH), BF16),
        compiler_params=_cparams("parallel"),
        name="hyena_filters",
    )(z, w1p, b1.astype(F32).reshape(1, hid), w2.astype(F32), b2.astype(F32).reshape(1, hid),
      freq.astype(F32).reshape(1, hid), w3.astype(F32), jnp.asarray(_filter_deltas()))


def _fft_dims(seq_len):
    n = 2 * seq_len
    n1 = 1 << ((n.bit_length() - 1) // 2)
    n2 = n // n1
    assert n1 * n2 == n and n1 == n2, "sequence length must give a square transform"
    return n1, n2


def _stack_complex(fr, fi):
    return np.block([[fr, -fi], [fi, fr]])


@functools.lru_cache(maxsize=None)
def _dft_constants(seq_len):
    n1, n2 = _fft_dims(seq_len)
    n = n1 * n2
    k1 = np.arange(n1)[:, None].astype(np.float64)
    t1 = np.arange(n1 // 2)[None, :].astype(np.float64)
    a1 = -2.0 * math.pi * k1 * t1 / n1
    f1r, f1i = np.cos(a1), np.sin(a1)
    w1_complex = _stack_complex(f1r, f1i)
    w1_real = np.concatenate([f1r, f1i], axis=0)
    k2 = np.arange(n2)[:, None].astype(np.float64)
    t2 = np.arange(n2)[None, :].astype(np.float64)
    a2 = -2.0 * math.pi * k2 * t2 / n2
    f2r, f2i = np.cos(a2), np.sin(a2)
    w2 = _stack_complex(f2r, f2i)
    w2_inv = _stack_complex(f2r, -f2i)
    at = -2.0 * math.pi * k1 * t2 / n
    tw = np.stack([np.cos(at), np.sin(at)], axis=0)[..., None]
    tw = np.broadcast_to(tw, (2, n1, n2, LANE)).astype(np.float32)
    g1r, g1i = f1r.T / n, -f1i.T / n
    w3 = _stack_complex(g1r, g1i)
    as_bf = lambda a: jnp.asarray(a, F32).astype(BF16)
    return dict(w1_complex=as_bf(w1_complex), w1_real=as_bf(w1_real), w2=as_bf(w2), w2_inv=as_bf(w2_inv),
                tw=jnp.asarray(tw), w3=as_bf(w3))


def _s1_kernel(w_ref, x_ref, o_ref):
    tl = x_ref.shape[-1]
    x = x_ref[...].reshape(-1, tl).astype(BF16)
    o_ref[0] = jnp.dot(w_ref[...], x, preferred_element_type=F32)


def _fft_stage1(u, w1, *, pair):
    s, seq_len, c = u.shape
    n1, n2 = _fft_dims(seq_len)
    per = 2 if pair else 1
    lanes = n2 * c
    tl = min(FFT_LANE_TILE, lanes)
    uv = u.reshape(s, n1 // 2, lanes)
    out = pl.pallas_call(
        _s1_kernel,
        grid=(s // per, lanes // tl),
        in_specs=[
            _resident(w1.shape),
            pl.BlockSpec((per, n1 // 2, tl), lambda p, j: (p, 0, j)),
        ],
        out_specs=pl.BlockSpec((1, 2 * n1, tl), lambda p, j: (p, 0, j)),
        out_shape=jax.ShapeDtypeStruct((s // per, 2 * n1, lanes), F32),
        compiler_params=_cparams("parallel", "arbitrary"),
        name="hyena_dft_stage1",
    )(w1, uv)
    return out.reshape(s // per, 2, n1, n2, c)


def _lane_tile(x, c):
    return jnp.concatenate([x] * (c // LANE), axis=-1)


def _twiddled_stack(a_ref, twr, twi):
    ar = a_ref[0, 0, 0]
    ai = a_ref[0, 1, 0]
    return jnp.concatenate([ar * twr - ai * twi, ar * twi + ai * twr], axis=0).astype(BF16)


def _s2_kernel(a_ref, k_ref, tw_ref, w2_ref, w2i_ref, o_ref):
    n2, c = a_ref.shape[-2:]
    twr = _lane_tile(tw_ref[0, 0], c)
    twi = _lane_tile(tw_ref[1, 0], c)
    spec = jnp.dot(w2_ref[...], _twiddled_stack(a_ref, twr, twi), preferred_element_type=F32)
    sr, si = spec[:n2], spec[n2:]
    kr = k_ref[0, 0]
    ki = k_ref[1, 0]
    prod = jnp.concatenate([sr * kr - si * ki, sr * ki + si * kr], axis=0).astype(BF16)
    back = jnp.dot(w2i_ref[...], prod, preferred_element_type=F32)
    br, bi = back[:n2], back[n2:]
    o_ref[0, 0, 0] = br * twr + bi * twi
    o_ref[0, 1, 0] = bi * twr - br * twi


def _fft_stage2(a, kf, consts):
    p, _, n1, n2, c = a.shape
    blk = pl.BlockSpec((1, 2, 1, n2, c), lambda k, q: (q, 0, k, 0, 0))
    return pl.pallas_call(
        _s2_kernel,
        grid=(n1, p),
        in_specs=[
            blk,
            pl.BlockSpec((2, 1, n2, c), lambda k, q: (0, k, 0, 0)),
            pl.BlockSpec((2, 1, n2, LANE), lambda k, q: (0, k, 0, 0)),
            _resident((2 * n2, 2 * n2)),
            _resident((2 * n2, 2 * n2)),
        ],
        out_specs=blk,
        out_shape=jax.ShapeDtypeStruct(a.shape, F32),
        compiler_params=_cparams("parallel", "arbitrary"),
        name="hyena_dft_stage2",
    )(a, kf, consts["tw"], consts["w2"], consts["w2_inv"])


def _s2f_kernel(af_ref, ab_ref, tw_ref, w2_ref, o_ref):
    n2, c = af_ref.shape[-2:]
    twr = _lane_tile(tw_ref[0, 0], c)
    twi = _lane_tile(tw_ref[1, 0], c)
    hf = jnp.dot(w2_ref[...], _twiddled_stack(af_ref, twr, twi), preferred_element_type=F32)
    hb = jnp.dot(w2_ref[...], _twiddled_stack(ab_ref, twr, twi), preferred_element_type=F32)
    o_ref[0, 0, 0] = hf[:n2] + hb[:n2]
    o_ref[0, 1, 0] = hf[n2:] - hb[n2:]


def _filter_spectrum(a_filt, consts):
    _, _, n1, n2, c = a_filt.shape
    return pl.pallas_call(
        _s2f_kernel,
        grid=(HY_ORDER, n1),
        in_specs=[
            pl.BlockSpec((1, 2, 1, n2, c), lambda o, k: (2 * o, 0, k, 0, 0)),
            pl.BlockSpec((1, 2, 1, n2, c), lambda o, k: (2 * o + 1, 0, k, 0, 0)),
            pl.BlockSpec((2, 1, n2, LANE), lambda o, k: (0, k, 0, 0)),
            _resident((2 * n2, 2 * n2)),
        ],
        out_specs=pl.BlockSpec((1, 2, 1, n2, c), lambda o, k: (o, 0, k, 0, 0)),
        out_shape=jax.ShapeDtypeStruct((HY_ORDER, 2, n1, n2, c), F32),
        compiler_params=_cparams("parallel", "arbitrary"),
        name="hyena_filter_spectrum",
    )(a_filt, a_filt, consts["tw"], consts["w2"])


def _s3_kernel(w_ref, b_ref, z_ref, g_ref, bias_ref, o_ref):
    tl = b_ref.shape[-1]
    y = jnp.dot(w_ref[...], b_ref[0].astype(BF16), preferred_element_type=F32)
    z = z_ref[...].reshape(-1, tl).astype(F32)
    g = g_ref[...].reshape(-1, tl).astype(F32)
    out = g * (y + bias_ref[...] * z)
    o_ref[...] = out.reshape(o_ref.shape).astype(o_ref.dtype)


def _fft_stage3(bv, z, gate, bias, w3):
    s, seq_len, c = z.shape
    n1, n2 = _fft_dims(seq_len)
    lanes = n2 * c
    tl = min(FFT_LANE_TILE, lanes)
    seq_blk = pl.BlockSpec((2, n1 // 2, tl), lambda p, j: (p, 0, j))
    out = pl.pallas_call(
        _s3_kernel,
        grid=(s // 2, lanes // tl),
        in_specs=[
            _resident(w3.shape),
            pl.BlockSpec((1, 2 * n1, tl), lambda p, j: (p, 0, j)),
            seq_blk, seq_blk,
            _resident((1, tl)),
        ],
        out_specs=seq_blk,
        out_shape=jax.ShapeDtypeStruct((s, n1 // 2, lanes), BF16),
        compiler_params=_cparams("parallel", "arbitrary"),
        name="hyena_dft_stage3",
    )(w3, bv.reshape(s // 2, 2 * n1, lanes), z.reshape(s, n1 // 2, lanes), gate.reshape(s, n1 // 2, lanes),
      jnp.tile(bias.astype(F32), tl // c).reshape(1, tl))
    return out.reshape(s, seq_len, c)


def _hyena_spectra(seq_len, w1, b1, w2, b2, freq, w3):
    consts = _dft_constants(seq_len)
    filt = _hyena_filters(seq_len, w1, b1, w2, b2, freq, w3)
    a_filt = _fft_stage1(filt, consts["w1_real"], pair=False)
    return _filter_spectrum(a_filt, consts)


def _hyena(h3d, conv_w, conv_b, kf, hy_bias):
    seq_len = h3d.shape[1]
    consts = _dft_constants(seq_len)
    v, x1, x2 = _short_conv(h3d, conv_w, conv_b)
    z = v
    for n, gate in enumerate((x1, x2)):
        a = _fft_stage1(z, consts["w1_complex"], pair=True)
        bv = _fft_stage2(a, kf[n], consts)
        z = _fft_stage3(bv, z, gate, hy_bias[n], consts["w3"])
    return z


def _layernorm(y, g, b):
    mu = jnp.mean(y, axis=-1, keepdims=True)
    d = y - mu
    var = jnp.mean(d * d, axis=-1, keepdims=True)
    return d * lax.rsqrt(var + LN_EPS) * g + b


def _merge_kernel(g_ref, a_ref, hb_ref, c_ref, x_ref, wa_ref, wb_ref, wc_ref, wo_ref, lg_ref, lb_ref, o_ref):
    d = D_MODEL
    merged = g_ref[:, 0:d].astype(F32) * jnp.dot(a_ref[...], wa_ref[...], preferred_element_type=F32)
    merged += g_ref[:, d:2 * d].astype(F32) * jnp.dot(hb_ref[...], wb_ref[...], preferred_element_type=F32)
    merged += g_ref[:, 2 * d:3 * d].astype(F32) * jnp.dot(c_ref[...], wc_ref[...], preferred_element_type=F32)
    mix = jnp.dot(merged.astype(BF16), wo_ref[...], preferred_element_type=F32)
    o_ref[...] = _layernorm(DEEPNORM_ALPHA * x_ref[...] + mix, lg_ref[...], lb_ref[...])


def _merge(h2d, a, hb, c, x2d, wa, wb, wc, wo, ln_g, ln_b):
    m = x2d.shape[0]
    tm = MERGE_TM
    row = lambda width: pl.BlockSpec((tm, width), lambda i: (i, 0))
    return pl.pallas_call(
        _merge_kernel,
        grid=(m // tm,),
        in_specs=[
            row(N_BRANCH * D_MODEL), row(NA_WIDTH), row(HY_WIDTH), row(SWA_WIDTH), row(D_MODEL),
            _resident((NA_WIDTH, D_MODEL)), _resident((HY_WIDTH, D_MODEL)), _resident((SWA_WIDTH, D_MODEL)),
            _resident((D_MODEL, D_MODEL)), _resident((1, D_MODEL)), _resident((1, D_MODEL)),
        ],
        out_specs=row(D_MODEL),
        out_shape=jax.ShapeDtypeStruct((m, D_MODEL), F32),
        compiler_params=_cparams("parallel"),
        name="merge_ln",
    )(h2d, a, hb, c, x2d, wa, wb, wc, wo, ln_g, ln_b)


def _mlp_kernel(x_ref, wu_ref, bu_ref, wd_ref, bd_ref, lg_ref, lb_ref, o_ref):
    x = x_ref[...]
    xb = x.astype(BF16)
    acc = DEEPNORM_ALPHA * x + bd_ref[...]
    for c in range(D_FF // MLP_FF_CHUNK):
        sl = slice(c * MLP_FF_CHUNK, (c + 1) * MLP_FF_CHUNK)
        up = jnp.dot(xb, wu_ref[:, sl], preferred_element_type=F32) + bu_ref[:, sl]
        up = jnp.square(jnp.maximum(up, 0.0))
        acc += jnp.dot(up.astype(BF16), wd_ref[sl, :], preferred_element_type=F32)
    o_ref[...] = _layernorm(acc, lg_ref[...], lb_ref[...])


def _mlp(x2d, wu, bu, wd, bd, ln_g, ln_b):
    m = x2d.shape[0]
    tm = MLP_TM
    row = pl.BlockSpec((tm, D_MODEL), lambda i: (i, 0))
    return pl.pallas_call(
        _mlp_kernel,
        grid=(m // tm,),
        in_specs=[
            row, _resident((D_MODEL, D_FF)), _resident((1, D_FF)), _resident((D_FF, D_MODEL)),
            _resident((1, D_MODEL)), _resident((1, D_MODEL)), _resident((1, D_MODEL)),
        ],
        out_specs=row,
        out_shape=jax.ShapeDtypeStruct((m, D_MODEL), F32),
        compiler_params=_cparams("parallel"),
        name="mlp_ln",
    )(x2d, wu, bu, wd, bd, ln_g, ln_b)


def _permute_in_columns(a):
    return jnp.concatenate([a[..., OFF_GATE:], a[..., :OFF_GATE]], axis=-1)


def _prepare_layer(l, p):
    row = lambda a: a.astype(F32).reshape(1, -1)
    return dict(
        w_in=_permute_in_columns(p["w_in"][l]).astype(BF16),
        b_in=row(_permute_in_columns(p["b_in"][l])),
        conv_w=p["hy_conv_w"][l], conv_b=p["hy_conv_b"][l], hy_bias=p["hy_bias"][l],
        filt=(p["hy_filt_w1"][l], p["hy_filt_b1"][l], p["hy_filt_w2"][l], p["hy_filt_b2"][l],
              p["hy_filt_freq"][l], p["hy_filt_w3"][l]),
        na_bias=_na_bias_table(p["na_rpb"][l]),
        sink=p["swa_sink"][l],
        wa=p["w_branch_a"][l].astype(BF16), wb=p["w_branch_b"][l].astype(BF16),
        wc=p["w_branch_c"][l].astype(BF16), wo=p["w_out"][l].astype(BF16),
        ln1_g=row(p["ln1_g"][l]), ln1_b=row(p["ln1_b"][l]),
        wu=p["w_up"][l].astype(BF16), bu=row(p["b_up"][l]),
        wd=p["w_down"][l].astype(BF16), bd=row(p["b_down"][l]),
        ln2_g=row(p["ln2_g"][l]), ln2_b=row(p["ln2_b"][l]),
    )


def _encoder_block(x, lp, swa_table):
    b_sz, seq_len, _ = x.shape
    m = b_sz * seq_len
    x2d = x.reshape(m, D_MODEL)
    h2d = _inproj(x2d, lp["w_in"], lp["b_in"])
    h3d = h2d.reshape(b_sz, seq_len, D_IN)
    a = _na_attention(h3d, lp["na_bias"])
    kf = _hyena_spectra(seq_len, *lp["filt"])
    hb = _hyena(h3d, lp["conv_w"], lp["conv_b"], kf, lp["hy_bias"])
    c = _swa_attention(h3d, lp["sink"], swa_table)
    x1 = _merge(h2d, a.reshape(m, NA_WIDTH), hb.reshape(m, HY_WIDTH), c.reshape(m, SWA_WIDTH), x2d,
                lp["wa"], lp["wb"], lp["wc"], lp["wo"], lp["ln1_g"], lp["ln1_b"])
    x2 = _mlp(x1, lp["wu"], lp["bu"], lp["wd"], lp["bd"], lp["ln2_g"], lp["ln2_b"])
    return x2.reshape(b_sz, seq_len, D_MODEL)


def kernel(x_prompt, x_sample, w_in, b_in, hy_conv_w, hy_conv_b, hy_filt_w1, hy_filt_b1, hy_filt_w2,
           hy_filt_b2, hy_filt_freq, hy_filt_w3, hy_bias, na_rpb, swa_sink, w_branch_a, w_branch_b,
           w_branch_c, w_out, ln1_g, ln1_b, w_up, b_up, w_down, b_down, ln2_g, ln2_b):
    params = dict(w_in=w_in, b_in=b_in, hy_conv_w=hy_conv_w, hy_conv_b=hy_conv_b, hy_filt_w1=hy_filt_w1,
                  hy_filt_b1=hy_filt_b1, hy_filt_w2=hy_filt_w2, hy_filt_b2=hy_filt_b2,
                  hy_filt_freq=hy_filt_freq, hy_filt_w3=hy_filt_w3, hy_bias=hy_bias, na_rpb=na_rpb,
                  swa_sink=swa_sink, w_branch_a=w_branch_a, w_branch_b=w_branch_b, w_branch_c=w_branch_c,
                  w_out=w_out, ln1_g=ln1_g, ln1_b=ln1_b, w_up=w_up, b_up=b_up, w_down=w_down,
                  b_down=b_down, ln2_g=ln2_g, ln2_b=ln2_b)
    swa_table = _swa_table()
    y_prompt = x_prompt
    y_sample = x_sample
    for l in range(DEPTH):
        lp = _prepare_layer(l, params)
        y_prompt = _encoder_block(y_prompt, lp, swa_table)
        y_sample = _encoder_block(y_sample, lp, swa_table)
    return (y_prompt, y_sample)
```

```python
import functools
import math

import numpy as np
import jax
import jax.numpy as jnp
from jax import lax
from jax.experimental import pallas as pl
from jax.experimental.pallas import tpu as pltpu

F32 = jnp.float32
BF16 = jnp.bfloat16

D_MODEL = 1024
DEPTH = 2
HEAD_DIM = 64
GRID_W = 64
NA_HEADS = 8
NA_WIN_ROWS = 8
NA_WIN_COLS = 16
NA_WIDTH = NA_HEADS * HEAD_DIM
HY_WIDTH = D_MODEL // 2
HY_ORDER = 2
HY_SHORT_CONV = 3
HY_POS_BANDS = 16
HY_EMB_DIM = 1 + 2 * HY_POS_BANDS
HY_EMB_PAD = 128
HY_FILTER_HIDDEN = 64
HY_FAST_DECAY_PCT = 0.3
HY_SLOW_DECAY_PCT = 1.5
HY_DECAY_TARGET = 1e-2
SWA_HEADS = 8
SWA_KV_HEADS = 2
SWA_GROUP = SWA_HEADS // SWA_KV_HEADS
SWA_WIDTH = SWA_HEADS * HEAD_DIM
SWA_KV_WIDTH = SWA_KV_HEADS * HEAD_DIM
SWA_WINDOW = 128
SWA_BLOCK = 128
N_BRANCH = 3
D_FF = 4 * D_MODEL
OFF_HY = 3 * NA_WIDTH
OFF_SWA = OFF_HY + 3 * HY_WIDTH
OFF_GATE = OFF_SWA + SWA_WIDTH + 2 * SWA_KV_WIDTH
D_IN = OFF_GATE + N_BRANCH * D_MODEL
DEEPNORM_ALPHA = (2 * DEPTH) ** 0.25
LN_EPS = 1e-5
NEG_BIG = -1e30
LOG2E = 1.4426950408889634

P_GATE = 0
P_NA = N_BRANCH * D_MODEL
P_HY = P_NA + 3 * NA_WIDTH
P_SWA = P_HY + 3 * HY_WIDTH

LANE = 128
VMEM_LIMIT_BYTES = 56 * 1024 * 1024

NA_ROWS_PER_STEP = 1
INPROJ_TM = 2048
INPROJ_TN = 768
MERGE_TM = 512
MLP_TM = 512
MLP_FF_CHUNK = 1024
CONV_TL = 256
FILT_TL = 512
FFT_LANE_TILE = 2048
HY_TW_SPLIT = 8


def _cparams(*sem):
    return pltpu.CompilerParams(dimension_semantics=sem, vmem_limit_bytes=VMEM_LIMIT_BYTES)


def _resident(shape):
    nd = len(shape)
    return pl.BlockSpec(shape, lambda *_: (0,) * nd, pipeline_mode=pl.Buffered(1))


def _inproj_kernel(x_ref, w_ref, b_ref, o_ref, xb_ref, *, n_gate_tiles):
    j = pl.program_id(1)

    @pl.when(j == 0)
    def _():
        xb_ref[...] = x_ref[...].astype(BF16)

    def tile(apply_sigmoid):
        acc = jnp.dot(xb_ref[...], w_ref[...], preferred_element_type=F32) + b_ref[...]
        if apply_sigmoid:
            acc = 1.0 / (1.0 + jnp.exp(-acc))
        o_ref[...] = acc.astype(o_ref.dtype)

    pl.when(j < n_gate_tiles)(lambda: tile(True))
    pl.when(j >= n_gate_tiles)(lambda: tile(False))


def _inproj(x2d, w_bf, b_row):
    m = x2d.shape[0]
    tm = min(INPROJ_TM, m)
    tn = INPROJ_TN
    assert m % tm == 0 and D_IN % tn == 0 and P_NA % tn == 0
    return pl.pallas_call(
        functools.partial(_inproj_kernel, n_gate_tiles=P_NA // tn),
        grid=(m // tm, D_IN // tn),
        in_specs=[
            pl.BlockSpec((tm, D_MODEL), lambda i, j: (i, 0)),
            pl.BlockSpec((D_MODEL, tn), lambda i, j: (0, j)),
            pl.BlockSpec((1, tn), lambda i, j: (0, j)),
        ],
        out_specs=pl.BlockSpec((tm, tn), lambda i, j: (i, j)),
        out_shape=jax.ShapeDtypeStruct((m, D_IN), BF16),
        scratch_shapes=[pltpu.VMEM((tm, D_MODEL), BF16)],
        compiler_params=_cparams("parallel", "arbitrary"),
        name="inproj",
    )(x2d, w_bf, b_row)


def _na_bias_table(rpb):
    kr = NA_WIN_ROWS
    pad = GRID_W - NA_WIN_COLS
    p = jnp.pad(rpb.astype(F32) * LOG2E, ((0, 0), (0, 0), (pad, pad)))
    cols = jnp.stack([p[:, :, GRID_W - 1 - w:2 * GRID_W - 1 - w] for w in range(GRID_W)], axis=2)
    t = jnp.stack([cols[:, kr - 1 - d:2 * kr - 1 - d] for d in range(kr)], axis=0)
    w = np.arange(GRID_W)[:, None]
    kc = np.arange(GRID_W)[None, :]
    col_start = np.clip(w - NA_WIN_COLS // 2, 0, GRID_W - NA_WIN_COLS)
    valid = (kc >= col_start) & (kc < col_start + NA_WIN_COLS)
    t = jnp.where(valid[None, None, None], t, NEG_BIG)
    t = jnp.transpose(t, (0, 1, 3, 2, 4))
    return t.reshape(kr, NA_HEADS, GRID_W, kr * GRID_W)


def _softmax_pv(scores, values, extra=None):
    ps, ls = [], []
    for h, s in enumerate(scores):
        m = jnp.max(s, axis=-1, keepdims=True)
        if extra is not None:
            m = jnp.maximum(m, extra[h])
        p = jnp.exp2(s - m)
        l = jnp.sum(p, axis=-1, keepdims=True)
        if extra is not None:
            l = l + jnp.exp2(extra[h] - m)
        ps.append(p.astype(BF16))
        ls.append(l)
    return [jnp.dot(p, v, preferred_element_type=F32) / l for p, v, l in zip(ps, values, ls)]


def _na_kernel(q_ref, k_ref, v_ref, *rest, rows, rows_per_step):
    bias_refs, o_ref = rest[:-1], rest[-1]
    span = NA_WIN_ROWS * GRID_W
    scale = HEAD_DIM ** -0.5 * LOG2E
    heads = [slice(h * HEAD_DIM, (h + 1) * HEAD_DIM) for h in range(NA_HEADS)]
    scores, values = [], []
    for j in range(rows_per_step):
        r = pl.program_id(1) * rows_per_step + j
        rs = jnp.clip(r - NA_WIN_ROWS // 2, 0, rows - NA_WIN_ROWS)
        start = pl.multiple_of(rs * GRID_W, GRID_W)
        q = q_ref[0, j * GRID_W:(j + 1) * GRID_W, :]
        k = k_ref[0, pl.ds(start, span), :]
        v = v_ref[0, pl.ds(start, span), :]
        for h, sl in enumerate(heads):
            s = lax.dot_general(q[:, sl], k[:, sl], (((1,), (1,)), ((), ())), preferred_element_type=F32)
            scores.append(s * scale + bias_refs[j][0, h])
            values.append(v[:, sl])
    outs = _softmax_pv(scores, values)
    for j in range(rows_per_step):
        for h, sl in enumerate(heads):
            o_ref[0, j * GRID_W:(j + 1) * GRID_W, sl] = outs[j * NA_HEADS + h].astype(o_ref.dtype)


def _na_attention(h3d, bias_table):
    b_sz, seq_len, _ = h3d.shape
    rows = seq_len // GRID_W
    assert rows >= NA_WIN_ROWS
    cb = P_NA // NA_WIDTH
    span = NA_WIN_ROWS * GRID_W

    def delta_of(r):
        return r - jnp.clip(r - NA_WIN_ROWS // 2, 0, rows - NA_WIN_ROWS)

    rps = NA_ROWS_PER_STEP
    assert rows % rps == 0
    bias_specs = [pl.BlockSpec((1, NA_HEADS, GRID_W, span), lambda b, r, j=j: (delta_of(r * rps + j), 0, 0, 0))
                  for j in range(rps)]
    return pl.pallas_call(
        functools.partial(_na_kernel, rows=rows, rows_per_step=rps),
        grid=(b_sz, rows // rps),
        in_specs=[
            pl.BlockSpec((1, rps * GRID_W, NA_WIDTH), lambda b, r: (b, r, cb)),
            pl.BlockSpec((1, seq_len, NA_WIDTH), lambda b, r: (b, 0, cb + 1), pipeline_mode=pl.Buffered(1)),
            pl.BlockSpec((1, seq_len, NA_WIDTH), lambda b, r: (b, 0, cb + 2), pipeline_mode=pl.Buffered(1)),
        ] + bias_specs,
        out_specs=pl.BlockSpec((1, rps * GRID_W, NA_WIDTH), lambda b, r: (b, r, 0)),
        out_shape=jax.ShapeDtypeStruct((b_sz, seq_len, NA_WIDTH), BF16),
        compiler_params=_cparams("parallel", "arbitrary"),
        name="na_attention",
    )(h3d, h3d, h3d, *([bias_table] * rps))


def _swa_table():
    span = SWA_BLOCK + 2 * SWA_WINDOW
    slopes = 2.0 ** (-8.0 * (np.arange(SWA_HEADS, dtype=np.float64) + 1.0) / SWA_HEADS)
    q_off = np.arange(SWA_BLOCK)[:, None]
    k_off = np.arange(span)[None, :] - SWA_WINDOW
    rel = np.abs(k_off - q_off)
    t = -slopes[:, None, None] * rel[None].astype(np.float64) * LOG2E
    t = np.where((rel <= SWA_WINDOW)[None], t, NEG_BIG)
    return jnp.asarray(t, F32)


def _swa_kernel(sink_ref, q_ref, k_ref, v_ref, tab_ref, o_ref, *, nb):
    i = pl.program_id(1)
    blk = SWA_BLOCK
    span = blk + 2 * SWA_WINDOW

    def rows(ref, j):
        return ref[0, pl.ds(pl.multiple_of(j * blk, blk), blk), :]

    ip = jnp.maximum(i - 1, 0)
    inx = jnp.minimum(i + 1, nb - 1)
    k = jnp.concatenate([rows(k_ref, ip), rows(k_ref, i), rows(k_ref, inx)], axis=0)
    v = jnp.concatenate([rows(v_ref, ip), rows(v_ref, i), rows(v_ref, inx)], axis=0)
    col = lax.broadcasted_iota(jnp.int32, (1, span), 1)
    lo = jnp.where(i == 0, SWA_WINDOW, 0)
    hi = jnp.where(i == nb - 1, blk + SWA_WINDOW, span)
    edge = jnp.where((col < lo) | (col >= hi), NEG_BIG, 0.0).astype(F32)
    q = q_ref[0]
    scale = HEAD_DIM ** -0.5 * LOG2E
    heads = [slice(h * HEAD_DIM, (h + 1) * HEAD_DIM) for h in range(SWA_HEADS)]
    kv_heads = [slice((h // SWA_GROUP) * HEAD_DIM, (h // SWA_GROUP + 1) * HEAD_DIM) for h in range(SWA_HEADS)]
    scores = [lax.dot_general(q[:, sl], k[:, gl], (((1,), (1,)), ((), ())), preferred_element_type=F32)
              * scale + tab_ref[h] + edge for h, (sl, gl) in enumerate(zip(heads, kv_heads))]
    sinks = [sink_ref[h] * LOG2E for h in range(SWA_HEADS)]
    outs = _softmax_pv(scores, [v[:, gl] for gl in kv_heads], extra=sinks)
    for sl, o in zip(heads, outs):
        o_ref[0, :, sl] = o.astype(o_ref.dtype)


def _swa_attention(h3d, sink, table):
    b_sz, seq_len, _ = h3d.shape
    nb = seq_len // SWA_BLOCK
    span = SWA_BLOCK + 2 * SWA_WINDOW
    qb = P_SWA // SWA_WIDTH
    kb = (P_SWA + SWA_WIDTH) // SWA_KV_WIDTH
    return pl.pallas_call(
        functools.partial(_swa_kernel, nb=nb),
        grid=(b_sz, nb),
        in_specs=[
            pl.BlockSpec(memory_space=pltpu.SMEM),
            pl.BlockSpec((1, SWA_BLOCK, SWA_WIDTH), lambda b, i: (b, i, qb)),
            pl.BlockSpec((1, seq_len, SWA_KV_WIDTH), lambda b, i: (b, 0, kb)),
            pl.BlockSpec((1, seq_len, SWA_KV_WIDTH), lambda b, i: (b, 0, kb + 1)),
            _resident((SWA_HEADS, SWA_BLOCK, span)),
        ],
        out_specs=pl.BlockSpec((1, SWA_BLOCK, SWA_WIDTH), lambda b, i: (b, i, 0)),
        out_shape=jax.ShapeDtypeStruct((b_sz, seq_len, SWA_WIDTH), BF16),
        compiler_params=_cparams("parallel", "arbitrary"),
        name="swa_attention",
    )(sink.astype(F32), h3d, h3d, h3d, table)


def _conv_kernel(x_ref, prev_ref, next_ref, w_ref, b_ref, v_ref, x1_ref, x2_ref, *, n_tiles):
    i = pl.program_id(1)
    tl = x_ref.shape[1]
    halo = prev_ref.shape[1]
    row = lax.broadcasted_iota(jnp.int32, (tl, 1), 0)
    outs = (v_ref, x1_ref, x2_ref)
    for c in range(3):
        sl = slice(c * HY_WIDTH, (c + 1) * HY_WIDTH)
        x = x_ref[0, :, sl].astype(F32)
        prev_row = jnp.where(i > 0, prev_ref[0, halo - 1:halo, sl].astype(F32), 0.0)
        next_row = jnp.where(i < n_tiles - 1, next_ref[0, 0:1, sl].astype(F32), 0.0)
        xm = jnp.where(row == 0, prev_row, pltpu.roll(x, 1, axis=0))
        xp = jnp.where(row == tl - 1, next_row, pltpu.roll(x, tl - 1, axis=0))
        u = xm * w_ref[0:1, sl] + x * w_ref[1:2, sl] + xp * w_ref[2:3, sl] + b_ref[:, sl]
        outs[c][0] = u.astype(BF16)


def _short_conv(h3d, conv_w, conv_b):
    b_sz, seq_len, _ = h3d.shape
    tl = CONV_TL
    halo = 16
    n_tiles = seq_len // tl
    width = 3 * HY_WIDTH
    cb = P_HY // width
    assert P_HY % width == 0 and seq_len % tl == 0
    per = tl // halo
    out = jax.ShapeDtypeStruct((b_sz, seq_len, HY_WIDTH), BF16)
    ospec = pl.BlockSpec((1, tl, HY_WIDTH), lambda b, i: (b, i, 0))
    return pl.pallas_call(
        functools.partial(_conv_kernel, n_tiles=n_tiles),
        grid=(b_sz, n_tiles),
        in_specs=[
            pl.BlockSpec((1, tl, width), lambda b, i: (b, i, cb)),
            pl.BlockSpec((1, halo, width), lambda b, i: (b, jnp.maximum(i * per - 1, 0), cb)),
            pl.BlockSpec((1, halo, width), lambda b, i: (b, jnp.minimum((i + 1) * per, seq_len // halo - 1), cb)),
            _resident((HY_SHORT_CONV, width)),
            _resident((1, width)),
        ],
        out_specs=[ospec, ospec, ospec],
        out_shape=[out, out, out],
        compiler_params=_cparams("parallel", "arbitrary"),
        name="hyena_short_conv",
    )(h3d, h3d, h3d, conv_w.astype(F32), conv_b.astype(F32).reshape(1, width))


def _filter_embedding(seq_len):
    t = np.linspace(0.0, 1.0, seq_len, dtype=np.float32).astype(np.float64)[:, None]
    w = (2.0 * math.pi * np.arange(seq_len, dtype=np.float32) / seq_len).astype(np.float32)
    bands = np.linspace(1e-4, HY_POS_BANDS - 1, HY_POS_BANDS, dtype=np.float32)
    ang = (w[:, None] * bands[None, :]).astype(np.float32).astype(np.float64)
    z = np.concatenate([t, np.cos(ang), -np.sin(ang)], axis=-1)
    zp = np.zeros((seq_len, HY_EMB_PAD), np.float32)
    zp[:, :HY_EMB_DIM] = z
    return zp


def _filter_deltas():
    min_decay = math.log(HY_DECAY_TARGET) / HY_SLOW_DECAY_PCT
    max_decay = math.log(HY_DECAY_TARGET) / HY_FAST_DECAY_PCT
    return np.abs(np.linspace(min_decay, max_decay, HY_WIDTH, dtype=np.float32))[None, :]


def _filter_kernel(z_ref, w1_ref, b1_ref, w2_ref, b2_ref, fr_ref, w3_ref, dl_ref, o_ref):
    hi = lax.Precision.HIGHEST
    tl = z_ref.shape[0]
    z = z_ref[...]
    fr = fr_ref[...]
    h = jnp.sin(fr * (jnp.dot(z, w1_ref[...], precision=hi, preferred_element_type=F32) + b1_ref[...]))
    h = jnp.sin(fr * (jnp.dot(h, w2_ref[...], precision=hi, preferred_element_type=F32) + b2_ref[...]))
    t = z[:, 0:1]
    window = jnp.exp(-t * dl_ref[...])
    row = pl.program_id(0) * tl + lax.broadcasted_iota(jnp.int32, (tl, 1), 0)
    for s in range(2 * HY_ORDER):
        sl = slice(s * HY_WIDTH, (s + 1) * HY_WIDTH)
        f = jnp.dot(h, w3_ref[:, sl], precision=hi, preferred_element_type=F32) * window
        if s % 2 == 1:
            f = jnp.where(row > 0, f, 0.0)
        o_ref[s] = f.astype(o_ref.dtype)


def _hyena_filters(seq_len, w1, b1, w2, b2, freq, w3):
    tl = min(FILT_TL, seq_len)
    z = jnp.asarray(_filter_embedding(seq_len))
    w1p = jnp.zeros((HY_EMB_PAD, HY_FILTER_HIDDEN), F32).at[:HY_EMB_DIM].set(w1.astype(F32))
    hid = HY_FILTER_HIDDEN
    n_f = 2 * HY_ORDER
    return pl.pallas_call(
        _filter_kernel,
        grid=(seq_len // tl,),
        in_specs=[
            pl.BlockSpec((tl, HY_EMB_PAD), lambda i: (i, 0)),
            _resident((HY_EMB_PAD, hid)), _resident((1, hid)),
            _resident((hid, hid)), _resident((1, hid)), _resident((1, hid)),
            _resident((hid, n_f * HY_WIDTH)), _resident((1, HY_WIDTH)),
        ],
        out_specs=pl.BlockSpec((n_f, tl, HY_WIDTH), lambda i: (0, i, 0)),
        out_shape=jax.ShapeDtypeStruct((n_f, seq_len, HY_WIDTH), BF16),
        compiler_params=_cparams("parallel"),
        name="hyena_filters",
    )(z, w1p, b1.astype(F32).reshape(1, hid), w2.astype(F32), b2.astype(F32).reshape(1, hid),
      freq.astype(F32).reshape(1, hid), w3.astype(F32), jnp.asarray(_filter_deltas()))


def _fft_dims(seq_len):
    n = 2 * seq_len
    n1 = 1 << ((n.bit_length() - 1) // 2)
    n2 = n // n1
    assert n1 * n2 == n and n1 == n2, "sequence length must give a square transform"
    return n1, n2


def _stack_complex(fr, fi):
    return np.block([[fr, -fi], [fi, fr]])


@functools.lru_cache(maxsize=None)
def _dft_constants(seq_len):
    n1, n2 = _fft_dims(seq_len)
    n = n1 * n2
    k1 = np.arange(n1)[:, None].astype(np.float64)
    t1 = np.arange(n1 // 2)[None, :].astype(np.float64)
    a1 = -2.0 * math.pi * k1 * t1 / n1
    f1r, f1i = np.cos(a1), np.sin(a1)
    w1_complex = _stack_complex(f1r, f1i)
    w1_real = np.concatenate([f1r, f1i], axis=0)
    k2 = np.arange(n2)[:, None].astype(np.float64)
    t2 = np.arange(n2)[None, :].astype(np.float64)
    a2 = -2.0 * math.pi * k2 * t2 / n2
    f2r, f2i = np.cos(a2), np.sin(a2)
    w2 = _stack_complex(f2r, f2i)
    w2_inv = _stack_complex(f2r, -f2i)
    at = -2.0 * math.pi * k1 * t2 / n
    tw = np.stack([np.cos(at), np.sin(at)], axis=0)[..., None]
    tw = np.broadcast_to(tw, (2, n1, n2, LANE)).astype(np.float32)
    g1r, g1i = f1r.T / n, -f1i.T / n
    w3 = _stack_complex(g1r, g1i)
    kb = np.arange(HY_TW_SPLIT)[:, None].astype(np.float64)
    ab = -2.0 * math.pi * kb * t2 / n
    tbr, tbi = np.cos(ab), np.sin(ab)
    w2f = np.stack([_stack_complex(f2r * tbr[b] - f2i * tbi[b], f2r * tbi[b] + f2i * tbr[b])
                    for b in range(HY_TW_SPLIT)])
    w2i = np.stack([_stack_complex(tbr[b][:, None] * f2r - tbi[b][:, None] * f2i,
                                   -(tbr[b][:, None] * f2i + tbi[b][:, None] * f2r))
                    for b in range(HY_TW_SPLIT)])
    ka = (np.arange(n1 // HY_TW_SPLIT) * HY_TW_SPLIT)[:, None].astype(np.float64)
    aa = -2.0 * math.pi * ka * t2 / n
    ta = np.stack([np.cos(aa), np.sin(aa)], axis=0)[..., None]
    ta = np.broadcast_to(ta, (2, n1 // HY_TW_SPLIT, n2, LANE)).astype(np.float32)
    as_bf = lambda a: jnp.asarray(a, F32).astype(BF16)
    return dict(w1_complex=as_bf(w1_complex), w1_real=as_bf(w1_real), w2=as_bf(w2), w2f=as_bf(w2f),
                w2i=as_bf(w2i), tw=jnp.asarray(tw), ta=jnp.asarray(ta), w3=as_bf(w3))


def _s1_kernel(w_ref, x_ref, o_ref):
    tl = x_ref.shape[-1]
    x = x_ref[...].reshape(-1, tl).astype(BF16)
    o_ref[0] = jnp.dot(w_ref[...], x, preferred_element_type=F32)


def _fft_stage1(u, w1, *, pair):
    s, seq_len, c = u.shape
    n1, n2 = _fft_dims(seq_len)
    per = 2 if pair else 1
    lanes = n2 * c
    tl = min(FFT_LANE_TILE, lanes)
    uv = u.reshape(s, n1 // 2, lanes)
    out = pl.pallas_call(
        _s1_kernel,
        grid=(s // per, lanes // tl),
        in_specs=[
            _resident(w1.shape),
            pl.BlockSpec((per, n1 // 2, tl), lambda p, j: (p, 0, j)),
        ],
        out_specs=pl.BlockSpec((1, 2 * n1, tl), lambda p, j: (p, 0, j)),
        out_shape=jax.ShapeDtypeStruct((s // per, 2 * n1, lanes), F32),
        compiler_params=_cparams("parallel", "arbitrary"),
        name="hyena_dft_stage1",
    )(w1, uv)
    return out.reshape(s // per, 2, n1, n2, c)


def _lane_tile(x, c):
    return jnp.concatenate([x] * (c // LANE), axis=-1)


def _twiddled_stack(a_ref, twr, twi):
    ar = a_ref[0, 0, 0]
    ai = a_ref[0, 1, 0]
    return jnp.concatenate([ar * twr - ai * twi, ar * twi + ai * twr], axis=0).astype(BF16)


def _pack_pair(a, b):
    return pltpu.pack_elementwise([a, b], packed_dtype=BF16)


def _unpack_pair(w):
    return (pltpu.unpack_elementwise(w, index=0, packed_dtype=BF16, unpacked_dtype=F32),
            pltpu.unpack_elementwise(w, index=1, packed_dtype=BF16, unpacked_dtype=F32))


def _conv_pitch(n2):
    return n2 + 8


def _hyena_conv_kernel(z_ref, g_ref, kf_ref, w1_ref, w2f_ref, w2i_ref, w3_ref, ta_ref, bias_ref, o_ref,
                       x_scr, a_scr, *, n1, n2):
    pitch = _conv_pitch(n2)
    half = n1 // 2
    groups = n1 // HY_TW_SPLIT

    def pack_in(t1, carry):
        rows = pl.ds(pl.multiple_of(t1 * n2, n2), n2)
        x_scr[pl.ds(pl.multiple_of(t1 * pitch, 8), n2), :] = _pack_pair(
            z_ref[0, rows, :].astype(F32), z_ref[1, rows, :].astype(F32))
        return carry

    lax.fori_loop(0, half, pack_in, 0)

    def stage1(j, carry):
        cols = []
        for u in range(2):
            xr, xi = _unpack_pair(x_scr[pl.ds(2 * j + u, half, stride=pitch), :])
            cols.append(jnp.concatenate([xr, xi], axis=0).astype(BF16))
        out = jnp.dot(w1_ref[...], jnp.concatenate(cols, axis=1), preferred_element_type=F32)
        for u in range(2):
            o = out[:, u * LANE:(u + 1) * LANE]
            a_scr[pl.ds(2 * j + u, n1, stride=pitch), :] = _pack_pair(o[:n1], o[n1:])
        return carry

    lax.fori_loop(0, n2 // 2, stage1, 0, unroll=4)

    for kb in range(HY_TW_SPLIT):
        def stage2(jp, carry, kb=kb):
            rows, tws, stacks = [], [], []
            for u in range(2):
                ka = 2 * jp + u
                k1 = ka * HY_TW_SPLIT + kb
                rows.append(pl.ds(pl.multiple_of(k1 * pitch, 8), n2))
                ar, ai = _unpack_pair(a_scr[rows[u], :])
                twr, twi = ta_ref[0, ka], ta_ref[1, ka]
                tws.append((twr, twi))
                stacks.append(jnp.concatenate([ar * twr - ai * twi, ar * twi + ai * twr], axis=0).astype(BF16))
            spec = jnp.dot(w2f_ref[kb], jnp.concatenate(stacks, axis=1), preferred_element_type=F32)
            prods = []
            for u in range(2):
                k1 = (2 * jp + u) * HY_TW_SPLIT + kb
                sr, si = spec[:n2, u * LANE:(u + 1) * LANE], spec[n2:, u * LANE:(u + 1) * LANE]
                kr = kf_ref[0, 0, k1].astype(F32)
                ki = kf_ref[0, 1, k1].astype(F32)
                prods.append(jnp.concatenate([sr * kr - si * ki, sr * ki + si * kr], axis=0).astype(BF16))
            back = jnp.dot(w2i_ref[kb], jnp.concatenate(prods, axis=1), preferred_element_type=F32)
            for u in range(2):
                br, bi = back[:n2, u * LANE:(u + 1) * LANE], back[n2:, u * LANE:(u + 1) * LANE]
                twr, twi = tws[u]
                a_scr[rows[u], :] = _pack_pair(br * twr + bi * twi, bi * twr - br * twi)
            return carry

        lax.fori_loop(0, groups // 2, stage2, 0, unroll=2)

    def stage3(j, carry):
        cols = []
        for u in range(2):
            br, bi = _unpack_pair(a_scr[pl.ds(2 * j + u, n1, stride=pitch), :])
            cols.append(jnp.concatenate([br, bi], axis=0).astype(BF16))
        y = jnp.dot(w3_ref[...], jnp.concatenate(cols, axis=1), preferred_element_type=F32)
        for u in range(2):
            sl = pl.ds(2 * j + u, half, stride=pitch)
            za, zb = _unpack_pair(x_scr[sl, :])
            yu = y[:, u * LANE:(u + 1) * LANE]
            x_scr[sl, :] = _pack_pair(yu[:half] + bias_ref[...] * za, yu[half:] + bias_ref[...] * zb)
        return carry

    lax.fori_loop(0, n2 // 2, stage3, 0, unroll=4)

    def gate_out(t1, carry):
        rows = pl.ds(pl.multiple_of(t1 * n2, n2), n2)
        ta, tb = _unpack_pair(x_scr[pl.ds(pl.multiple_of(t1 * pitch, 8), n2), :])
        o_ref[0, rows, :] = (g_ref[0, rows, :].astype(F32) * ta).astype(o_ref.dtype)
        o_ref[1, rows, :] = (g_ref[1, rows, :].astype(F32) * tb).astype(o_ref.dtype)
        return carry

    lax.fori_loop(0, half, gate_out, 0)


def _hyena_conv(z, gate, kf, order, bias, consts):
    s, seq_len, c = z.shape
    n1, n2 = _fft_dims(seq_len)
    pitch = _conv_pitch(n2)
    assert s % 2 == 0 and c % LANE == 0 and (n1 // HY_TW_SPLIT) % 2 == 0
    seq_blk = pl.BlockSpec((2, seq_len, LANE), lambda cc, p: (p, 0, cc))
    return pl.pallas_call(
        functools.partial(_hyena_conv_kernel, n1=n1, n2=n2),
        grid=(c // LANE, s // 2),
        in_specs=[
            seq_blk,
            pl.BlockSpec((2, seq_len, LANE), lambda cc, p: (p, 0, cc), pipeline_mode=pl.Buffered(1)),
            pl.BlockSpec((1, 2, n1, n2, LANE), lambda cc, p: (order, 0, 0, 0, cc), pipeline_mode=pl.Buffered(1)),
            _resident(consts["w1_complex"].shape),
            _resident(consts["w2f"].shape),
            _resident(consts["w2i"].shape),
            _resident(consts["w3"].shape),
            _resident(consts["ta"].shape),
            pl.BlockSpec((1, LANE), lambda cc, p: (0, cc)),
        ],
        out_specs=seq_blk,
        out_shape=jax.ShapeDtypeStruct(z.shape, BF16),
        scratch_shapes=[pltpu.VMEM((n1 // 2 * pitch, LANE), jnp.uint32),
                        pltpu.VMEM((n1 * pitch, LANE), jnp.uint32)],
        compiler_params=_cparams("parallel", "arbitrary"),
        name="hyena_conv",
    )(z, gate, kf, consts["w1_complex"], consts["w2f"], consts["w2i"], consts["w3"], consts["ta"],
      bias.astype(F32).reshape(1, c))


def _s2f_kernel(af_ref, ab_ref, tw_ref, w2_ref, o_ref):
    n2, c = af_ref.shape[-2:]
    twr = _lane_tile(tw_ref[0, 0], c)
    twi = _lane_tile(tw_ref[1, 0], c)
    hf = jnp.dot(w2_ref[...], _twiddled_stack(af_ref, twr, twi), preferred_element_type=F32)
    hb = jnp.dot(w2_ref[...], _twiddled_stack(ab_ref, twr, twi), preferred_element_type=F32)
    o_ref[0, 0, 0] = (hf[:n2] + hb[:n2]).astype(o_ref.dtype)
    o_ref[0, 1, 0] = (hf[n2:] - hb[n2:]).astype(o_ref.dtype)


def _filter_spectrum(a_filt, consts):
    _, _, n1, n2, c = a_filt.shape
    return pl.pallas_call(
        _s2f_kernel,
        grid=(HY_ORDER, n1),
        in_specs=[
            pl.BlockSpec((1, 2, 1, n2, c), lambda o, k: (2 * o, 0, k, 0, 0)),
            pl.BlockSpec((1, 2, 1, n2, c), lambda o, k: (2 * o + 1, 0, k, 0, 0)),
            pl.BlockSpec((2, 1, n2, LANE), lambda o, k: (0, k, 0, 0)),
            _resident((2 * n2, 2 * n2)),
        ],
        out_specs=pl.BlockSpec((1, 2, 1, n2, c), lambda o, k: (o, 0, k, 0, 0)),
        out_shape=jax.ShapeDtypeStruct((HY_ORDER, 2, n1, n2, c), BF16),
        compiler_params=_cparams("parallel", "arbitrary"),
        name="hyena_filter_spectrum",
    )(a_filt, a_filt, consts["tw"], consts["w2"])


def _hyena_spectra(seq_len, w1, b1, w2, b2, freq, w3):
    consts = _dft_constants(seq_len)
    filt = _hyena_filters(seq_len, w1, b1, w2, b2, freq, w3)
    a_filt = _fft_stage1(filt, consts["w1_real"], pair=False)
    return _filter_spectrum(a_filt, consts)


def _hyena(h3d, conv_w, conv_b, kf, hy_bias):
    seq_len = h3d.shape[1]
    consts = _dft_constants(seq_len)
    v, x1, x2 = _short_conv(h3d, conv_w, conv_b)
    z = v
    for n, gate in enumerate((x1, x2)):
        z = _hyena_conv(z, gate, kf, n, hy_bias[n], consts)
    return z


def _layernorm(y, g, b):
    mu = jnp.mean(y, axis=-1, keepdims=True)
    d = y - mu
    var = jnp.mean(d * d, axis=-1, keepdims=True)
    return d * lax.rsqrt(var + LN_EPS) * g + b


def _merge_kernel(g_ref, a_ref, hb_ref, c_ref, x_ref, wa_ref, wb_ref, wc_ref, wo_ref, lg_ref, lb_ref, o_ref):
    d = D_MODEL
    merged = g_ref[:, 0:d].astype(F32) * jnp.dot(a_ref[...], wa_ref[...], preferred_element_type=F32)
    merged += g_ref[:, d:2 * d].astype(F32) * jnp.dot(hb_ref[...], wb_ref[...], preferred_element_type=F32)
    merged += g_ref[:, 2 * d:3 * d].astype(F32) * jnp.dot(c_ref[...], wc_ref[...], preferred_element_type=F32)
    mix = jnp.dot(merged.astype(BF16), wo_ref[...], preferred_element_type=F32)
    o_ref[...] = _layernorm(DEEPNORM_ALPHA * x_ref[...] + mix, lg_ref[...], lb_ref[...])


def _merge(h2d, a, hb, c, x2d, wa, wb, wc, wo, ln_g, ln_b):
    m = x2d.shape[0]
    tm = MERGE_TM
    row = lambda width: pl.BlockSpec((tm, width), lambda i: (i, 0))
    return pl.pallas_call(
        _merge_kernel,
        grid=(m // tm,),
        in_specs=[
            row(N_BRANCH * D_MODEL), row(NA_WIDTH), row(HY_WIDTH), row(SWA_WIDTH), row(D_MODEL),
            _resident((NA_WIDTH, D_MODEL)), _resident((HY_WIDTH, D_MODEL)), _resident((SWA_WIDTH, D_MODEL)),
            _resident((D_MODEL, D_MODEL)), _resident((1, D_MODEL)), _resident((1, D_MODEL)),
        ],
        out_specs=row(D_MODEL),
        out_shape=jax.ShapeDtypeStruct((m, D_MODEL), F32),
        compiler_params=_cparams("parallel"),
        name="merge_ln",
    )(h2d, a, hb, c, x2d, wa, wb, wc, wo, ln_g, ln_b)


def _mlp_kernel(x_ref, wu_ref, bu_ref, wd_ref, bd_ref, lg_ref, lb_ref, o_ref):
    x = x_ref[...]
    xb = x.astype(BF16)
    acc = DEEPNORM_ALPHA * x + bd_ref[...]
    for c in range(D_FF // MLP_FF_CHUNK):
        sl = slice(c * MLP_FF_CHUNK, (c + 1) * MLP_FF_CHUNK)
        up = jnp.dot(xb, wu_ref[:, sl], preferred_element_type=F32) + bu_ref[:, sl]
        up = jnp.square(jnp.maximum(up, 0.0))
        acc += jnp.dot(up.astype(BF16), wd_ref[sl, :], preferred_element_type=F32)
    o_ref[...] = _layernorm(acc, lg_ref[...], lb_ref[...])


def _mlp(x2d, wu, bu, wd, bd, ln_g, ln_b):
    m = x2d.shape[0]
    tm = MLP_TM
    row = pl.BlockSpec((tm, D_MODEL), lambda i: (i, 0))
    return pl.pallas_call(
        _mlp_kernel,
        grid=(m // tm,),
        in_specs=[
            row, _resident((D_MODEL, D_FF)), _resident((1, D_FF)), _resident((D_FF, D_MODEL)),
            _resident((1, D_MODEL)), _resident((1, D_MODEL)), _resident((1, D_MODEL)),
        ],
        out_specs=row,
        out_shape=jax.ShapeDtypeStruct((m, D_MODEL), F32),
        compiler_params=_cparams("parallel"),
        name="mlp_ln",
    )(x2d, wu, bu, wd, bd, ln_g, ln_b)


def _permute_in_columns(a):
    return jnp.concatenate([a[..., OFF_GATE:], a[..., :OFF_GATE]], axis=-1)


def _prepare_layer(l, p):
    row = lambda a: a.astype(F32).reshape(1, -1)
    return dict(
        w_in=_permute_in_columns(p["w_in"][l]).astype(BF16),
        b_in=row(_permute_in_columns(p["b_in"][l])),
        conv_w=p["hy_conv_w"][l], conv_b=p["hy_conv_b"][l], hy_bias=p["hy_bias"][l],
        filt=(p["hy_filt_w1"][l], p["hy_filt_b1"][l], p["hy_filt_w2"][l], p["hy_filt_b2"][l],
              p["hy_filt_freq"][l], p["hy_filt_w3"][l]),
        na_bias=_na_bias_table(p["na_rpb"][l]),
        sink=p["swa_sink"][l],
        wa=p["w_branch_a"][l].astype(BF16), wb=p["w_branch_b"][l].astype(BF16),
        wc=p["w_branch_c"][l].astype(BF16), wo=p["w_out"][l].astype(BF16),
        ln1_g=row(p["ln1_g"][l]), ln1_b=row(p["ln1_b"][l]),
        wu=p["w_up"][l].astype(BF16), bu=row(p["b_up"][l]),
        wd=p["w_down"][l].astype(BF16), bd=row(p["b_down"][l]),
        ln2_g=row(p["ln2_g"][l]), ln2_b=row(p["ln2_b"][l]),
    )


def _encoder_block(x, lp, swa_table):
    b_sz, seq_len, _ = x.shape
    m = b_sz * seq_len
    x2d = x.reshape(m, D_MODEL)
    h2d = _inproj(x2d, lp["w_in"], lp["b_in"])
    h3d = h2d.reshape(b_sz, seq_len, D_IN)
    a = _na_attention(h3d, lp["na_bias"])
    kf = _hyena_spectra(seq_len, *lp["filt"])
    hb = _hyena(h3d, lp["conv_w"], lp["conv_b"], kf, lp["hy_bias"])
    c = _swa_attention(h3d, lp["sink"], swa_table)
    x1 = _merge(h2d, a.reshape(m, NA_WIDTH), hb.reshape(m, HY_WIDTH), c.reshape(m, SWA_WIDTH), x2d,
                lp["wa"], lp["wb"], lp["wc"], lp["wo"], lp["ln1_g"], lp["ln1_b"])
    x2 = _mlp(x1, lp["wu"], lp["bu"], lp["wd"], lp["bd"], lp["ln2_g"], lp["ln2_b"])
    return x2.reshape(b_sz, seq_len, D_MODEL)


def kernel(x_prompt, x_sample, w_in, b_in, hy_conv_w, hy_conv_b, hy_filt_w1, hy_filt_b1, hy_filt_w2,
           hy_filt_b2, hy_filt_freq, hy_filt_w3, hy_bias, na_rpb, swa_sink, w_branch_a, w_branch_b,
           w_branch_c, w_out, ln1_g, ln1_b, w_up, b_up, w_down, b_down, ln2_g, ln2_b):
    params = dict(w_in=w_in, b_in=b_in, hy_conv_w=hy_conv_w, hy_conv_b=hy_conv_b, hy_filt_w1=hy_filt_w1,
                  hy_filt_b1=hy_filt_b1, hy_filt_w2=hy_filt_w2, hy_filt_b2=hy_filt_b2,
                  hy_filt_freq=hy_filt_freq, hy_filt_w3=hy_filt_w3, hy_bias=hy_bias, na_rpb=na_rpb,
                  swa_sink=swa_sink, w_branch_a=w_branch_a, w_branch_b=w_branch_b, w_branch_c=w_branch_c,
                  w_out=w_out, ln1_g=ln1_g, ln1_b=ln1_b, w_up=w_up, b_up=b_up, w_down=w_down,
                  b_down=b_down, ln2_g=ln2_g, ln2_b=ln2_b)
    swa_table = _swa_table()
    y_prompt = x_prompt
    y_sample = x_sample
    for l in range(DEPTH):
        lp = _prepare_layer(l, params)
        y_prompt = _encoder_block(y_prompt, lp, swa_table)
        y_sample = _encoder_block(y_sample, lp, swa_table)
    return (y_prompt, y_sample)
```

```python
import functools
import math

import numpy as np
import jax
import jax.numpy as jnp
from jax import lax
from jax.experimental import pallas as pl
from jax.experimental.pallas import tpu as pltpu

F32 = jnp.float32
BF16 = jnp.bfloat16

D_MODEL = 1024
DEPTH = 2
HEAD_DIM = 64
GRID_W = 64
NA_HEADS = 8
NA_WIN_ROWS = 8
NA_WIN_COLS = 16
NA_WIDTH = NA_HEADS * HEAD_DIM
HY_WIDTH = D_MODEL // 2
HY_ORDER = 2
HY_SHORT_CONV = 3
HY_POS_BANDS = 16
HY_EMB_DIM = 1 + 2 * HY_POS_BANDS
HY_EMB_PAD = 128
HY_FILTER_HIDDEN = 64
HY_FAST_DECAY_PCT = 0.3
HY_SLOW_DECAY_PCT = 1.5
HY_DECAY_TARGET = 1e-2
SWA_HEADS = 8
SWA_KV_HEADS = 2
SWA_GROUP = SWA_HEADS // SWA_KV_HEADS
SWA_WIDTH = SWA_HEADS * HEAD_DIM
SWA_KV_WIDTH = SWA_KV_HEADS * HEAD_DIM
SWA_WINDOW = 128
SWA_BLOCK = 128
N_BRANCH = 3
D_FF = 4 * D_MODEL
OFF_HY = 3 * NA_WIDTH
OFF_SWA = OFF_HY + 3 * HY_WIDTH
OFF_GATE = OFF_SWA + SWA_WIDTH + 2 * SWA_KV_WIDTH
D_IN = OFF_GATE + N_BRANCH * D_MODEL
DEEPNORM_ALPHA = (2 * DEPTH) ** 0.25
LN_EPS = 1e-5
NEG_BIG = -1e30
LOG2E = 1.4426950408889634

P_GATE = 0
P_NA = N_BRANCH * D_MODEL
P_HY = P_NA + 3 * NA_WIDTH
P_SWA = P_HY + 3 * HY_WIDTH
VT_WIDTH = NA_WIDTH + SWA_KV_WIDTH

LANE = 128
VMEM_LIMIT_BYTES = 56 * 1024 * 1024

NA_PAIR_ROWS = NA_WIN_ROWS + 2
INPROJ_TM = 2048
INPROJ_TN = 768
MERGE_TM = 512
MLP_TM = 512
MLP_FF_CHUNK = 1024
CONV_TL = 256
FILT_TL = 512
FFT_LANE_TILE = 2048
HY_TW_SPLIT = 8


def _cparams(*sem):
    return pltpu.CompilerParams(dimension_semantics=sem, vmem_limit_bytes=VMEM_LIMIT_BYTES)


def _resident(shape):
    nd = len(shape)
    return pl.BlockSpec(shape, lambda *_: (0,) * nd, pipeline_mode=pl.Buffered(1))


def _inproj_kernel(x_ref, w_ref, b_ref, wvt_ref, bvt_ref, o_ref, vt_ref, xb_ref, *, n_gate_tiles):
    j = pl.program_id(1)

    @pl.when(j == 0)
    def _():
        xb_ref[...] = x_ref[...].astype(BF16)
        vt = lax.dot_general(wvt_ref[...], xb_ref[...], (((1,), (1,)), ((), ())), preferred_element_type=F32)
        vt_ref[...] = (vt + bvt_ref[...]).astype(vt_ref.dtype)

    def tile(apply_sigmoid):
        acc = jnp.dot(xb_ref[...], w_ref[...], preferred_element_type=F32) + b_ref[...]
        if apply_sigmoid:
            acc = 1.0 / (1.0 + jnp.exp(-acc))
        o_ref[...] = acc.astype(o_ref.dtype)

    pl.when(j < n_gate_tiles)(lambda: tile(True))
    pl.when(j >= n_gate_tiles)(lambda: tile(False))


def _inproj(x2d, w_bf, b_row, wvt_bf, bvt_col):
    m = x2d.shape[0]
    tm = min(INPROJ_TM, m)
    tn = INPROJ_TN
    assert m % tm == 0 and D_IN % tn == 0 and P_NA % tn == 0
    return pl.pallas_call(
        functools.partial(_inproj_kernel, n_gate_tiles=P_NA // tn),
        grid=(m // tm, D_IN // tn),
        in_specs=[
            pl.BlockSpec((tm, D_MODEL), lambda i, j: (i, 0)),
            pl.BlockSpec((D_MODEL, tn), lambda i, j: (0, j)),
            pl.BlockSpec((1, tn), lambda i, j: (0, j)),
            _resident((VT_WIDTH, D_MODEL)),
            _resident((VT_WIDTH, 1)),
        ],
        out_specs=[pl.BlockSpec((tm, tn), lambda i, j: (i, j)),
                   pl.BlockSpec((VT_WIDTH, tm), lambda i, j: (0, i))],
        out_shape=[jax.ShapeDtypeStruct((m, D_IN), BF16), jax.ShapeDtypeStruct((VT_WIDTH, m), BF16)],
        scratch_shapes=[pltpu.VMEM((tm, D_MODEL), BF16)],
        compiler_params=_cparams("parallel", "arbitrary"),
        name="inproj",
    )(x2d, w_bf, b_row, wvt_bf, bvt_col)


def _na_bias_table(rpb):
    kr = NA_WIN_ROWS
    pad = GRID_W - NA_WIN_COLS
    p = jnp.pad(rpb.astype(F32) * LOG2E, ((0, 0), (0, 0), (pad, pad)))
    cols = jnp.stack([p[:, :, GRID_W - 1 - w:2 * GRID_W - 1 - w] for w in range(GRID_W)], axis=2)
    t = jnp.stack([cols[:, kr - 1 - d:2 * kr - 1 - d] for d in range(kr)], axis=0)
    w = np.arange(GRID_W)[:, None]
    kc = np.arange(GRID_W)[None, :]
    col_start = np.clip(w - NA_WIN_COLS // 2, 0, GRID_W - NA_WIN_COLS)
    valid = (kc >= col_start) & (kc < col_start + NA_WIN_COLS)
    t = jnp.where(valid[None, None, None], t, NEG_BIG)
    t = jnp.transpose(t, (0, 1, 2, 4, 3))
    return t.reshape(kr, NA_HEADS, kr * GRID_W, GRID_W)


def _na_pair_geometry(p, rows):
    base = min(max(2 * p - NA_WIN_ROWS // 2, 0), rows - NA_PAIR_ROWS)
    geo = []
    for j in range(2):
        r = 2 * p + j
        rs = min(max(r - NA_WIN_ROWS // 2, 0), rows - NA_WIN_ROWS)
        assert 0 <= rs - base <= NA_PAIR_ROWS - NA_WIN_ROWS
        geo.append((rs - base, r - rs))
    return base, geo


def _na_pair_table(table, rows):
    n_pairs = rows // 2
    variants = []
    for p in (0, 1, 2, n_pairs - 2, n_pairs - 1):
        _, geo = _na_pair_geometry(p, rows)
        cols = []
        for off, delta in geo:
            after = NA_PAIR_ROWS - NA_WIN_ROWS - off
            cols.append(jnp.pad(table[delta], ((0, 0), (off * GRID_W, after * GRID_W), (0, 0)),
                                constant_values=NEG_BIG))
        variants.append(jnp.concatenate(cols, axis=-1))
    return jnp.stack(variants, axis=0)


def _na_kernel(q_ref, k_ref, vt_ref, bias_ref, o_ref, *, rows):
    span = NA_PAIR_ROWS * GRID_W
    base = jnp.clip(2 * pl.program_id(1) - NA_WIN_ROWS // 2, 0, rows - NA_PAIR_ROWS)
    start = pl.multiple_of(base * GRID_W, 2 * GRID_W)
    q = q_ref[0]
    k = k_ref[0, pl.ds(start, span), :]
    vt = vt_ref[:, pl.ds(start, span)]
    scale = HEAD_DIM ** -0.5 * LOG2E
    lane = lax.broadcasted_iota(jnp.int32, (1, LANE), 1)
    weights, sums = [], []
    for h in range(NA_HEADS):
        grp = slice((h // 2) * LANE, (h // 2 + 1) * LANE)
        mine = (lane < HEAD_DIM) if h % 2 == 0 else (lane >= HEAD_DIM)
        qh = jnp.where(mine, q[:, grp], jnp.zeros((), q.dtype))
        s = lax.dot_general(k[:, grp], qh, (((1,), (1,)), ((), ())), preferred_element_type=F32)
        s = s * scale + bias_ref[0, h]
        p = jnp.exp2(s - jnp.max(s, axis=0, keepdims=True))
        sums.append(jnp.sum(p, axis=0, keepdims=True))
        weights.append(p.astype(BF16))
    outs = [jnp.dot(vt[h * HEAD_DIM:(h + 1) * HEAD_DIM, :], weights[h], preferred_element_type=F32) / sums[h]
            for h in range(NA_HEADS)]
    o_ref[0] = jnp.concatenate(outs, axis=0).T.astype(o_ref.dtype)


def _na_attention(h3d, vt, bias_table):
    b_sz, seq_len, _ = h3d.shape
    rows = seq_len // GRID_W
    n_pairs = rows // 2
    assert rows % 2 == 0 and n_pairs >= 5
    cb = P_NA // NA_WIDTH
    span = NA_PAIR_ROWS * GRID_W
    pair_table = _na_pair_table(bias_table, rows)

    def variant(p):
        return jnp.where(p < 2, p, jnp.where(p > n_pairs - 3, p - (n_pairs - 5), 2))

    return pl.pallas_call(
        functools.partial(_na_kernel, rows=rows),
        grid=(b_sz, n_pairs),
        in_specs=[
            pl.BlockSpec((1, 2 * GRID_W, NA_WIDTH), lambda b, p: (b, p, cb)),
            pl.BlockSpec((1, seq_len, NA_WIDTH), lambda b, p: (b, 0, cb + 1), pipeline_mode=pl.Buffered(1)),
            pl.BlockSpec((NA_WIDTH, seq_len), lambda b, p: (0, b), pipeline_mode=pl.Buffered(1)),
            pl.BlockSpec((1, NA_HEADS, span, 2 * GRID_W), lambda b, p: (variant(p), 0, 0, 0)),
        ],
        out_specs=pl.BlockSpec((1, 2 * GRID_W, NA_WIDTH), lambda b, p: (b, p, 0)),
        out_shape=jax.ShapeDtypeStruct((b_sz, seq_len, NA_WIDTH), BF16),
        compiler_params=_cparams("parallel", "arbitrary"),
        name="na_attention",
    )(h3d, h3d, vt, pair_table)


def _swa_table():
    span = SWA_BLOCK + 2 * SWA_WINDOW
    slopes = 2.0 ** (-8.0 * (np.arange(SWA_HEADS, dtype=np.float64) + 1.0) / SWA_HEADS)
    kk = np.arange(span)[:, None]
    t = np.arange(SWA_BLOCK)[None, :]
    variants = []
    for off in (0, SWA_BLOCK, 2 * SWA_BLOCK):
        rel = np.abs(kk - off - t)
        a = -slopes[:, None, None] * rel[None].astype(np.float64) * LOG2E
        variants.append(np.where((rel <= SWA_WINDOW)[None], a, NEG_BIG))
    return jnp.asarray(np.stack(variants), F32)


def _swa_kernel(sink_ref, q_ref, k_ref, vt_ref, tab_ref, o_ref, *, seq_len):
    span = SWA_BLOCK + 2 * SWA_WINDOW
    start = pl.multiple_of(jnp.clip((pl.program_id(1) - 1) * SWA_BLOCK, 0, seq_len - span), SWA_BLOCK)
    q = q_ref[0]
    k = k_ref[0, pl.ds(start, span), :]
    vt = vt_ref[:, pl.ds(start, span)]
    scale = HEAD_DIM ** -0.5 * LOG2E
    lane = lax.broadcasted_iota(jnp.int32, (1, LANE), 1)
    weights, sums = [], []
    for h in range(SWA_HEADS):
        g = h // SWA_GROUP
        grp = slice((h % SWA_GROUP) * LANE, (h % SWA_GROUP + 1) * LANE)
        mine = (lane < HEAD_DIM) if g == 0 else (lane >= HEAD_DIM)
        qh = jnp.where(mine, q[:, grp], jnp.zeros((), q.dtype))
        s = lax.dot_general(k, qh, (((1,), (1,)), ((), ())), preferred_element_type=F32)
        s = s * scale + tab_ref[0, h]
        sink = sink_ref[h] * LOG2E
        m = jnp.maximum(jnp.max(s, axis=0, keepdims=True), sink)
        p = jnp.exp2(s - m)
        sums.append(jnp.sum(p, axis=0, keepdims=True) + jnp.exp2(sink - m))
        weights.append(p.astype(BF16))
    outs = [jnp.dot(vt[(h // SWA_GROUP) * HEAD_DIM:(h // SWA_GROUP + 1) * HEAD_DIM, :], weights[h],
                    preferred_element_type=F32) / sums[h] for h in range(SWA_HEADS)]
    o_ref[0] = jnp.concatenate(outs, axis=0).T.astype(o_ref.dtype)


def _swa_attention(h3d, vt, sink, table):
    b_sz, seq_len, _ = h3d.shape
    nb = seq_len // SWA_BLOCK
    span = SWA_BLOCK + 2 * SWA_WINDOW
    assert SWA_KV_HEADS == 2 and SWA_BLOCK == LANE and nb >= 3
    qb = P_SWA // SWA_WIDTH
    kb = (P_SWA + SWA_WIDTH) // SWA_KV_WIDTH

    def variant(i):
        return jnp.where(i == 0, 0, jnp.where(i == nb - 1, 2, 1))

    return pl.pallas_call(
        functools.partial(_swa_kernel, seq_len=seq_len),
        grid=(b_sz, nb),
        in_specs=[
            pl.BlockSpec(memory_space=pltpu.SMEM),
            pl.BlockSpec((1, SWA_BLOCK, SWA_WIDTH), lambda b, i: (b, i, qb)),
            pl.BlockSpec((1, seq_len, SWA_KV_WIDTH), lambda b, i: (b, 0, kb)),
            pl.BlockSpec((SWA_KV_WIDTH, seq_len), lambda b, i: (NA_WIDTH // SWA_KV_WIDTH, b)),
            pl.BlockSpec((1, SWA_HEADS, span, SWA_BLOCK), lambda b, i: (variant(i), 0, 0, 0)),
        ],
        out_specs=pl.BlockSpec((1, SWA_BLOCK, SWA_WIDTH), lambda b, i: (b, i, 0)),
        out_shape=jax.ShapeDtypeStruct((b_sz, seq_len, SWA_WIDTH), BF16),
        compiler_params=_cparams("parallel", "arbitrary"),
        name="swa_attention",
    )(sink.astype(F32), h3d, h3d, vt, table)


def _conv_kernel(x_ref, prev_ref, next_ref, w_ref, b_ref, v_ref, x1_ref, x2_ref, *, n_tiles):
    i = pl.program_id(1)
    tl = x_ref.shape[1]
    halo = prev_ref.shape[1]
    row = lax.broadcasted_iota(jnp.int32, (tl, 1), 0)
    outs = (v_ref, x1_ref, x2_ref)
    for c in range(3):
        sl = slice(c * HY_WIDTH, (c + 1) * HY_WIDTH)
        x = x_ref[0, :, sl].astype(F32)
        prev_row = jnp.where(i > 0, prev_ref[0, halo - 1:halo, sl].astype(F32), 0.0)
        next_row = jnp.where(i < n_tiles - 1, next_ref[0, 0:1, sl].astype(F32), 0.0)
        xm = jnp.where(row == 0, prev_row, pltpu.roll(x, 1, axis=0))
        xp = jnp.where(row == tl - 1, next_row, pltpu.roll(x, tl - 1, axis=0))
        u = xm * w_ref[0:1, sl] + x * w_ref[1:2, sl] + xp * w_ref[2:3, sl] + b_ref[:, sl]
        outs[c][0] = u.astype(BF16)


def _short_conv(h3d, conv_w, conv_b):
    b_sz, seq_len, _ = h3d.shape
    tl = CONV_TL
    halo = 16
    n_tiles = seq_len // tl
    width = 3 * HY_WIDTH
    cb = P_HY // width
    assert P_HY % width == 0 and seq_len % tl == 0
    per = tl // halo
    out = jax.ShapeDtypeStruct((b_sz, seq_len, HY_WIDTH), BF16)
    ospec = pl.BlockSpec((1, tl, HY_WIDTH), lambda b, i: (b, i, 0))
    return pl.pallas_call(
        functools.partial(_conv_kernel, n_tiles=n_tiles),
        grid=(b_sz, n_tiles),
        in_specs=[
            pl.BlockSpec((1, tl, width), lambda b, i: (b, i, cb)),
            pl.BlockSpec((1, halo, width), lambda b, i: (b, jnp.maximum(i * per - 1, 0), cb)),
            pl.BlockSpec((1, halo, width), lambda b, i: (b, jnp.minimum((i + 1) * per, seq_len // halo - 1), cb)),
            _resident((HY_SHORT_CONV, width)),
            _resident((1, width)),
        ],
        out_specs=[ospec, ospec, ospec],
        out_shape=[out, out, out],
        compiler_params=_cparams("parallel", "arbitrary"),
        name="hyena_short_conv",
    )(h3d, h3d, h3d, conv_w.astype(F32), conv_b.astype(F32).reshape(1, width))


def _filter_embedding(seq_len):
    t = np.linspace(0.0, 1.0, seq_len, dtype=np.float32).astype(np.float64)[:, None]
    w = (2.0 * math.pi * np.arange(seq_len, dtype=np.float32) / seq_len).astype(np.float32)
    bands = np.linspace(1e-4, HY_POS_BANDS - 1, HY_POS_BANDS, dtype=np.float32)
    ang = (w[:, None] * bands[None, :]).astype(np.float32).astype(np.float64)
    z = np.concatenate([t, np.cos(ang), -np.sin(ang)], axis=-1)
    zp = np.zeros((seq_len, HY_EMB_PAD), np.float32)
    zp[:, :HY_EMB_DIM] = z
    return zp


def _filter_deltas():
    min_decay = math.log(HY_DECAY_TARGET) / HY_SLOW_DECAY_PCT
    max_decay = math.log(HY_DECAY_TARGET) / HY_FAST_DECAY_PCT
    return np.abs(np.linspace(min_decay, max_decay, HY_WIDTH, dtype=np.float32))[None, :]


def _filter_kernel(z_ref, w1_ref, b1_ref, w2_ref, b2_ref, fr_ref, w3_ref, dl_ref, o_ref):
    hi = lax.Precision.HIGHEST
    tl = z_ref.shape[0]
    z = z_ref[...]
    fr = fr_ref[...]
    h = jnp.sin(fr * (jnp.dot(z, w1_ref[...], precision=hi, preferred_element_type=F32) + b1_ref[...]))
    h = jnp.sin(fr * (jnp.dot(h, w2_ref[...], precision=hi, preferred_element_type=F32) + b2_ref[...]))
    t = z[:, 0:1]
    window = jnp.exp(-t * dl_ref[...])
    row = pl.program_id(0) * tl + lax.broadcasted_iota(jnp.int32, (tl, 1), 0)
    for s in range(2 * HY_ORDER):
        sl = slice(s * HY_WIDTH, (s + 1) * HY_WIDTH)
        f = jnp.dot(h, w3_ref[:, sl], precision=hi, preferred_element_type=F32) * window
        if s % 2 == 1:
            f = jnp.where(row > 0, f, 0.0)
        o_ref[s] = f.astype(o_ref.dtype)


def _hyena_filters(seq_len, w1, b1, w2, b2, freq, w3):
    tl = min(FILT_TL, seq_len)
    z = jnp.asarray(_filter_embedding(seq_len))
    w1p = jnp.zeros((HY_EMB_PAD, HY_FILTER_HIDDEN), F32).at[:HY_EMB_DIM].set(w1.astype(F32))
    hid = HY_FILTER_HIDDEN
    n_f = 2 * HY_ORDER
    return pl.pallas_call(
        _filter_kernel,
        grid=(seq_len // tl,),
        in_specs=[
            pl.BlockSpec((tl, HY_EMB_PAD), lambda i: (i, 0)),
            _resident((HY_EMB_PAD, hid)), _resident((1, hid)),
            _resident((hid, hid)), _resident((1, hid)), _resident((1, hid)),
            _resident((hid, n_f * HY_WIDTH)), _resident((1, HY_WIDTH)),
        ],
        out_specs=pl.BlockSpec((n_f, tl, HY_WIDTH), lambda i: (0, i, 0)),
        out_shape=jax.ShapeDtypeStruct((n_f, seq_len, HY_WIDTH), BF16),
        compiler_params=_cparams("parallel"),
        name="hyena_filters",
    )(z, w1p, b1.astype(F32).reshape(1, hid), w2.astype(F32), b2.astype(F32).reshape(1, hid),
      freq.astype(F32).reshape(1, hid), w3.astype(F32), jnp.asarray(_filter_deltas()))


def _fft_dims(seq_len):
    n = 2 * seq_len
    n1 = 1 << ((n.bit_length() - 1) // 2)
    n2 = n // n1
    assert n1 * n2 == n and n1 == n2, "sequence length must give a square transform"
    return n1, n2


def _stack_complex(fr, fi):
    return np.block([[fr, -fi], [fi, fr]])


@functools.lru_cache(maxsize=None)
def _dft_constants(seq_len):
    n1, n2 = _fft_dims(seq_len)
    n = n1 * n2
    k1 = np.arange(n1)[:, None].astype(np.float64)
    t1 = np.arange(n1 // 2)[None, :].astype(np.float64)
    a1 = -2.0 * math.pi * k1 * t1 / n1
    f1r, f1i = np.cos(a1), np.sin(a1)
    w1_complex = _stack_complex(f1r, f1i)
    w1_real = np.concatenate([f1r, f1i], axis=0)
    k2 = np.arange(n2)[:, None].astype(np.float64)
    t2 = np.arange(n2)[None, :].astype(np.float64)
    a2 = -2.0 * math.pi * k2 * t2 / n2
    f2r, f2i = np.cos(a2), np.sin(a2)
    w2 = _stack_complex(f2r, f2i)
    w2_inv = _stack_complex(f2r, -f2i)
    at = -2.0 * math.pi * k1 * t2 / n
    tw = np.stack([np.cos(at), np.sin(at)], axis=0)[..., None]
    tw = np.broadcast_to(tw, (2, n1, n2, LANE)).astype(np.float32)
    g1r, g1i = f1r.T / n, -f1i.T / n
    w3 = _stack_complex(g1r, g1i)
    kb = np.arange(HY_TW_SPLIT)[:, None].astype(np.float64)
    ab = -2.0 * math.pi * kb * t2 / n
    tbr, tbi = np.cos(ab), np.sin(ab)
    w2f = np.stack([_stack_complex(f2r * tbr[b] - f2i * tbi[b], f2r * tbi[b] + f2i * tbr[b])
                    for b in range(HY_TW_SPLIT)])
    w2i = np.stack([_stack_complex(tbr[b][:, None] * f2r - tbi[b][:, None] * f2i,
                                   -(tbr[b][:, None] * f2i + tbi[b][:, None] * f2r))
                    for b in range(HY_TW_SPLIT)])
    ka = (np.arange(n1 // HY_TW_SPLIT) * HY_TW_SPLIT)[:, None].astype(np.float64)
    aa = -2.0 * math.pi * ka * t2 / n
    ta = np.stack([np.cos(aa), np.sin(aa)], axis=0)[..., None]
    ta = np.broadcast_to(ta, (2, n1 // HY_TW_SPLIT, n2, LANE)).astype(np.float32)
    as_bf = lambda a: jnp.asarray(a, F32).astype(BF16)
    return dict(w1_complex=as_bf(w1_complex), w1_real=as_bf(w1_real), w2=as_bf(w2), w2f=as_bf(w2f),
                w2i=as_bf(w2i), tw=jnp.asarray(tw), ta=jnp.asarray(ta), w3=as_bf(w3))


def _s1_kernel(w_ref, x_ref, o_ref):
    tl = x_ref.shape[-1]
    x = x_ref[...].reshape(-1, tl).astype(BF16)
    o_ref[0] = jnp.dot(w_ref[...], x, preferred_element_type=F32)


def _fft_stage1(u, w1, *, pair):
    s, seq_len, c = u.shape
    n1, n2 = _fft_dims(seq_len)
    per = 2 if pair else 1
    lanes = n2 * c
    tl = min(FFT_LANE_TILE, lanes)
    uv = u.reshape(s, n1 // 2, lanes)
    out = pl.pallas_call(
        _s1_kernel,
        grid=(s // per, lanes // tl),
        in_specs=[
            _resident(w1.shape),
            pl.BlockSpec((per, n1 // 2, tl), lambda p, j: (p, 0, j)),
        ],
        out_specs=pl.BlockSpec((1, 2 * n1, tl), lambda p, j: (p, 0, j)),
        out_shape=jax.ShapeDtypeStruct((s // per, 2 * n1, lanes), F32),
        compiler_params=_cparams("parallel", "arbitrary"),
        name="hyena_dft_stage1",
    )(w1, uv)
    return out.reshape(s // per, 2, n1, n2, c)


def _lane_tile(x, c):
    return jnp.concatenate([x] * (c // LANE), axis=-1)


def _twiddled_stack(a_ref, twr, twi):
    ar = a_ref[0, 0, 0]
    ai = a_ref[0, 1, 0]
    return jnp.concatenate([ar * twr - ai * twi, ar * twi + ai * twr], axis=0).astype(BF16)


def _pack_pair(a, b):
    return pltpu.pack_elementwise([a, b], packed_dtype=BF16)


def _unpack_pair(w):
    return (pltpu.unpack_elementwise(w, index=0, packed_dtype=BF16, unpacked_dtype=F32),
            pltpu.unpack_elementwise(w, index=1, packed_dtype=BF16, unpacked_dtype=F32))


def _conv_pitch(n2):
    return n2 + 8


def _hyena_conv_kernel(z_ref, g_ref, kf_ref, w1_ref, w2f_ref, w2i_ref, w3_ref, ta_ref, bias_ref, o_ref,
                       x_scr, a_scr, *, n1, n2):
    pitch = _conv_pitch(n2)
    half = n1 // 2
    groups = n1 // HY_TW_SPLIT

    def pack_in(t1, carry):
        rows = pl.ds(pl.multiple_of(t1 * n2, n2), n2)
        x_scr[pl.ds(pl.multiple_of(t1 * pitch, 8), n2), :] = _pack_pair(
            z_ref[0, rows, :].astype(F32), z_ref[1, rows, :].astype(F32))
        return carry

    lax.fori_loop(0, half, pack_in, 0)

    def stage1(j, carry):
        cols = []
        for u in range(2):
            xr, xi = _unpack_pair(x_scr[pl.ds(2 * j + u, half, stride=pitch), :])
            cols.append(jnp.concatenate([xr, xi], axis=0).astype(BF16))
        out = jnp.dot(w1_ref[...], jnp.concatenate(cols, axis=1), preferred_element_type=F32)
        for u in range(2):
            o = out[:, u * LANE:(u + 1) * LANE]
            a_scr[pl.ds(2 * j + u, n1, stride=pitch), :] = _pack_pair(o[:n1], o[n1:])
        return carry

    lax.fori_loop(0, n2 // 2, stage1, 0, unroll=4)

    for kb in range(HY_TW_SPLIT):
        def stage2(jp, carry, kb=kb):
            rows, tws, stacks = [], [], []
            for u in range(2):
                ka = 2 * jp + u
                k1 = ka * HY_TW_SPLIT + kb
                rows.append(pl.ds(pl.multiple_of(k1 * pitch, 8), n2))
                ar, ai = _unpack_pair(a_scr[rows[u], :])
                twr, twi = ta_ref[0, ka], ta_ref[1, ka]
                tws.append((twr, twi))
                stacks.append(jnp.concatenate([ar * twr - ai * twi, ar * twi + ai * twr], axis=0).astype(BF16))
            spec = jnp.dot(w2f_ref[kb], jnp.concatenate(stacks, axis=1), preferred_element_type=F32)
            prods = []
            for u in range(2):
                k1 = (2 * jp + u) * HY_TW_SPLIT + kb
                sr, si = spec[:n2, u * LANE:(u + 1) * LANE], spec[n2:, u * LANE:(u + 1) * LANE]
                kr = kf_ref[0, 0, k1].astype(F32)
                ki = kf_ref[0, 1, k1].astype(F32)
                prods.append(jnp.concatenate([sr * kr - si * ki, sr * ki + si * kr], axis=0).astype(BF16))
            back = jnp.dot(w2i_ref[kb], jnp.concatenate(prods, axis=1), preferred_element_type=F32)
            for u in range(2):
                br, bi = back[:n2, u * LANE:(u + 1) * LANE], back[n2:, u * LANE:(u + 1) * LANE]
                twr, twi = tws[u]
                a_scr[rows[u], :] = _pack_pair(br * twr + bi * twi, bi * twr - br * twi)
            return carry

        lax.fori_loop(0, groups // 2, stage2, 0, unroll=2)

    def stage3(j, carry):
        cols = []
        for u in range(2):
            br, bi = _unpack_pair(a_scr[pl.ds(2 * j + u, n1, stride=pitch), :])
            cols.append(jnp.concatenate([br, bi], axis=0).astype(BF16))
        y = jnp.dot(w3_ref[...], jnp.concatenate(cols, axis=1), preferred_element_type=F32)
        for u in range(2):
            sl = pl.ds(2 * j + u, half, stride=pitch)
            za, zb = _unpack_pair(x_scr[sl, :])
            yu = y[:, u * LANE:(u + 1) * LANE]
            x_scr[sl, :] = _pack_pair(yu[:half] + bias_ref[...] * za, yu[half:] + bias_ref[...] * zb)
        return carry

    lax.fori_loop(0, n2 // 2, stage3, 0, unroll=4)

    def gate_out(t1, carry):
        rows = pl.ds(pl.multiple_of(t1 * n2, n2), n2)
        ta, tb = _unpack_pair(x_scr[pl.ds(pl.multiple_of(t1 * pitch, 8), n2), :])
        o_ref[0, rows, :] = (g_ref[0, rows, :].astype(F32) * ta).astype(o_ref.dtype)
        o_ref[1, rows, :] = (g_ref[1, rows, :].astype(F32) * tb).astype(o_ref.dtype)
        return carry

    lax.fori_loop(0, half, gate_out, 0)


def _hyena_conv(z, gate, kf, order, bias, consts):
    s, seq_len, c = z.shape
    n1, n2 = _fft_dims(seq_len)
    pitch = _conv_pitch(n2)
    assert s % 2 == 0 and c % LANE == 0 and (n1 // HY_TW_SPLIT) % 2 == 0
    seq_blk = pl.BlockSpec((2, seq_len, LANE), lambda cc, p: (p, 0, cc))
    return pl.pallas_call(
        functools.partial(_hyena_conv_kernel, n1=n1, n2=n2),
        grid=(c // LANE, s // 2),
        in_specs=[
            seq_blk,
            pl.BlockSpec((2, seq_len, LANE), lambda cc, p: (p, 0, cc), pipeline_mode=pl.Buffered(1)),
            pl.BlockSpec((1, 2, n1, n2, LANE), lambda cc, p: (order, 0, 0, 0, cc), pipeline_mode=pl.Buffered(1)),
            _resident(consts["w1_complex"].shape),
            _resident(consts["w2f"].shape),
            _resident(consts["w2i"].shape),
            _resident(consts["w3"].shape),
            _resident(consts["ta"].shape),
            pl.BlockSpec((1, LANE), lambda cc, p: (0, cc)),
        ],
        out_specs=seq_blk,
        out_shape=jax.ShapeDtypeStruct(z.shape, BF16),
        scratch_shapes=[pltpu.VMEM((n1 // 2 * pitch, LANE), jnp.uint32),
                        pltpu.VMEM((n1 * pitch, LANE), jnp.uint32)],
        compiler_params=_cparams("parallel", "arbitrary"),
        name="hyena_conv",
    )(z, gate, kf, consts["w1_complex"], consts["w2f"], consts["w2i"], consts["w3"], consts["ta"],
      bias.astype(F32).reshape(1, c))


def _s2f_kernel(af_ref, ab_ref, tw_ref, w2_ref, o_ref):
    n2, c = af_ref.shape[-2:]
    twr = _lane_tile(tw_ref[0, 0], c)
    twi = _lane_tile(tw_ref[1, 0], c)
    hf = jnp.dot(w2_ref[...], _twiddled_stack(af_ref, twr, twi), preferred_element_type=F32)
    hb = jnp.dot(w2_ref[...], _twiddled_stack(ab_ref, twr, twi), preferred_element_type=F32)
    o_ref[0, 0, 0] = (hf[:n2] + hb[:n2]).astype(o_ref.dtype)
    o_ref[0, 1, 0] = (hf[n2:] - hb[n2:]).astype(o_ref.dtype)


def _filter_spectrum(a_filt, consts):
    _, _, n1, n2, c = a_filt.shape
    return pl.pallas_call(
        _s2f_kernel,
        grid=(HY_ORDER, n1),
        in_specs=[
            pl.BlockSpec((1, 2, 1, n2, c), lambda o, k: (2 * o, 0, k, 0, 0)),
            pl.BlockSpec((1, 2, 1, n2, c), lambda o, k: (2 * o + 1, 0, k, 0, 0)),
            pl.BlockSpec((2, 1, n2, LANE), lambda o, k: (0, k, 0, 0)),
            _resident((2 * n2, 2 * n2)),
        ],
        out_specs=pl.BlockSpec((1, 2, 1, n2, c), lambda o, k: (o, 0, k, 0, 0)),
        out_shape=jax.ShapeDtypeStruct((HY_ORDER, 2, n1, n2, c), BF16),
        compiler_params=_cparams("parallel", "arbitrary"),
        name="hyena_filter_spectrum",
    )(a_filt, a_filt, consts["tw"], consts["w2"])


def _hyena_spectra(seq_len, w1, b1, w2, b2, freq, w3):
    consts = _dft_constants(seq_len)
    filt = _hyena_filters(seq_len, w1, b1, w2, b2, freq, w3)
    a_filt = _fft_stage1(filt, consts["w1_real"], pair=False)
    return _filter_spectrum(a_filt, consts)


def _hyena(h3d, conv_w, conv_b, kf, hy_bias):
    seq_len = h3d.shape[1]
    consts = _dft_constants(seq_len)
    v, x1, x2 = _short_conv(h3d, conv_w, conv_b)
    z = v
    for n, gate in enumerate((x1, x2)):
        z = _hyena_conv(z, gate, kf, n, hy_bias[n], consts)
    return z


def _layernorm(y, g, b):
    mu = jnp.mean(y, axis=-1, keepdims=True)
    d = y - mu
    var = jnp.mean(d * d, axis=-1, keepdims=True)
    return d * lax.rsqrt(var + LN_EPS) * g + b


def _merge_kernel(g_ref, a_ref, hb_ref, c_ref, x_ref, wa_ref, wb_ref, wc_ref, wo_ref, lg_ref, lb_ref, o_ref):
    d = D_MODEL
    merged = g_ref[:, 0:d].astype(F32) * jnp.dot(a_ref[...], wa_ref[...], preferred_element_type=F32)
    merged += g_ref[:, d:2 * d].astype(F32) * jnp.dot(hb_ref[...], wb_ref[...], preferred_element_type=F32)
    merged += g_ref[:, 2 * d:3 * d].astype(F32) * jnp.dot(c_ref[...], wc_ref[...], preferred_element_type=F32)
    mix = jnp.dot(merged.astype(BF16), wo_ref[...], preferred_element_type=F32)
    o_ref[...] = _layernorm(DEEPNORM_ALPHA * x_ref[...] + mix, lg_ref[...], lb_ref[...])


def _merge(h2d, a, hb, c, x2d, wa, wb, wc, wo, ln_g, ln_b):
    m = x2d.shape[0]
    tm = MERGE_TM
    row = lambda width: pl.BlockSpec((tm, width), lambda i: (i, 0))
    return pl.pallas_call(
        _merge_kernel,
        grid=(m // tm,),
        in_specs=[
            row(N_BRANCH * D_MODEL), row(NA_WIDTH), row(HY_WIDTH), row(SWA_WIDTH), row(D_MODEL),
            _resident((NA_WIDTH, D_MODEL)), _resident((HY_WIDTH, D_MODEL)), _resident((SWA_WIDTH, D_MODEL)),
            _resident((D_MODEL, D_MODEL)), _resident((1, D_MODEL)), _resident((1, D_MODEL)),
        ],
        out_specs=row(D_MODEL),
        out_shape=jax.ShapeDtypeStruct((m, D_MODEL), F32),
        compiler_params=_cparams("parallel"),
        name="merge_ln",
    )(h2d, a, hb, c, x2d, wa, wb, wc, wo, ln_g, ln_b)


def _mlp_kernel(x_ref, wu_ref, bu_ref, wd_ref, bd_ref, lg_ref, lb_ref, o_ref):
    x = x_ref[...]
    xb = x.astype(BF16)
    acc = DEEPNORM_ALPHA * x + bd_ref[...]
    for c in range(D_FF // MLP_FF_CHUNK):
        sl = slice(c * MLP_FF_CHUNK, (c + 1) * MLP_FF_CHUNK)
        up = jnp.dot(xb, wu_ref[:, sl], preferred_element_type=F32) + bu_ref[:, sl]
        up = jnp.square(jnp.maximum(up, 0.0))
        acc += jnp.dot(up.astype(BF16), wd_ref[sl, :], preferred_element_type=F32)
    o_ref[...] = _layernorm(acc, lg_ref[...], lb_ref[...])


def _mlp(x2d, wu, bu, wd, bd, ln_g, ln_b):
    m = x2d.shape[0]
    tm = MLP_TM
    row = pl.BlockSpec((tm, D_MODEL), lambda i: (i, 0))
    return pl.pallas_call(
        _mlp_kernel,
        grid=(m // tm,),
        in_specs=[
            row, _resident((D_MODEL, D_FF)), _resident((1, D_FF)), _resident((D_FF, D_MODEL)),
            _resident((1, D_MODEL)), _resident((1, D_MODEL)), _resident((1, D_MODEL)),
        ],
        out_specs=row,
        out_shape=jax.ShapeDtypeStruct((m, D_MODEL), F32),
        compiler_params=_cparams("parallel"),
        name="mlp_ln",
    )(x2d, wu, bu, wd, bd, ln_g, ln_b)


def _permute_in_columns(a):
    q0 = OFF_SWA
    order = [h for j in range(SWA_GROUP) for h in (j, j + SWA_GROUP)]
    swa_q = [a[..., q0 + h * HEAD_DIM:q0 + (h + 1) * HEAD_DIM] for h in order]
    return jnp.concatenate([a[..., OFF_GATE:], a[..., :OFF_SWA]] + swa_q + [a[..., q0 + SWA_WIDTH:OFF_GATE]],
                           axis=-1)


def _values_columns(a):
    v_swa = OFF_SWA + SWA_WIDTH + SWA_KV_WIDTH
    return jnp.concatenate([a[..., 2 * NA_WIDTH:3 * NA_WIDTH], a[..., v_swa:v_swa + SWA_KV_WIDTH]], axis=-1)


def _prepare_layer(l, p):
    row = lambda a: a.astype(F32).reshape(1, -1)
    return dict(
        w_in=_permute_in_columns(p["w_in"][l]).astype(BF16),
        b_in=row(_permute_in_columns(p["b_in"][l])),
        conv_w=p["hy_conv_w"][l], conv_b=p["hy_conv_b"][l], hy_bias=p["hy_bias"][l],
        filt=(p["hy_filt_w1"][l], p["hy_filt_b1"][l], p["hy_filt_w2"][l], p["hy_filt_b2"][l],
              p["hy_filt_freq"][l], p["hy_filt_w3"][l]),
        w_vt=_values_columns(p["w_in"][l]).T.astype(BF16),
        b_vt=_values_columns(p["b_in"][l]).astype(F32).reshape(VT_WIDTH, 1),
        na_bias=_na_bias_table(p["na_rpb"][l]),
        sink=p["swa_sink"][l],
        wa=p["w_branch_a"][l].astype(BF16), wb=p["w_branch_b"][l].astype(BF16),
        wc=p["w_branch_c"][l].astype(BF16), wo=p["w_out"][l].astype(BF16),
        ln1_g=row(p["ln1_g"][l]), ln1_b=row(p["ln1_b"][l]),
        wu=p["w_up"][l].astype(BF16), bu=row(p["b_up"][l]),
        wd=p["w_down"][l].astype(BF16), bd=row(p["b_down"][l]),
        ln2_g=row(p["ln2_g"][l]), ln2_b=row(p["ln2_b"][l]),
    )


def _encoder_block(x, lp, swa_table):
    b_sz, seq_len, _ = x.shape
    m = b_sz * seq_len
    x2d = x.reshape(m, D_MODEL)
    h2d, vt = _inproj(x2d, lp["w_in"], lp["b_in"], lp["w_vt"], lp["b_vt"])
    h3d = h2d.reshape(b_sz, seq_len, D_IN)
    a = _na_attention(h3d, vt, lp["na_bias"])
    kf = _hyena_spectra(seq_len, *lp["filt"])
    hb = _hyena(h3d, lp["conv_w"], lp["conv_b"], kf, lp["hy_bias"])
    c = _swa_attention(h3d, vt, lp["sink"], swa_table)
    x1 = _merge(h2d, a.reshape(m, NA_WIDTH), hb.reshape(m, HY_WIDTH), c.reshape(m, SWA_WIDTH), x2d,
                lp["wa"], lp["wb"], lp["wc"], lp["wo"], lp["ln1_g"], lp["ln1_b"])
    x2 = _mlp(x1, lp["wu"], lp["bu"], lp["wd"], lp["bd"], lp["ln2_g"], lp["ln2_b"])
    return x2.reshape(b_sz, seq_len, D_MODEL)


def kernel(x_prompt, x_sample, w_in, b_in, hy_conv_w, hy_conv_b, hy_filt_w1, hy_filt_b1, hy_filt_w2,
           hy_filt_b2, hy_filt_freq, hy_filt_w3, hy_bias, na_rpb, swa_sink, w_branch_a, w_branch_b,
           w_branch_c, w_out, ln1_g, ln1_b, w_up, b_up, w_down, b_down, ln2_g, ln2_b):
    params = dict(w_in=w_in, b_in=b_in, hy_conv_w=hy_conv_w, hy_conv_b=hy_conv_b, hy_filt_w1=hy_filt_w1,
                  hy_filt_b1=hy_filt_b1, hy_filt_w2=hy_filt_w2, hy_filt_b2=hy_filt_b2,
                  hy_filt_freq=hy_filt_freq, hy_filt_w3=hy_filt_w3, hy_bias=hy_bias, na_rpb=na_rpb,
                  swa_sink=swa_sink, w_branch_a=w_branch_a, w_branch_b=w_branch_b, w_branch_c=w_branch_c,
                  w_out=w_out, ln1_g=ln1_g, ln1_b=ln1_b, w_up=w_up, b_up=b_up, w_down=w_down,
                  b_down=b_down, ln2_g=ln2_g, ln2_b=ln2_b)
    swa_table = _swa_table()
    y_prompt = x_prompt
    y_sample = x_sample
    for l in range(DEPTH):
        lp = _prepare_layer(l, params)
        y_prompt = _encoder_block(y_prompt, lp, swa_table)
        y_sample = _encoder_block(y_sample, lp, swa_table)
    return (y_prompt, y_sample)
```

```python
import functools
import math

import numpy as np
import jax
import jax.numpy as jnp
from jax import lax
from jax.experimental import pallas as pl
from jax.experimental.pallas import tpu as pltpu

F32 = jnp.float32
BF16 = jnp.bfloat16

D_MODEL = 1024
DEPTH = 2
HEAD_DIM = 64
GRID_W = 64
NA_HEADS = 8
NA_WIN_ROWS = 8
NA_WIN_COLS = 16
NA_WIDTH = NA_HEADS * HEAD_DIM
HY_WIDTH = D_MODEL // 2
HY_ORDER = 2
HY_SHORT_CONV = 3
HY_POS_BANDS = 16
HY_EMB_DIM = 1 + 2 * HY_POS_BANDS
HY_EMB_PAD = 128
HY_FILTER_HIDDEN = 64
HY_FAST_DECAY_PCT = 0.3
HY_SLOW_DECAY_PCT = 1.5
HY_DECAY_TARGET = 1e-2
SWA_HEADS = 8
SWA_KV_HEADS = 2
SWA_GROUP = SWA_HEADS // SWA_KV_HEADS
SWA_WIDTH = SWA_HEADS * HEAD_DIM
SWA_KV_WIDTH = SWA_KV_HEADS * HEAD_DIM
SWA_WINDOW = 128
SWA_BLOCK = 128
N_BRANCH = 3
D_FF = 4 * D_MODEL
OFF_HY = 3 * NA_WIDTH
OFF_SWA = OFF_HY + 3 * HY_WIDTH
OFF_GATE = OFF_SWA + SWA_WIDTH + 2 * SWA_KV_WIDTH
D_IN = OFF_GATE + N_BRANCH * D_MODEL
DEEPNORM_ALPHA = (2 * DEPTH) ** 0.25
LN_EPS = 1e-5
NEG_BIG = -1e30
LOG2E = 1.4426950408889634

P_GATE = 0
P_NA = N_BRANCH * D_MODEL
P_HY = P_NA + 3 * NA_WIDTH
P_SWA = P_HY + 3 * HY_WIDTH
VT_WIDTH = NA_WIDTH + SWA_KV_WIDTH

LANE = 128
VMEM_LIMIT_BYTES = 56 * 1024 * 1024

NA_PAIR_ROWS = NA_WIN_ROWS + 2
ATTN_BLOCKS_PER_STEP = 2
INPROJ_TM = 2048
INPROJ_TN = 768
MERGE_TM = 512
MLP_TM = 512
MLP_FF_CHUNK = 1024
CONV_TL = 256
FILT_TL = 512
FFT_LANE_TILE = 2048
HY_TW_SPLIT = 8


def _cparams(*sem):
    return pltpu.CompilerParams(dimension_semantics=sem, vmem_limit_bytes=VMEM_LIMIT_BYTES)


def _resident(shape):
    nd = len(shape)
    return pl.BlockSpec(shape, lambda *_: (0,) * nd, pipeline_mode=pl.Buffered(1))


def _inproj_kernel(x_ref, w_ref, b_ref, wvt_ref, bvt_ref, o_ref, vt_ref, xb_ref, *, n_gate_tiles):
    j = pl.program_id(1)

    @pl.when(j == 0)
    def _():
        xb_ref[...] = x_ref[...].astype(BF16)
        vt = lax.dot_general(wvt_ref[...], xb_ref[...], (((1,), (1,)), ((), ())), preferred_element_type=F32)
        vt_ref[...] = (vt + bvt_ref[...]).astype(vt_ref.dtype)

    def tile(apply_sigmoid):
        acc = jnp.dot(xb_ref[...], w_ref[...], preferred_element_type=F32) + b_ref[...]
        if apply_sigmoid:
            acc = 1.0 / (1.0 + jnp.exp(-acc))
        o_ref[...] = acc.astype(o_ref.dtype)

    pl.when(j < n_gate_tiles)(lambda: tile(True))
    pl.when(j >= n_gate_tiles)(lambda: tile(False))


def _inproj(x2d, w_bf, b_row, wvt_bf, bvt_col):
    m = x2d.shape[0]
    tm = min(INPROJ_TM, m)
    tn = INPROJ_TN
    assert m % tm == 0 and D_IN % tn == 0 and P_NA % tn == 0
    return pl.pallas_call(
        functools.partial(_inproj_kernel, n_gate_tiles=P_NA // tn),
        grid=(m // tm, D_IN // tn),
        in_specs=[
            pl.BlockSpec((tm, D_MODEL), lambda i, j: (i, 0)),
            pl.BlockSpec((D_MODEL, tn), lambda i, j: (0, j)),
            pl.BlockSpec((1, tn), lambda i, j: (0, j)),
            _resident((VT_WIDTH, D_MODEL)),
            _resident((VT_WIDTH, 1)),
        ],
        out_specs=[pl.BlockSpec((tm, tn), lambda i, j: (i, j)),
                   pl.BlockSpec((VT_WIDTH, tm), lambda i, j: (0, i))],
        out_shape=[jax.ShapeDtypeStruct((m, D_IN), BF16), jax.ShapeDtypeStruct((VT_WIDTH, m), BF16)],
        scratch_shapes=[pltpu.VMEM((tm, D_MODEL), BF16)],
        compiler_params=_cparams("parallel", "arbitrary"),
        name="inproj",
    )(x2d, w_bf, b_row, wvt_bf, bvt_col)


def _na_bias_table(rpb):
    kr = NA_WIN_ROWS
    pad = GRID_W - NA_WIN_COLS
    p = jnp.pad(rpb.astype(F32) * LOG2E, ((0, 0), (0, 0), (pad, pad)))
    cols = jnp.stack([p[:, :, GRID_W - 1 - w:2 * GRID_W - 1 - w] for w in range(GRID_W)], axis=2)
    t = jnp.stack([cols[:, kr - 1 - d:2 * kr - 1 - d] for d in range(kr)], axis=0)
    w = np.arange(GRID_W)[:, None]
    kc = np.arange(GRID_W)[None, :]
    col_start = np.clip(w - NA_WIN_COLS // 2, 0, GRID_W - NA_WIN_COLS)
    valid = (kc >= col_start) & (kc < col_start + NA_WIN_COLS)
    t = jnp.where(valid[None, None, None], t, NEG_BIG)
    t = jnp.transpose(t, (0, 1, 2, 4, 3))
    return t.reshape(kr, NA_HEADS, kr * GRID_W, GRID_W)


def _na_pair_geometry(p, rows):
    base = min(max(2 * p - NA_WIN_ROWS // 2, 0), rows - NA_PAIR_ROWS)
    geo = []
    for j in range(2):
        r = 2 * p + j
        rs = min(max(r - NA_WIN_ROWS // 2, 0), rows - NA_WIN_ROWS)
        assert 0 <= rs - base <= NA_PAIR_ROWS - NA_WIN_ROWS
        geo.append((rs - base, r - rs))
    return base, geo


def _na_pair_table(table, rows):
    n_pairs = rows // 2
    variants = []
    for p in (0, 1, 2, n_pairs - 2, n_pairs - 1):
        _, geo = _na_pair_geometry(p, rows)
        cols = []
        for off, delta in geo:
            after = NA_PAIR_ROWS - NA_WIN_ROWS - off
            cols.append(jnp.pad(table[delta], ((0, 0), (off * GRID_W, after * GRID_W), (0, 0)),
                                constant_values=NEG_BIG))
        variants.append(jnp.concatenate(cols, axis=-1))
    return jnp.stack(variants, axis=0)


def _softmax_keys(s, sink=None):
    m = jnp.max(s, axis=0, keepdims=True)
    if sink is not None:
        m = jnp.maximum(m, sink)
    p = jnp.exp2(s - m)
    l = jnp.sum(p, axis=0, keepdims=True)
    if sink is not None:
        l = l + jnp.exp2(sink - m)
    return p.astype(BF16), l


def _half_lanes(x, upper):
    lane = lax.broadcasted_iota(jnp.int32, (1, LANE), 1)
    mine = (lane >= HEAD_DIM) if upper else (lane < HEAD_DIM)
    return jnp.where(mine, x, jnp.zeros((), x.dtype))


def _na_kernel(q_ref, k_ref, vt_ref, *rest, rows, pairs_per_step):
    bias_refs, o_ref = rest[:-1], rest[-1]
    span = NA_PAIR_ROWS * GRID_W
    work = []
    for u in range(pairs_per_step):
        pair = pl.program_id(1) * pairs_per_step + u
        base = jnp.clip(2 * pair - NA_WIN_ROWS // 2, 0, rows - NA_PAIR_ROWS)
        start = pl.multiple_of(base * GRID_W, 2 * GRID_W)
        q = q_ref[0, u * LANE:(u + 1) * LANE, :]
        k = k_ref[0, pl.ds(start, span), :]
        vt = vt_ref[:, pl.ds(start, span)]
        for h in range(NA_HEADS):
            grp = slice((h // 2) * LANE, (h // 2 + 1) * LANE)
            s = lax.dot_general(k[:, grp], _half_lanes(q[:, grp], h % 2 == 1), (((1,), (1,)), ((), ())),
                                preferred_element_type=F32)
            work.append((vt[h * HEAD_DIM:(h + 1) * HEAD_DIM, :],) + _softmax_keys(s + bias_refs[u][0, h]))
    outs = [jnp.dot(v, p, preferred_element_type=F32) / l for v, p, l in work]
    for u in range(pairs_per_step):
        o_t = jnp.concatenate(outs[u * NA_HEADS:(u + 1) * NA_HEADS], axis=0)
        o_ref[0, u * LANE:(u + 1) * LANE, :] = o_t.T.astype(o_ref.dtype)


def _na_attention(h3d, vt, bias_table):
    b_sz, seq_len, _ = h3d.shape
    rows = seq_len // GRID_W
    n_pairs = rows // 2
    assert rows % 2 == 0 and n_pairs >= 5
    cb = P_NA // NA_WIDTH
    span = NA_PAIR_ROWS * GRID_W
    pair_table = _na_pair_table(bias_table, rows)

    def variant(p):
        return jnp.where(p < 2, p, jnp.where(p > n_pairs - 3, p - (n_pairs - 5), 2))

    pps = ATTN_BLOCKS_PER_STEP
    assert n_pairs % pps == 0 and 2 * GRID_W == LANE
    bias_specs = [pl.BlockSpec((1, NA_HEADS, span, LANE), lambda b, p, u=u: (variant(p * pps + u), 0, 0, 0))
                  for u in range(pps)]
    return pl.pallas_call(
        functools.partial(_na_kernel, rows=rows, pairs_per_step=pps),
        grid=(b_sz, n_pairs // pps),
        in_specs=[
            pl.BlockSpec((1, pps * LANE, NA_WIDTH), lambda b, p: (b, p, cb)),
            pl.BlockSpec((1, seq_len, NA_WIDTH), lambda b, p: (b, 0, cb + 1), pipeline_mode=pl.Buffered(1)),
            pl.BlockSpec((NA_WIDTH, seq_len), lambda b, p: (0, b), pipeline_mode=pl.Buffered(1)),
        ] + bias_specs,
        out_specs=pl.BlockSpec((1, pps * LANE, NA_WIDTH), lambda b, p: (b, p, 0)),
        out_shape=jax.ShapeDtypeStruct((b_sz, seq_len, NA_WIDTH), BF16),
        compiler_params=_cparams("parallel", "arbitrary"),
        name="na_attention",
    )(h3d, h3d, vt, *([pair_table] * pps))


def _swa_table():
    span = SWA_BLOCK + 2 * SWA_WINDOW
    slopes = 2.0 ** (-8.0 * (np.arange(SWA_HEADS, dtype=np.float64) + 1.0) / SWA_HEADS)
    kk = np.arange(span)[:, None]
    t = np.arange(SWA_BLOCK)[None, :]
    variants = []
    for off in (0, SWA_BLOCK, 2 * SWA_BLOCK):
        rel = np.abs(kk - off - t)
        a = -slopes[:, None, None] * rel[None].astype(np.float64) * LOG2E
        variants.append(np.where((rel <= SWA_WINDOW)[None], a, NEG_BIG))
    return jnp.asarray(np.stack(variants), F32)


def _swa_kernel(sink_ref, q_ref, k_ref, vt_ref, *rest, seq_len, blocks_per_step):
    tab_refs, o_ref = rest[:-1], rest[-1]
    span = SWA_BLOCK + 2 * SWA_WINDOW
    work = []
    for u in range(blocks_per_step):
        blk = pl.program_id(1) * blocks_per_step + u
        start = pl.multiple_of(jnp.clip((blk - 1) * SWA_BLOCK, 0, seq_len - span), SWA_BLOCK)
        q = q_ref[0, u * SWA_BLOCK:(u + 1) * SWA_BLOCK, :]
        k = k_ref[0, pl.ds(start, span), :]
        vt = vt_ref[:, pl.ds(start, span)]
        for h in range(SWA_HEADS):
            g = h // SWA_GROUP
            grp = slice((h % SWA_GROUP) * LANE, (h % SWA_GROUP + 1) * LANE)
            s = lax.dot_general(k, _half_lanes(q[:, grp], g == 1), (((1,), (1,)), ((), ())),
                                preferred_element_type=F32)
            work.append((vt[g * HEAD_DIM:(g + 1) * HEAD_DIM, :],)
                        + _softmax_keys(s + tab_refs[u][0, h], sink_ref[h] * LOG2E))
    outs = [jnp.dot(v, p, preferred_element_type=F32) / l for v, p, l in work]
    for u in range(blocks_per_step):
        o_t = jnp.concatenate(outs[u * SWA_HEADS:(u + 1) * SWA_HEADS], axis=0)
        o_ref[0, u * SWA_BLOCK:(u + 1) * SWA_BLOCK, :] = o_t.T.astype(o_ref.dtype)


def _swa_attention(h3d, vt, sink, table):
    b_sz, seq_len, _ = h3d.shape
    nb = seq_len // SWA_BLOCK
    span = SWA_BLOCK + 2 * SWA_WINDOW
    assert SWA_KV_HEADS == 2 and SWA_BLOCK == LANE and nb >= 3
    qb = P_SWA // SWA_WIDTH
    kb = (P_SWA + SWA_WIDTH) // SWA_KV_WIDTH

    def variant(i):
        return jnp.where(i == 0, 0, jnp.where(i == nb - 1, 2, 1))

    bps = ATTN_BLOCKS_PER_STEP
    assert nb % bps == 0
    tab_specs = [pl.BlockSpec((1, SWA_HEADS, span, SWA_BLOCK), lambda b, i, u=u: (variant(i * bps + u), 0, 0, 0))
                 for u in range(bps)]
    return pl.pallas_call(
        functools.partial(_swa_kernel, seq_len=seq_len, blocks_per_step=bps),
        grid=(b_sz, nb // bps),
        in_specs=[
            pl.BlockSpec(memory_space=pltpu.SMEM),
            pl.BlockSpec((1, bps * SWA_BLOCK, SWA_WIDTH), lambda b, i: (b, i, qb)),
            pl.BlockSpec((1, seq_len, SWA_KV_WIDTH), lambda b, i: (b, 0, kb)),
            pl.BlockSpec((SWA_KV_WIDTH, seq_len), lambda b, i: (NA_WIDTH // SWA_KV_WIDTH, b)),
        ] + tab_specs,
        out_specs=pl.BlockSpec((1, bps * SWA_BLOCK, SWA_WIDTH), lambda b, i: (b, i, 0)),
        out_shape=jax.ShapeDtypeStruct((b_sz, seq_len, SWA_WIDTH), BF16),
        compiler_params=_cparams("parallel", "arbitrary"),
        name="swa_attention",
    )(sink.astype(F32), h3d, h3d, vt, *([table] * bps))


def _conv_kernel(x_ref, prev_ref, next_ref, w_ref, b_ref, v_ref, x1_ref, x2_ref, *, n_tiles):
    i = pl.program_id(1)
    tl = x_ref.shape[1]
    halo = prev_ref.shape[1]
    row = lax.broadcasted_iota(jnp.int32, (tl, 1), 0)
    outs = (v_ref, x1_ref, x2_ref)
    for c in range(3):
        sl = slice(c * HY_WIDTH, (c + 1) * HY_WIDTH)
        x = x_ref[0, :, sl].astype(F32)
        prev_row = jnp.where(i > 0, prev_ref[0, halo - 1:halo, sl].astype(F32), 0.0)
        next_row = jnp.where(i < n_tiles - 1, next_ref[0, 0:1, sl].astype(F32), 0.0)
        xm = jnp.where(row == 0, prev_row, pltpu.roll(x, 1, axis=0))
        xp = jnp.where(row == tl - 1, next_row, pltpu.roll(x, tl - 1, axis=0))
        u = xm * w_ref[0:1, sl] + x * w_ref[1:2, sl] + xp * w_ref[2:3, sl] + b_ref[:, sl]
        outs[c][0] = u.astype(BF16)


def _short_conv(h3d, conv_w, conv_b):
    b_sz, seq_len, _ = h3d.shape
    tl = CONV_TL
    halo = 16
    n_tiles = seq_len // tl
    width = 3 * HY_WIDTH
    cb = P_HY // width
    assert P_HY % width == 0 and seq_len % tl == 0
    per = tl // halo
    out = jax.ShapeDtypeStruct((b_sz, seq_len, HY_WIDTH), BF16)
    ospec = pl.BlockSpec((1, tl, HY_WIDTH), lambda b, i: (b, i, 0))
    return pl.pallas_call(
        functools.partial(_conv_kernel, n_tiles=n_tiles),
        grid=(b_sz, n_tiles),
        in_specs=[
            pl.BlockSpec((1, tl, width), lambda b, i: (b, i, cb)),
            pl.BlockSpec((1, halo, width), lambda b, i: (b, jnp.maximum(i * per - 1, 0), cb)),
            pl.BlockSpec((1, halo, width), lambda b, i: (b, jnp.minimum((i + 1) * per, seq_len // halo - 1), cb)),
            _resident((HY_SHORT_CONV, width)),
            _resident((1, width)),
        ],
        out_specs=[ospec, ospec, ospec],
        out_shape=[out, out, out],
        compiler_params=_cparams("parallel", "arbitrary"),
        name="hyena_short_conv",
    )(h3d, h3d, h3d, conv_w.astype(F32), conv_b.astype(F32).reshape(1, width))


def _filter_embedding(seq_len):
    t = np.linspace(0.0, 1.0, seq_len, dtype=np.float32).astype(np.float64)[:, None]
    w = (2.0 * math.pi * np.arange(seq_len, dtype=np.float32) / seq_len).astype(np.float32)
    bands = np.linspace(1e-4, HY_POS_BANDS - 1, HY_POS_BANDS, dtype=np.float32)
    ang = (w[:, None] * bands[None, :]).astype(np.float32).astype(np.float64)
    z = np.concatenate([t, np.cos(ang), -np.sin(ang)], axis=-1)
    zp = np.zeros((seq_len, HY_EMB_PAD), np.float32)
    zp[:, :HY_EMB_DIM] = z
    return zp


def _filter_deltas():
    min_decay = math.log(HY_DECAY_TARGET) / HY_SLOW_DECAY_PCT
    max_decay = math.log(HY_DECAY_TARGET) / HY_FAST_DECAY_PCT
    return np.abs(np.linspace(min_decay, max_decay, HY_WIDTH, dtype=np.float32))[None, :]


def _filter_kernel(z_ref, w1_ref, b1_ref, w2_ref, b2_ref, fr_ref, w3_ref, dl_ref, o_ref):
    hi = lax.Precision.HIGHEST
    tl = z_ref.shape[0]
    z = z_ref[...]
    fr = fr_ref[...]
    h = jnp.sin(fr * (jnp.dot(z, w1_ref[...], precision=hi, preferred_element_type=F32) + b1_ref[...]))
    h = jnp.sin(fr * (jnp.dot(h, w2_ref[...], precision=hi, preferred_element_type=F32) + b2_ref[...]))
    t = z[:, 0:1]
    window = jnp.exp(-t * dl_ref[...])
    row = pl.program_id(0) * tl + lax.broadcasted_iota(jnp.int32, (tl, 1), 0)
    for s in range(2 * HY_ORDER):
        sl = slice(s * HY_WIDTH, (s + 1) * HY_WIDTH)
        f = jnp.dot(h, w3_ref[:, sl], precision=hi, preferred_element_type=F32) * window
        if s % 2 == 1:
            f = jnp.where(row > 0, f, 0.0)
        o_ref[s] = f.astype(o_ref.dtype)


def _hyena_filters(seq_len, w1, b1, w2, b2, freq, w3):
    tl = min(FILT_TL, seq_len)
    z = jnp.asarray(_filter_embedding(seq_len))
    w1p = jnp.zeros((HY_EMB_PAD, HY_FILTER_HIDDEN), F32).at[:HY_EMB_DIM].set(w1.astype(F32))
    hid = HY_FILTER_HIDDEN
    n_f = 2 * HY_ORDER
    return pl.pallas_call(
        _filter_kernel,
        grid=(seq_len // tl,),
        in_specs=[
            pl.BlockSpec((tl, HY_EMB_PAD), lambda i: (i, 0)),
            _resident((HY_EMB_PAD, hid)), _resident((1, hid)),
            _resident((hid, hid)), _resident((1, hid)), _resident((1, hid)),
            _resident((hid, n_f * HY_WIDTH)), _resident((1, HY_WIDTH)),
        ],
        out_specs=pl.BlockSpec((n_f, tl, HY_WIDTH), lambda i: (0, i, 0)),
        out_shape=jax.ShapeDtypeStruct((n_f, seq_len, HY_WIDTH), BF16),
        compiler_params=_cparams("parallel"),
        name="hyena_filters",
    )(z, w1p, b1.astype(F32).reshape(1, hid), w2.astype(F32), b2.astype(F32).reshape(1, hid),
      freq.astype(F32).reshape(1, hid), w3.astype(F32), jnp.asarray(_filter_deltas()))


def _fft_dims(seq_len):
    n = 2 * seq_len
    n1 = 1 << ((n.bit_length() - 1) // 2)
    n2 = n // n1
    assert n1 * n2 == n and n1 == n2, "sequence length must give a square transform"
    return n1, n2


def _stack_complex(fr, fi):
    return np.block([[fr, -fi], [fi, fr]])


@functools.lru_cache(maxsize=None)
def _dft_constants(seq_len):
    n1, n2 = _fft_dims(seq_len)
    n = n1 * n2
    k1 = np.arange(n1)[:, None].astype(np.float64)
    t1 = np.arange(n1 // 2)[None, :].astype(np.float64)
    a1 = -2.0 * math.pi * k1 * t1 / n1
    f1r, f1i = np.cos(a1), np.sin(a1)
    w1_complex = _stack_complex(f1r, f1i)
    w1_real = np.concatenate([f1r, f1i], axis=0)
    k2 = np.arange(n2)[:, None].astype(np.float64)
    t2 = np.arange(n2)[None, :].astype(np.float64)
    a2 = -2.0 * math.pi * k2 * t2 / n2
    f2r, f2i = np.cos(a2), np.sin(a2)
    w2 = _stack_complex(f2r, f2i)
    w2_inv = _stack_complex(f2r, -f2i)
    at = -2.0 * math.pi * k1 * t2 / n
    tw = np.stack([np.cos(at), np.sin(at)], axis=0)[..., None]
    tw = np.broadcast_to(tw, (2, n1, n2, LANE)).astype(np.float32)
    g1r, g1i = f1r.T / n, -f1i.T / n
    w3 = _stack_complex(g1r, g1i)
    kb = np.arange(HY_TW_SPLIT)[:, None].astype(np.float64)
    ab = -2.0 * math.pi * kb * t2 / n
    tbr, tbi = np.cos(ab), np.sin(ab)
    w2f = np.stack([_stack_complex(f2r * tbr[b] - f2i * tbi[b], f2r * tbi[b] + f2i * tbr[b])
                    for b in range(HY_TW_SPLIT)])
    w2i = np.stack([_stack_complex(tbr[b][:, None] * f2r - tbi[b][:, None] * f2i,
                                   -(tbr[b][:, None] * f2i + tbi[b][:, None] * f2r))
                    for b in range(HY_TW_SPLIT)])
    ka = (np.arange(n1 // HY_TW_SPLIT) * HY_TW_SPLIT)[:, None].astype(np.float64)
    aa = -2.0 * math.pi * ka * t2 / n
    ta = np.stack([np.cos(aa), np.sin(aa)], axis=0)[..., None]
    ta = np.broadcast_to(ta, (2, n1 // HY_TW_SPLIT, n2, LANE)).astype(np.float32)
    as_bf = lambda a: jnp.asarray(a, F32).astype(BF16)
    return dict(w1_complex=as_bf(w1_complex), w1_real=as_bf(w1_real), w2=as_bf(w2), w2f=as_bf(w2f),
                w2i=as_bf(w2i), tw=jnp.asarray(tw), ta=jnp.asarray(ta), w3=as_bf(w3))


def _s1_kernel(w_ref, x_ref, o_ref):
    tl = x_ref.shape[-1]
    x = x_ref[...].reshape(-1, tl).astype(BF16)
    o_ref[0] = jnp.dot(w_ref[...], x, preferred_element_type=F32).astype(o_ref.dtype)


def _fft_stage1(u, w1, *, pair):
    s, seq_len, c = u.shape
    n1, n2 = _fft_dims(seq_len)
    per = 2 if pair else 1
    lanes = n2 * c
    tl = min(FFT_LANE_TILE, lanes)
    uv = u.reshape(s, n1 // 2, lanes)
    out = pl.pallas_call(
        _s1_kernel,
        grid=(s // per, lanes // tl),
        in_specs=[
            _resident(w1.shape),
            pl.BlockSpec((per, n1 // 2, tl), lambda p, j: (p, 0, j)),
        ],
        out_specs=pl.BlockSpec((1, 2 * n1, tl), lambda p, j: (p, 0, j)),
        out_shape=jax.ShapeDtypeStruct((s // per, 2 * n1, lanes), BF16),
        compiler_params=_cparams("parallel", "arbitrary"),
        name="hyena_dft_stage1",
    )(w1, uv)
    return out.reshape(s // per, 2, n1, n2, c)


def _lane_tile(x, c):
    return jnp.concatenate([x] * (c // LANE), axis=-1)


def _twiddled_stack(a_ref, twr, twi):
    ar = a_ref[0, 0, 0].astype(F32)
    ai = a_ref[0, 1, 0].astype(F32)
    return jnp.concatenate([ar * twr - ai * twi, ar * twi + ai * twr], axis=0).astype(BF16)


def _pack_pair(a, b):
    return pltpu.pack_elementwise([a, b], packed_dtype=BF16)


def _unpack_pair(w):
    return (pltpu.unpack_elementwise(w, index=0, packed_dtype=BF16, unpacked_dtype=F32),
            pltpu.unpack_elementwise(w, index=1, packed_dtype=BF16, unpacked_dtype=F32))


def _conv_pitch(n2):
    return n2 + 8


def _hyena_conv_kernel(z_ref, g_ref, kf_ref, w1_ref, w2f_ref, w2i_ref, w3_ref, ta_ref, bias_ref, o_ref,
                       x_scr, a_scr, *, n1, n2):
    pitch = _conv_pitch(n2)
    half = n1 // 2
    groups = n1 // HY_TW_SPLIT

    def pack_in(t1, carry):
        rows = pl.ds(pl.multiple_of(t1 * n2, n2), n2)
        x_scr[pl.ds(pl.multiple_of(t1 * pitch, 8), n2), :] = _pack_pair(
            z_ref[0, rows, :].astype(F32), z_ref[1, rows, :].astype(F32))
        return carry

    lax.fori_loop(0, half, pack_in, 0)

    def stage1(j, carry):
        cols = []
        for u in range(2):
            xr, xi = _unpack_pair(x_scr[pl.ds(2 * j + u, half, stride=pitch), :])
            cols.append(jnp.concatenate([xr, xi], axis=0).astype(BF16))
        out = jnp.dot(w1_ref[...], jnp.concatenate(cols, axis=1), preferred_element_type=F32)
        for u in range(2):
            o = out[:, u * LANE:(u + 1) * LANE]
            a_scr[pl.ds(2 * j + u, n1, stride=pitch), :] = _pack_pair(o[:n1], o[n1:])
        return carry

    lax.fori_loop(0, n2 // 2, stage1, 0, unroll=4)

    for kb in range(HY_TW_SPLIT):
        def stage2(jp, carry, kb=kb):
            rows, tws, stacks = [], [], []
            for u in range(2):
                ka = 2 * jp + u
                k1 = ka * HY_TW_SPLIT + kb
                rows.append(pl.ds(pl.multiple_of(k1 * pitch, 8), n2))
                ar, ai = _unpack_pair(a_scr[rows[u], :])
                twr, twi = ta_ref[0, ka], ta_ref[1, ka]
                tws.append((twr, twi))
                stacks.append(jnp.concatenate([ar * twr - ai * twi, ar * twi + ai * twr], axis=0).astype(BF16))
            spec = jnp.dot(w2f_ref[kb], jnp.concatenate(stacks, axis=1), preferred_element_type=F32)
            prods = []
            for u in range(2):
                k1 = (2 * jp + u) * HY_TW_SPLIT + kb
                sr, si = spec[:n2, u * LANE:(u + 1) * LANE], spec[n2:, u * LANE:(u + 1) * LANE]
                kr = kf_ref[0, 0, k1].astype(F32)
                ki = kf_ref[0, 1, k1].astype(F32)
                prods.append(jnp.concatenate([sr * kr - si * ki, sr * ki + si * kr], axis=0).astype(BF16))
            back = jnp.dot(w2i_ref[kb], jnp.concatenate(prods, axis=1), preferred_element_type=F32)
            for u in range(2):
                br, bi = back[:n2, u * LANE:(u + 1) * LANE], back[n2:, u * LANE:(u + 1) * LANE]
                twr, twi = tws[u]
                a_scr[rows[u], :] = _pack_pair(br * twr + bi * twi, bi * twr - br * twi)
            return carry

        lax.fori_loop(0, groups // 2, stage2, 0, unroll=4)

    def stage3(j, carry):
        cols = []
        for u in range(2):
            br, bi = _unpack_pair(a_scr[pl.ds(2 * j + u, n1, stride=pitch), :])
            cols.append(jnp.concatenate([br, bi], axis=0).astype(BF16))
        y = jnp.dot(w3_ref[...], jnp.concatenate(cols, axis=1), preferred_element_type=F32)
        for u in range(2):
            sl = pl.ds(2 * j + u, half, stride=pitch)
            za, zb = _unpack_pair(x_scr[sl, :])
            yu = y[:, u * LANE:(u + 1) * LANE]
            x_scr[sl, :] = _pack_pair(yu[:half] + bias_ref[...] * za, yu[half:] + bias_ref[...] * zb)
        return carry

    lax.fori_loop(0, n2 // 2, stage3, 0, unroll=4)

    def gate_out(t1, carry):
        rows = pl.ds(pl.multiple_of(t1 * n2, n2), n2)
        ta, tb = _unpack_pair(x_scr[pl.ds(pl.multiple_of(t1 * pitch, 8), n2), :])
        o_ref[0, rows, :] = (g_ref[0, rows, :].astype(F32) * ta).astype(o_ref.dtype)
        o_ref[1, rows, :] = (g_ref[1, rows, :].astype(F32) * tb).astype(o_ref.dtype)
        return carry

    lax.fori_loop(0, half, gate_out, 0)


def _hyena_conv(z, gate, kf, order, bias, consts):
    s, seq_len, c = z.shape
    n1, n2 = _fft_dims(seq_len)
    pitch = _conv_pitch(n2)
    assert s % 2 == 0 and c % LANE == 0 and (n1 // HY_TW_SPLIT) % 2 == 0
    seq_blk = pl.BlockSpec((2, seq_len, LANE), lambda cc, p: (p, 0, cc))
    return pl.pallas_call(
        functools.partial(_hyena_conv_kernel, n1=n1, n2=n2),
        grid=(c // LANE, s // 2),
        in_specs=[
            seq_blk,
            pl.BlockSpec((2, seq_len, LANE), lambda cc, p: (p, 0, cc), pipeline_mode=pl.Buffered(1)),
            pl.BlockSpec((1, 2, n1, n2, LANE), lambda cc, p: (order, 0, 0, 0, cc), pipeline_mode=pl.Buffered(1)),
            _resident(consts["w1_complex"].shape),
            _resident(consts["w2f"].shape),
            _resident(consts["w2i"].shape),
            _resident(consts["w3"].shape),
            _resident(consts["ta"].shape),
            pl.BlockSpec((1, LANE), lambda cc, p: (0, cc)),
        ],
        out_specs=seq_blk,
        out_shape=jax.ShapeDtypeStruct(z.shape, BF16),
        scratch_shapes=[pltpu.VMEM((n1 // 2 * pitch, LANE), jnp.uint32),
                        pltpu.VMEM((n1 * pitch, LANE), jnp.uint32)],
        compiler_params=_cparams("parallel", "arbitrary"),
        name="hyena_conv",
    )(z, gate, kf, consts["w1_complex"], consts["w2f"], consts["w2i"], consts["w3"], consts["ta"],
      bias.astype(F32).reshape(1, c))


def _s2f_kernel(af_ref, ab_ref, tw_ref, w2_ref, o_ref):
    n2, c = af_ref.shape[-2:]
    twr = _lane_tile(tw_ref[0, 0], c)
    twi = _lane_tile(tw_ref[1, 0], c)
    hf = jnp.dot(w2_ref[...], _twiddled_stack(af_ref, twr, twi), preferred_element_type=F32)
    hb = jnp.dot(w2_ref[...], _twiddled_stack(ab_ref, twr, twi), preferred_element_type=F32)
    o_ref[0, 0, 0] = (hf[:n2] + hb[:n2]).astype(o_ref.dtype)
    o_ref[0, 1, 0] = (hf[n2:] - hb[n2:]).astype(o_ref.dtype)


def _filter_spectrum(a_filt, consts):
    _, _, n1, n2, c = a_filt.shape
    return pl.pallas_call(
        _s2f_kernel,
        grid=(HY_ORDER, n1),
        in_specs=[
            pl.BlockSpec((1, 2, 1, n2, c), lambda o, k: (2 * o, 0, k, 0, 0)),
            pl.BlockSpec((1, 2, 1, n2, c), lambda o, k: (2 * o + 1, 0, k, 0, 0)),
            pl.BlockSpec((2, 1, n2, LANE), lambda o, k: (0, k, 0, 0)),
            _resident((2 * n2, 2 * n2)),
        ],
        out_specs=pl.BlockSpec((1, 2, 1, n2, c), lambda o, k: (o, 0, k, 0, 0)),
        out_shape=jax.ShapeDtypeStruct((HY_ORDER, 2, n1, n2, c), BF16),
        compiler_params=_cparams("parallel", "arbitrary"),
        name="hyena_filter_spectrum",
    )(a_filt, a_filt, consts["tw"], consts["w2"])


def _hyena_spectra(seq_len, w1, b1, w2, b2, freq, w3):
    consts = _dft_constants(seq_len)
    filt = _hyena_filters(seq_len, w1, b1, w2, b2, freq, w3)
    a_filt = _fft_stage1(filt, consts["w1_real"], pair=False)
    return _filter_spectrum(a_filt, consts)


def _hyena(h3d, conv_w, conv_b, kf, hy_bias):
    seq_len = h3d.shape[1]
    consts = _dft_constants(seq_len)
    v, x1, x2 = _short_conv(h3d, conv_w, conv_b)
    z = v
    for n, gate in enumerate((x1, x2)):
        z = _hyena_conv(z, gate, kf, n, hy_bias[n], consts)
    return z


def _layernorm(y, g, b):
    mu = jnp.mean(y, axis=-1, keepdims=True)
    d = y - mu
    var = jnp.mean(d * d, axis=-1, keepdims=True)
    return d * lax.rsqrt(var + LN_EPS) * g + b


def _merge_kernel(g_ref, a_ref, hb_ref, c_ref, x_ref, wa_ref, wb_ref, wc_ref, wo_ref, lg_ref, lb_ref, o_ref):
    d = D_MODEL
    merged = g_ref[:, 0:d].astype(F32) * jnp.dot(a_ref[...], wa_ref[...], preferred_element_type=F32)
    merged += g_ref[:, d:2 * d].astype(F32) * jnp.dot(hb_ref[...], wb_ref[...], preferred_element_type=F32)
    merged += g_ref[:, 2 * d:3 * d].astype(F32) * jnp.dot(c_ref[...], wc_ref[...], preferred_element_type=F32)
    mix = jnp.dot(merged.astype(BF16), wo_ref[...], preferred_element_type=F32)
    o_ref[...] = _layernorm(DEEPNORM_ALPHA * x_ref[...] + mix, lg_ref[...], lb_ref[...])


def _merge(h2d, a, hb, c, x2d, wa, wb, wc, wo, ln_g, ln_b):
    m = x2d.shape[0]
    tm = MERGE_TM
    row = lambda width: pl.BlockSpec((tm, width), lambda i: (i, 0))
    return pl.pallas_call(
        _merge_kernel,
        grid=(m // tm,),
        in_specs=[
            row(N_BRANCH * D_MODEL), row(NA_WIDTH), row(HY_WIDTH), row(SWA_WIDTH), row(D_MODEL),
            _resident((NA_WIDTH, D_MODEL)), _resident((HY_WIDTH, D_MODEL)), _resident((SWA_WIDTH, D_MODEL)),
            _resident((D_MODEL, D_MODEL)), _resident((1, D_MODEL)), _resident((1, D_MODEL)),
        ],
        out_specs=row(D_MODEL),
        out_shape=jax.ShapeDtypeStruct((m, D_MODEL), F32),
        compiler_params=_cparams("parallel"),
        name="merge_ln",
    )(h2d, a, hb, c, x2d, wa, wb, wc, wo, ln_g, ln_b)


def _mlp_kernel(x_ref, wu_ref, bu_ref, wd_ref, bd_ref, lg_ref, lb_ref, o_ref):
    x = x_ref[...]
    xb = x.astype(BF16)
    acc = DEEPNORM_ALPHA * x + bd_ref[...]
    for c in range(D_FF // MLP_FF_CHUNK):
        sl = slice(c * MLP_FF_CHUNK, (c + 1) * MLP_FF_CHUNK)
        up = jnp.dot(xb, wu_ref[:, sl], preferred_element_type=F32) + bu_ref[:, sl]
        up = jnp.square(jnp.maximum(up, 0.0))
        acc += jnp.dot(up.astype(BF16), wd_ref[sl, :], preferred_element_type=F32)
    o_ref[...] = _layernorm(acc, lg_ref[...], lb_ref[...])


def _mlp(x2d, wu, bu, wd, bd, ln_g, ln_b):
    m = x2d.shape[0]
    tm = MLP_TM
    row = pl.BlockSpec((tm, D_MODEL), lambda i: (i, 0))
    return pl.pallas_call(
        _mlp_kernel,
        grid=(m // tm,),
        in_specs=[
            row, _resident((D_MODEL, D_FF)), _resident((1, D_FF)), _resident((D_FF, D_MODEL)),
            _resident((1, D_MODEL)), _resident((1, D_MODEL)), _resident((1, D_MODEL)),
        ],
        out_specs=row,
        out_shape=jax.ShapeDtypeStruct((m, D_MODEL), F32),
        compiler_params=_cparams("parallel"),
        name="mlp_ln",
    )(x2d, wu, bu, wd, bd, ln_g, ln_b)


def _permute_in_columns(a):
    q0 = OFF_SWA
    order = [h for j in range(SWA_GROUP) for h in (j, j + SWA_GROUP)]
    swa_q = [a[..., q0 + h * HEAD_DIM:q0 + (h + 1) * HEAD_DIM] for h in order]
    return jnp.concatenate([a[..., OFF_GATE:], a[..., :OFF_SWA]] + swa_q + [a[..., q0 + SWA_WIDTH:OFF_GATE]],
                           axis=-1)


def _scale_query_columns(a):
    c = HEAD_DIM ** -0.5 * LOG2E
    return jnp.concatenate([a[..., :NA_WIDTH] * c, a[..., NA_WIDTH:OFF_SWA],
                            a[..., OFF_SWA:OFF_SWA + SWA_WIDTH] * c, a[..., OFF_SWA + SWA_WIDTH:]], axis=-1)


def _values_columns(a):
    v_swa = OFF_SWA + SWA_WIDTH + SWA_KV_WIDTH
    return jnp.concatenate([a[..., 2 * NA_WIDTH:3 * NA_WIDTH], a[..., v_swa:v_swa + SWA_KV_WIDTH]], axis=-1)


def _prepare_layer(l, p):
    row = lambda a: a.astype(F32).reshape(1, -1)
    return dict(
        w_in=_permute_in_columns(_scale_query_columns(p["w_in"][l])).astype(BF16),
        b_in=row(_permute_in_columns(_scale_query_columns(p["b_in"][l]))),
        conv_w=p["hy_conv_w"][l], conv_b=p["hy_conv_b"][l], hy_bias=p["hy_bias"][l],
        filt=(p["hy_filt_w1"][l], p["hy_filt_b1"][l], p["hy_filt_w2"][l], p["hy_filt_b2"][l],
              p["hy_filt_freq"][l], p["hy_filt_w3"][l]),
        w_vt=_values_columns(p["w_in"][l]).T.astype(BF16),
        b_vt=_values_columns(p["b_in"][l]).astype(F32).reshape(VT_WIDTH, 1),
        na_bias=_na_bias_table(p["na_rpb"][l]),
        sink=p["swa_sink"][l],
        wa=p["w_branch_a"][l].astype(BF16), wb=p["w_branch_b"][l].astype(BF16),
        wc=p["w_branch_c"][l].astype(BF16), wo=p["w_out"][l].astype(BF16),
        ln1_g=row(p["ln1_g"][l]), ln1_b=row(p["ln1_b"][l]),
        wu=p["w_up"][l].astype(BF16), bu=row(p["b_up"][l]),
        wd=p["w_down"][l].astype(BF16), bd=row(p["b_down"][l]),
        ln2_g=row(p["ln2_g"][l]), ln2_b=row(p["ln2_b"][l]),
    )


def _encoder_block(x, lp, swa_table):
    b_sz, seq_len, _ = x.shape
    m = b_sz * seq_len
    x2d = x.reshape(m, D_MODEL)
    h2d, vt = _inproj(x2d, lp["w_in"], lp["b_in"], lp["w_vt"], lp["b_vt"])
    h3d = h2d.reshape(b_sz, seq_len, D_IN)
    a = _na_attention(h3d, vt, lp["na_bias"])
    kf = _hyena_spectra(seq_len, *lp["filt"])
    hb = _hyena(h3d, lp["conv_w"], lp["conv_b"], kf, lp["hy_bias"])
    c = _swa_attention(h3d, vt, lp["sink"], swa_table)
    x1 = _merge(h2d, a.reshape(m, NA_WIDTH), hb.reshape(m, HY_WIDTH), c.reshape(m, SWA_WIDTH), x2d,
                lp["wa"], lp["wb"], lp["wc"], lp["wo"], lp["ln1_g"], lp["ln1_b"])
    x2 = _mlp(x1, lp["wu"], lp["bu"], lp["wd"], lp["bd"], lp["ln2_g"], lp["ln2_b"])
    return x2.reshape(b_sz, seq_len, D_MODEL)


def kernel(x_prompt, x_sample, w_in, b_in, hy_conv_w, hy_conv_b, hy_filt_w1, hy_filt_b1, hy_filt_w2,
           hy_filt_b2, hy_filt_freq, hy_filt_w3, hy_bias, na_rpb, swa_sink, w_branch_a, w_branch_b,
           w_branch_c, w_out, ln1_g, ln1_b, w_up, b_up, w_down, b_down, ln2_g, ln2_b):
    params = dict(w_in=w_in, b_in=b_in, hy_conv_w=hy_conv_w, hy_conv_b=hy_conv_b, hy_filt_w1=hy_filt_w1,
                  hy_filt_b1=hy_filt_b1, hy_filt_w2=hy_filt_w2, hy_filt_b2=hy_filt_b2,
                  hy_filt_freq=hy_filt_freq, hy_filt_w3=hy_filt_w3, hy_bias=hy_bias, na_rpb=na_rpb,
                  swa_sink=swa_sink, w_branch_a=w_branch_a, w_branch_b=w_branch_b, w_branch_c=w_branch_c,
                  w_out=w_out, ln1_g=ln1_g, ln1_b=ln1_b, w_up=w_up, b_up=b_up, w_down=w_down,
                  b_down=b_down, ln2_g=ln2_g, ln2_b=ln2_b)
    swa_table = _swa_table()
    y_prompt = x_prompt
    y_sample = x_sample
    for l in range(DEPTH):
        lp = _prepare_layer(l, params)
        y_prompt = _encoder_block(y_prompt, lp, swa_table)
        y_sample = _encoder_block(y_sample, lp, swa_table)
    return (y_prompt, y_sample)
```

```python
import functools
import math

import numpy as np
import jax
import jax.numpy as jnp
from jax import lax
from jax.experimental import pallas as pl
from jax.experimental.pallas import tpu as pltpu

F32 = jnp.float32
BF16 = jnp.bfloat16

D_MODEL = 1024
DEPTH = 2
HEAD_DIM = 64
GRID_W = 64
NA_HEADS = 8
NA_WIN_ROWS = 8
NA_WIN_COLS = 16
NA_WIDTH = NA_HEADS * HEAD_DIM
HY_WIDTH = D_MODEL // 2
HY_ORDER = 2
HY_SHORT_CONV = 3
HY_POS_BANDS = 16
HY_EMB_DIM = 1 + 2 * HY_POS_BANDS
HY_EMB_PAD = 128
HY_FILTER_HIDDEN = 64
HY_FAST_DECAY_PCT = 0.3
HY_SLOW_DECAY_PCT = 1.5
HY_DECAY_TARGET = 1e-2
SWA_HEADS = 8
SWA_KV_HEADS = 2
SWA_GROUP = SWA_HEADS // SWA_KV_HEADS
SWA_WIDTH = SWA_HEADS * HEAD_DIM
SWA_KV_WIDTH = SWA_KV_HEADS * HEAD_DIM
SWA_WINDOW = 128
SWA_BLOCK = 128
N_BRANCH = 3
D_FF = 4 * D_MODEL
OFF_HY = 3 * NA_WIDTH
OFF_SWA = OFF_HY + 3 * HY_WIDTH
OFF_GATE = OFF_SWA + SWA_WIDTH + 2 * SWA_KV_WIDTH
D_IN = OFF_GATE + N_BRANCH * D_MODEL
DEEPNORM_ALPHA = (2 * DEPTH) ** 0.25
LN_EPS = 1e-5
NEG_BIG = -1e30
LOG2E = 1.4426950408889634

P_GATE = 0
P_HY = N_BRANCH * D_MODEL
P_NA = P_HY + 3 * HY_WIDTH
P_SWA = P_NA + 2 * NA_WIDTH
D_TOK = P_SWA + SWA_WIDTH + SWA_KV_WIDTH
VT_WIDTH = NA_WIDTH + SWA_KV_WIDTH

LANE = 128
VMEM_LIMIT_BYTES = 56 * 1024 * 1024

NA_PAIR_ROWS = NA_WIN_ROWS + 2
ATTN_BLOCKS_PER_STEP = 2
INPROJ_TM = 512
INPROJ_TN = 768
MERGE_TM = 1024
MERGE_SUB = 256
MLP_TM = 512
MLP_FF_CHUNK = 1024
CONV_TL = 256
FILT_TL = 512
FFT_LANE_TILE = 2048
HY_TW_SPLIT = 8


def _cparams(*sem):
    return pltpu.CompilerParams(dimension_semantics=sem, vmem_limit_bytes=VMEM_LIMIT_BYTES)


def _resident(shape):
    nd = len(shape)
    return pl.BlockSpec(shape, lambda *_: (0,) * nd, pipeline_mode=pl.Buffered(1))


def _inproj_kernel(x_ref, w_ref, b_ref, wvt_ref, bvt_ref, o_ref, vt_ref):
    xb = x_ref[...].astype(BF16)
    vt = lax.dot_general(wvt_ref[...], xb, (((1,), (1,)), ((), ())), preferred_element_type=F32)
    vt_ref[...] = (vt + bvt_ref[...]).astype(vt_ref.dtype)
    for c0 in range(0, D_TOK, INPROJ_TN):
        cols = slice(c0, min(c0 + INPROJ_TN, D_TOK))
        acc = jnp.dot(xb, w_ref[:, cols], preferred_element_type=F32) + b_ref[:, cols]
        if cols.stop <= P_HY:
            acc = 1.0 / (1.0 + jnp.exp(-acc))
        o_ref[:, cols] = acc.astype(o_ref.dtype)


def _inproj(x2d, w_bf, b_row, wvt_bf, bvt_col):
    m = x2d.shape[0]
    tm = min(INPROJ_TM, m)
    assert m % tm == 0 and P_HY % INPROJ_TN == 0
    return pl.pallas_call(
        _inproj_kernel,
        grid=(m // tm,),
        in_specs=[
            pl.BlockSpec((tm, D_MODEL), lambda i: (i, 0)),
            _resident((D_MODEL, D_TOK)),
            _resident((1, D_TOK)),
            _resident((VT_WIDTH, D_MODEL)),
            _resident((VT_WIDTH, 1)),
        ],
        out_specs=[pl.BlockSpec((tm, D_TOK), lambda i: (i, 0)),
                   pl.BlockSpec((VT_WIDTH, tm), lambda i: (0, i))],
        out_shape=[jax.ShapeDtypeStruct((m, D_TOK), BF16), jax.ShapeDtypeStruct((VT_WIDTH, m), BF16)],
        compiler_params=_cparams("parallel"),
        name="inproj",
    )(x2d, w_bf, b_row, wvt_bf, bvt_col)


def _na_bias_table(rpb):
    kr = NA_WIN_ROWS
    pad = GRID_W - NA_WIN_COLS
    p = jnp.pad(rpb.astype(F32) * LOG2E, ((0, 0), (0, 0), (pad, pad)))
    cols = jnp.stack([p[:, :, GRID_W - 1 - w:2 * GRID_W - 1 - w] for w in range(GRID_W)], axis=2)
    t = jnp.stack([cols[:, kr - 1 - d:2 * kr - 1 - d] for d in range(kr)], axis=0)
    w = np.arange(GRID_W)[:, None]
    kc = np.arange(GRID_W)[None, :]
    col_start = np.clip(w - NA_WIN_COLS // 2, 0, GRID_W - NA_WIN_COLS)
    valid = (kc >= col_start) & (kc < col_start + NA_WIN_COLS)
    t = jnp.where(valid[None, None, None], t, NEG_BIG)
    t = jnp.transpose(t, (0, 1, 2, 4, 3))
    return t.reshape(kr, NA_HEADS, kr * GRID_W, GRID_W)


def _na_pair_geometry(p, rows):
    base = min(max(2 * p - NA_WIN_ROWS // 2, 0), rows - NA_PAIR_ROWS)
    geo = []
    for j in range(2):
        r = 2 * p + j
        rs = min(max(r - NA_WIN_ROWS // 2, 0), rows - NA_WIN_ROWS)
        assert 0 <= rs - base <= NA_PAIR_ROWS - NA_WIN_ROWS
        geo.append((rs - base, r - rs))
    return base, geo


def _na_pair_table(table, rows):
    n_pairs = rows // 2
    variants = []
    for p in (0, 1, 2, n_pairs - 2, n_pairs - 1):
        _, geo = _na_pair_geometry(p, rows)
        cols = []
        for off, delta in geo:
            after = NA_PAIR_ROWS - NA_WIN_ROWS - off
            cols.append(jnp.pad(table[delta], ((0, 0), (off * GRID_W, after * GRID_W), (0, 0)),
                                constant_values=NEG_BIG))
        variants.append(jnp.concatenate(cols, axis=-1))
    return jnp.stack(variants, axis=0)


def _softmax_keys(s, sink=None):
    m = jnp.max(s, axis=0, keepdims=True)
    if sink is not None:
        m = jnp.maximum(m, sink)
    p = jnp.exp2(s - m)
    l = jnp.sum(p, axis=0, keepdims=True)
    if sink is not None:
        l = l + jnp.exp2(sink - m)
    return p.astype(BF16), l


def _half_lanes(x, upper):
    lane = lax.broadcasted_iota(jnp.int32, (1, LANE), 1)
    mine = (lane >= HEAD_DIM) if upper else (lane < HEAD_DIM)
    return jnp.where(mine, x, jnp.zeros((), x.dtype))


def _na_kernel(q_ref, k_ref, vt_ref, *rest, rows, pairs_per_step):
    bias_refs, o_ref = rest[:-1], rest[-1]
    span = NA_PAIR_ROWS * GRID_W
    work = []
    for u in range(pairs_per_step):
        pair = pl.program_id(1) * pairs_per_step + u
        base = jnp.clip(2 * pair - NA_WIN_ROWS // 2, 0, rows - NA_PAIR_ROWS)
        start = pl.multiple_of(base * GRID_W, 2 * GRID_W)
        q = q_ref[0, u * LANE:(u + 1) * LANE, :]
        k = k_ref[0, pl.ds(start, span), :]
        vt = vt_ref[:, pl.ds(start, span)]
        for h in range(NA_HEADS):
            grp = slice((h // 2) * LANE, (h // 2 + 1) * LANE)
            s = lax.dot_general(k[:, grp], _half_lanes(q[:, grp], h % 2 == 1), (((1,), (1,)), ((), ())),
                                preferred_element_type=F32)
            work.append((vt[h * HEAD_DIM:(h + 1) * HEAD_DIM, :],) + _softmax_keys(s + bias_refs[u][0, h]))
    outs = [jnp.dot(v, p, preferred_element_type=F32) / l for v, p, l in work]
    for u in range(pairs_per_step):
        o_t = jnp.concatenate(outs[u * NA_HEADS:(u + 1) * NA_HEADS], axis=0)
        o_ref[0, u * LANE:(u + 1) * LANE, :] = o_t.T.astype(o_ref.dtype)


def _na_attention(h3d, vt, bias_table):
    b_sz, seq_len, _ = h3d.shape
    rows = seq_len // GRID_W
    n_pairs = rows // 2
    assert rows % 2 == 0 and n_pairs >= 5
    cb = P_NA // NA_WIDTH
    span = NA_PAIR_ROWS * GRID_W
    pair_table = _na_pair_table(bias_table, rows)

    def variant(p):
        return jnp.where(p < 2, p, jnp.where(p > n_pairs - 3, p - (n_pairs - 5), 2))

    pps = ATTN_BLOCKS_PER_STEP
    assert n_pairs % pps == 0 and 2 * GRID_W == LANE
    bias_specs = [pl.BlockSpec((1, NA_HEADS, span, LANE), lambda b, p, u=u: (variant(p * pps + u), 0, 0, 0))
                  for u in range(pps)]
    return pl.pallas_call(
        functools.partial(_na_kernel, rows=rows, pairs_per_step=pps),
        grid=(b_sz, n_pairs // pps),
        in_specs=[
            pl.BlockSpec((1, pps * LANE, NA_WIDTH), lambda b, p: (b, p, cb)),
            pl.BlockSpec((1, seq_len, NA_WIDTH), lambda b, p: (b, 0, cb + 1), pipeline_mode=pl.Buffered(1)),
            pl.BlockSpec((NA_WIDTH, seq_len), lambda b, p: (0, b), pipeline_mode=pl.Buffered(1)),
        ] + bias_specs,
        out_specs=pl.BlockSpec((1, pps * LANE, NA_WIDTH), lambda b, p: (b, p, 0)),
        out_shape=jax.ShapeDtypeStruct((b_sz, seq_len, NA_WIDTH), BF16),
        compiler_params=_cparams("parallel", "arbitrary"),
        name="na_attention",
    )(h3d, h3d, vt, *([pair_table] * pps))


def _swa_table():
    span = SWA_BLOCK + 2 * SWA_WINDOW
    slopes = 2.0 ** (-8.0 * (np.arange(SWA_HEADS, dtype=np.float64) + 1.0) / SWA_HEADS)
    kk = np.arange(span)[:, None]
    t = np.arange(SWA_BLOCK)[None, :]
    variants = []
    for off in (0, SWA_BLOCK, 2 * SWA_BLOCK):
        rel = np.abs(kk - off - t)
        a = -slopes[:, None, None] * rel[None].astype(np.float64) * LOG2E
        variants.append(np.where((rel <= SWA_WINDOW)[None], a, NEG_BIG))
    return jnp.asarray(np.stack(variants), F32)


def _swa_kernel(sink_ref, q_ref, k_ref, vt_ref, *rest, seq_len, blocks_per_step):
    tab_refs, o_ref = rest[:-1], rest[-1]
    span = SWA_BLOCK + 2 * SWA_WINDOW
    work = []
    for u in range(blocks_per_step):
        blk = pl.program_id(1) * blocks_per_step + u
        start = pl.multiple_of(jnp.clip((blk - 1) * SWA_BLOCK, 0, seq_len - span), SWA_BLOCK)
        q = q_ref[0, u * SWA_BLOCK:(u + 1) * SWA_BLOCK, :]
        k = k_ref[0, pl.ds(start, span), :]
        vt = vt_ref[:, pl.ds(start, span)]
        for h in range(SWA_HEADS):
            g = h // SWA_GROUP
            grp = slice((h % SWA_GROUP) * LANE, (h % SWA_GROUP + 1) * LANE)
            s = lax.dot_general(k, _half_lanes(q[:, grp], g == 1), (((1,), (1,)), ((), ())),
                                preferred_element_type=F32)
            work.append((vt[g * HEAD_DIM:(g + 1) * HEAD_DIM, :],)
                        + _softmax_keys(s + tab_refs[u][0, h], sink_ref[h] * LOG2E))
    outs = [jnp.dot(v, p, preferred_element_type=F32) / l for v, p, l in work]
    for u in range(blocks_per_step):
        o_t = jnp.concatenate(outs[u * SWA_HEADS:(u + 1) * SWA_HEADS], axis=0)
        o_ref[0, u * SWA_BLOCK:(u + 1) * SWA_BLOCK, :] = o_t.T.astype(o_ref.dtype)


def _swa_attention(h3d, vt, sink, table):
    b_sz, seq_len, _ = h3d.shape
    nb = seq_len // SWA_BLOCK
    span = SWA_BLOCK + 2 * SWA_WINDOW
    assert SWA_KV_HEADS == 2 and SWA_BLOCK == LANE and nb >= 3
    qb = P_SWA // SWA_WIDTH
    kb = (P_SWA + SWA_WIDTH) // SWA_KV_WIDTH

    def variant(i):
        return jnp.where(i == 0, 0, jnp.where(i == nb - 1, 2, 1))

    bps = ATTN_BLOCKS_PER_STEP
    assert nb % bps == 0
    tab_specs = [pl.BlockSpec((1, SWA_HEADS, span, SWA_BLOCK), lambda b, i, u=u: (variant(i * bps + u), 0, 0, 0))
                 for u in range(bps)]
    return pl.pallas_call(
        functools.partial(_swa_kernel, seq_len=seq_len, blocks_per_step=bps),
        grid=(b_sz, nb // bps),
        in_specs=[
            pl.BlockSpec(memory_space=pltpu.SMEM),
            pl.BlockSpec((1, bps * SWA_BLOCK, SWA_WIDTH), lambda b, i: (b, i, qb)),
            pl.BlockSpec((1, seq_len, SWA_KV_WIDTH), lambda b, i: (b, 0, kb)),
            pl.BlockSpec((SWA_KV_WIDTH, seq_len), lambda b, i: (NA_WIDTH // SWA_KV_WIDTH, b)),
        ] + tab_specs,
        out_specs=pl.BlockSpec((1, bps * SWA_BLOCK, SWA_WIDTH), lambda b, i: (b, i, 0)),
        out_shape=jax.ShapeDtypeStruct((b_sz, seq_len, SWA_WIDTH), BF16),
        compiler_params=_cparams("parallel", "arbitrary"),
        name="swa_attention",
    )(sink.astype(F32), h3d, h3d, vt, *([table] * bps))


def _conv_kernel(x_ref, prev_ref, next_ref, w_ref, b_ref, v_ref, x1_ref, x2_ref, *, n_tiles):
    i = pl.program_id(1)
    tl = x_ref.shape[1]
    halo = prev_ref.shape[1]
    row = lax.broadcasted_iota(jnp.int32, (tl, 1), 0)
    outs = (v_ref, x1_ref, x2_ref)
    for c in range(3):
        sl = slice(c * HY_WIDTH, (c + 1) * HY_WIDTH)
        x = x_ref[0, :, sl].astype(F32)
        prev_row = jnp.where(i > 0, prev_ref[0, halo - 1:halo, sl].astype(F32), 0.0)
        next_row = jnp.where(i < n_tiles - 1, next_ref[0, 0:1, sl].astype(F32), 0.0)
        xm = jnp.where(row == 0, prev_row, pltpu.roll(x, 1, axis=0))
        xp = jnp.where(row == tl - 1, next_row, pltpu.roll(x, tl - 1, axis=0))
        u = xm * w_ref[0:1, sl] + x * w_ref[1:2, sl] + xp * w_ref[2:3, sl] + b_ref[:, sl]
        outs[c][0] = u.astype(BF16)


def _short_conv(h3d, conv_w, conv_b):
    b_sz, seq_len, _ = h3d.shape
    tl = CONV_TL
    halo = 16
    n_tiles = seq_len // tl
    width = 3 * HY_WIDTH
    cb = P_HY // width
    assert P_HY % width == 0 and seq_len % tl == 0
    per = tl // halo
    out = jax.ShapeDtypeStruct((b_sz, seq_len, HY_WIDTH), BF16)
    ospec = pl.BlockSpec((1, tl, HY_WIDTH), lambda b, i: (b, i, 0))
    return pl.pallas_call(
        functools.partial(_conv_kernel, n_tiles=n_tiles),
        grid=(b_sz, n_tiles),
        in_specs=[
            pl.BlockSpec((1, tl, width), lambda b, i: (b, i, cb)),
            pl.BlockSpec((1, halo, width), lambda b, i: (b, jnp.maximum(i * per - 1, 0), cb)),
            pl.BlockSpec((1, halo, width), lambda b, i: (b, jnp.minimum((i + 1) * per, seq_len // halo - 1), cb)),
            _resident((HY_SHORT_CONV, width)),
            _resident((1, width)),
        ],
        out_specs=[ospec, ospec, ospec],
        out_shape=[out, out, out],
        compiler_params=_cparams("parallel", "arbitrary"),
        name="hyena_short_conv",
    )(h3d, h3d, h3d, conv_w.astype(F32), conv_b.astype(F32).reshape(1, width))


def _filter_embedding(seq_len):
    t = np.linspace(0.0, 1.0, seq_len, dtype=np.float32).astype(np.float64)[:, None]
    w = (2.0 * math.pi * np.arange(seq_len, dtype=np.float32) / seq_len).astype(np.float32)
    bands = np.linspace(1e-4, HY_POS_BANDS - 1, HY_POS_BANDS, dtype=np.float32)
    ang = (w[:, None] * bands[None, :]).astype(np.float32).astype(np.float64)
    z = np.concatenate([t, np.cos(ang), -np.sin(ang)], axis=-1)
    zp = np.zeros((seq_len, HY_EMB_PAD), np.float32)
    zp[:, :HY_EMB_DIM] = z
    return zp


def _filter_deltas():
    min_decay = math.log(HY_DECAY_TARGET) / HY_SLOW_DECAY_PCT
    max_decay = math.log(HY_DECAY_TARGET) / HY_FAST_DECAY_PCT
    return np.abs(np.linspace(min_decay, max_decay, HY_WIDTH, dtype=np.float32))[None, :]


def _filter_kernel(z_ref, w1_ref, b1_ref, w2_ref, b2_ref, fr_ref, w3_ref, dl_ref, o_ref):
    hi = lax.Precision.HIGHEST
    tl = z_ref.shape[0]
    z = z_ref[...]
    fr = fr_ref[...]
    h = jnp.sin(fr * (jnp.dot(z, w1_ref[...], precision=hi, preferred_element_type=F32) + b1_ref[...]))
    h = jnp.sin(fr * (jnp.dot(h, w2_ref[...], precision=hi, preferred_element_type=F32) + b2_ref[...]))
    t = z[:, 0:1]
    window = jnp.exp(-t * dl_ref[...])
    row = pl.program_id(0) * tl + lax.broadcasted_iota(jnp.int32, (tl, 1), 0)
    for s in range(2 * HY_ORDER):
        sl = slice(s * HY_WIDTH, (s + 1) * HY_WIDTH)
        f = jnp.dot(h, w3_ref[:, sl], precision=hi, preferred_element_type=F32) * window
        if s % 2 == 1:
            f = jnp.where(row > 0, f, 0.0)
        o_ref[s] = f.astype(o_ref.dtype)


def _hyena_filters(seq_len, w1, b1, w2, b2, freq, w3):
    tl = min(FILT_TL, seq_len)
    z = jnp.asarray(_filter_embedding(seq_len))
    w1p = jnp.zeros((HY_EMB_PAD, HY_FILTER_HIDDEN), F32).at[:HY_EMB_DIM].set(w1.astype(F32))
    hid = HY_FILTER_HIDDEN
    n_f = 2 * HY_ORDER
    return pl.pallas_call(
        _filter_kernel,
        grid=(seq_len // tl,),
        in_specs=[
            pl.BlockSpec((tl, HY_EMB_PAD), lambda i: (i, 0)),
            _resident((HY_EMB_PAD, hid)), _resident((1, hid)),
            _resident((hid, hid)), _resident((1, hid)), _resident((1, hid)),
            _resident((hid, n_f * HY_WIDTH)), _resident((1, HY_WIDTH)),
        ],
        out_specs=pl.BlockSpec((n_f, tl, HY_WIDTH), lambda i: (0, i, 0)),
        out_shape=jax.ShapeDtypeStruct((n_f, seq_len, HY_WIDTH), BF16),
        compiler_params=_cparams("parallel"),
        name="hyena_filters",
    )(z, w1p, b1.astype(F32).reshape(1, hid), w2.astype(F32), b2.astype(F32).reshape(1, hid),
      freq.astype(F32).reshape(1, hid), w3.astype(F32), jnp.asarray(_filter_deltas()))


def _fft_dims(seq_len):
    n = 2 * seq_len
    n1 = 1 << ((n.bit_length() - 1) // 2)
    n2 = n // n1
    assert n1 * n2 == n and n1 == n2, "sequence length must give a square transform"
    return n1, n2


def _stack_complex(fr, fi):
    return np.block([[fr, -fi], [fi, fr]])


@functools.lru_cache(maxsize=None)
def _dft_constants(seq_len):
    n1, n2 = _fft_dims(seq_len)
    n = n1 * n2
    k1 = np.arange(n1)[:, None].astype(np.float64)
    t1 = np.arange(n1 // 2)[None, :].astype(np.float64)
    a1 = -2.0 * math.pi * k1 * t1 / n1
    f1r, f1i = np.cos(a1), np.sin(a1)
    w1_complex = _stack_complex(f1r, f1i)
    w1_real = np.concatenate([f1r, f1i], axis=0)
    k2 = np.arange(n2)[:, None].astype(np.float64)
    t2 = np.arange(n2)[None, :].astype(np.float64)
    a2 = -2.0 * math.pi * k2 * t2 / n2
    f2r, f2i = np.cos(a2), np.sin(a2)
    w2 = _stack_complex(f2r, f2i)
    w2_inv = _stack_complex(f2r, -f2i)
    at = -2.0 * math.pi * k1 * t2 / n
    tw = np.stack([np.cos(at), np.sin(at)], axis=0)[..., None]
    tw = np.broadcast_to(tw, (2, n1, n2, LANE)).astype(np.float32)
    g1r, g1i = f1r.T / n, -f1i.T / n
    w3 = _stack_complex(g1r, g1i)
    kb = np.arange(HY_TW_SPLIT)[:, None].astype(np.float64)
    ab = -2.0 * math.pi * kb * t2 / n
    tbr, tbi = np.cos(ab), np.sin(ab)
    w2f = np.stack([_stack_complex(f2r * tbr[b] - f2i * tbi[b], f2r * tbi[b] + f2i * tbr[b])
                    for b in range(HY_TW_SPLIT)])
    w2i = np.stack([_stack_complex(tbr[b][:, None] * f2r - tbi[b][:, None] * f2i,
                                   -(tbr[b][:, None] * f2i + tbi[b][:, None] * f2r))
                    for b in range(HY_TW_SPLIT)])
    ka = (np.arange(n1 // HY_TW_SPLIT) * HY_TW_SPLIT)[:, None].astype(np.float64)
    aa = -2.0 * math.pi * ka * t2 / n
    ta = np.stack([np.cos(aa), np.sin(aa)], axis=0)[..., None]
    ta = np.broadcast_to(ta, (2, n1 // HY_TW_SPLIT, n2, LANE)).astype(np.float32)
    as_bf = lambda a: jnp.asarray(a, F32).astype(BF16)
    return dict(w1_complex=as_bf(w1_complex), w1_real=as_bf(w1_real), w2=as_bf(w2), w2f=as_bf(w2f),
                w2i=as_bf(w2i), tw=jnp.asarray(tw), ta=jnp.asarray(ta), w3=as_bf(w3))


def _s1_kernel(w_ref, x_ref, o_ref):
    tl = x_ref.shape[-1]
    x = x_ref[...].reshape(-1, tl).astype(BF16)
    o_ref[0] = jnp.dot(w_ref[...], x, preferred_element_type=F32).astype(o_ref.dtype)


def _fft_stage1(u, w1, *, pair):
    s, seq_len, c = u.shape
    n1, n2 = _fft_dims(seq_len)
    per = 2 if pair else 1
    lanes = n2 * c
    tl = min(FFT_LANE_TILE, lanes)
    uv = u.reshape(s, n1 // 2, lanes)
    out = pl.pallas_call(
        _s1_kernel,
        grid=(s // per, lanes // tl),
        in_specs=[
            _resident(w1.shape),
            pl.BlockSpec((per, n1 // 2, tl), lambda p, j: (p, 0, j)),
        ],
        out_specs=pl.BlockSpec((1, 2 * n1, tl), lambda p, j: (p, 0, j)),
        out_shape=jax.ShapeDtypeStruct((s // per, 2 * n1, lanes), BF16),
        compiler_params=_cparams("parallel", "arbitrary"),
        name="hyena_dft_stage1",
    )(w1, uv)
    return out.reshape(s // per, 2, n1, n2, c)


def _lane_tile(x, c):
    return jnp.concatenate([x] * (c // LANE), axis=-1)


def _twiddled_stack(a_ref, twr, twi):
    ar = a_ref[0, 0, 0].astype(F32)
    ai = a_ref[0, 1, 0].astype(F32)
    return jnp.concatenate([ar * twr - ai * twi, ar * twi + ai * twr], axis=0).astype(BF16)


def _pack_pair(a, b):
    return pltpu.pack_elementwise([a, b], packed_dtype=BF16)


def _unpack_pair(w):
    return (pltpu.unpack_elementwise(w, index=0, packed_dtype=BF16, unpacked_dtype=F32),
            pltpu.unpack_elementwise(w, index=1, packed_dtype=BF16, unpacked_dtype=F32))


def _conv_pitch(n2):
    return n2 + 8


def _hyena_conv_kernel(z_ref, g_ref, kf_ref, w1_ref, w2f_ref, w2i_ref, w3_ref, ta_ref, bias_ref, o_ref,
                       x_scr, a_scr, *, n1, n2):
    pitch = _conv_pitch(n2)
    half = n1 // 2
    groups = n1 // HY_TW_SPLIT

    def pack_in(t1, carry):
        rows = pl.ds(pl.multiple_of(t1 * n2, n2), n2)
        x_scr[pl.ds(pl.multiple_of(t1 * pitch, 8), n2), :] = _pack_pair(
            z_ref[0, rows, :].astype(F32), z_ref[1, rows, :].astype(F32))
        return carry

    lax.fori_loop(0, half, pack_in, 0)

    def stage1(j, carry):
        cols = []
        for u in range(2):
            xr, xi = _unpack_pair(x_scr[pl.ds(2 * j + u, half, stride=pitch), :])
            cols.append(jnp.concatenate([xr, xi], axis=0).astype(BF16))
        out = jnp.dot(w1_ref[...], jnp.concatenate(cols, axis=1), preferred_element_type=F32)
        for u in range(2):
            o = out[:, u * LANE:(u + 1) * LANE]
            a_scr[pl.ds(2 * j + u, n1, stride=pitch), :] = _pack_pair(o[:n1], o[n1:])
        return carry

    lax.fori_loop(0, n2 // 2, stage1, 0, unroll=4)

    for kb in range(HY_TW_SPLIT):
        def stage2(jp, carry, kb=kb):
            rows, tws, stacks = [], [], []
            for u in range(2):
                ka = 2 * jp + u
                k1 = ka * HY_TW_SPLIT + kb
                rows.append(pl.ds(pl.multiple_of(k1 * pitch, 8), n2))
                ar, ai = _unpack_pair(a_scr[rows[u], :])
                twr, twi = ta_ref[0, ka], ta_ref[1, ka]
                tws.append((twr, twi))
                stacks.append(jnp.concatenate([ar * twr - ai * twi, ar * twi + ai * twr], axis=0).astype(BF16))
            spec = jnp.dot(w2f_ref[kb], jnp.concatenate(stacks, axis=1), preferred_element_type=F32)
            prods = []
            for u in range(2):
                k1 = (2 * jp + u) * HY_TW_SPLIT + kb
                sr, si = spec[:n2, u * LANE:(u + 1) * LANE], spec[n2:, u * LANE:(u + 1) * LANE]
                kr = kf_ref[0, 0, k1].astype(F32)
                ki = kf_ref[0, 1, k1].astype(F32)
                prods.append(jnp.concatenate([sr * kr - si * ki, sr * ki + si * kr], axis=0).astype(BF16))
            back = jnp.dot(w2i_ref[kb], jnp.concatenate(prods, axis=1), preferred_element_type=F32)
            for u in range(2):
                br, bi = back[:n2, u * LANE:(u + 1) * LANE], back[n2:, u * LANE:(u + 1) * LANE]
                twr, twi = tws[u]
                a_scr[rows[u], :] = _pack_pair(br * twr + bi * twi, bi * twr - br * twi)
            return carry

        lax.fori_loop(0, groups // 2, stage2, 0, unroll=4)

    def stage3(j, carry):
        cols = []
        for u in range(2):
            br, bi = _unpack_pair(a_scr[pl.ds(2 * j + u, n1, stride=pitch), :])
            cols.append(jnp.concatenate([br, bi], axis=0).astype(BF16))
        y = jnp.dot(w3_ref[...], jnp.concatenate(cols, axis=1), preferred_element_type=F32)
        for u in range(2):
            sl = pl.ds(2 * j + u, half, stride=pitch)
            za, zb = _unpack_pair(x_scr[sl, :])
            yu = y[:, u * LANE:(u + 1) * LANE]
            x_scr[sl, :] = _pack_pair(yu[:half] + bias_ref[...] * za, yu[half:] + bias_ref[...] * zb)
        return carry

    lax.fori_loop(0, n2 // 2, stage3, 0, unroll=4)

    def gate_out(t1, carry):
        rows = pl.ds(pl.multiple_of(t1 * n2, n2), n2)
        ta, tb = _unpack_pair(x_scr[pl.ds(pl.multiple_of(t1 * pitch, 8), n2), :])
        o_ref[0, rows, :] = (g_ref[0, rows, :].astype(F32) * ta).astype(o_ref.dtype)
        o_ref[1, rows, :] = (g_ref[1, rows, :].astype(F32) * tb).astype(o_ref.dtype)
        return carry

    lax.fori_loop(0, half, gate_out, 0)


def _hyena_conv(z, gate, kf, order, bias, consts):
    s, seq_len, c = z.shape
    n1, n2 = _fft_dims(seq_len)
    pitch = _conv_pitch(n2)
    assert s % 2 == 0 and c % LANE == 0 and (n1 // HY_TW_SPLIT) % 2 == 0
    seq_blk = pl.BlockSpec((2, seq_len, LANE), lambda cc, p: (p, 0, cc))
    return pl.pallas_call(
        functools.partial(_hyena_conv_kernel, n1=n1, n2=n2),
        grid=(c // LANE, s // 2),
        in_specs=[
            seq_blk,
            pl.BlockSpec((2, seq_len, LANE), lambda cc, p: (p, 0, cc), pipeline_mode=pl.Buffered(1)),
            pl.BlockSpec((1, 2, n1, n2, LANE), lambda cc, p: (order, 0, 0, 0, cc), pipeline_mode=pl.Buffered(1)),
            _resident(consts["w1_complex"].shape),
            _resident(consts["w2f"].shape),
            _resident(consts["w2i"].shape),
            _resident(consts["w3"].shape),
            _resident(consts["ta"].shape),
            pl.BlockSpec((1, LANE), lambda cc, p: (0, cc)),
        ],
        out_specs=seq_blk,
        out_shape=jax.ShapeDtypeStruct(z.shape, BF16),
        scratch_shapes=[pltpu.VMEM((n1 // 2 * pitch, LANE), jnp.uint32),
                        pltpu.VMEM((n1 * pitch, LANE), jnp.uint32)],
        compiler_params=_cparams("parallel", "arbitrary"),
        name="hyena_conv",
    )(z, gate, kf, consts["w1_complex"], consts["w2f"], consts["w2i"], consts["w3"], consts["ta"],
      bias.astype(F32).reshape(1, c))


def _s2f_kernel(af_ref, ab_ref, tw_ref, w2_ref, o_ref):
    n2, c = af_ref.shape[-2:]
    twr = _lane_tile(tw_ref[0, 0], c)
    twi = _lane_tile(tw_ref[1, 0], c)
    hf = jnp.dot(w2_ref[...], _twiddled_stack(af_ref, twr, twi), preferred_element_type=F32)
    hb = jnp.dot(w2_ref[...], _twiddled_stack(ab_ref, twr, twi), preferred_element_type=F32)
    o_ref[0, 0, 0] = (hf[:n2] + hb[:n2]).astype(o_ref.dtype)
    o_ref[0, 1, 0] = (hf[n2:] - hb[n2:]).astype(o_ref.dtype)


def _filter_spectrum(a_filt, consts):
    _, _, n1, n2, c = a_filt.shape
    return pl.pallas_call(
        _s2f_kernel,
        grid=(HY_ORDER, n1),
        in_specs=[
            pl.BlockSpec((1, 2, 1, n2, c), lambda o, k: (2 * o, 0, k, 0, 0)),
            pl.BlockSpec((1, 2, 1, n2, c), lambda o, k: (2 * o + 1, 0, k, 0, 0)),
            pl.BlockSpec((2, 1, n2, LANE), lambda o, k: (0, k, 0, 0)),
            _resident((2 * n2, 2 * n2)),
        ],
        out_specs=pl.BlockSpec((1, 2, 1, n2, c), lambda o, k: (o, 0, k, 0, 0)),
        out_shape=jax.ShapeDtypeStruct((HY_ORDER, 2, n1, n2, c), BF16),
        compiler_params=_cparams("parallel", "arbitrary"),
        name="hyena_filter_spectrum",
    )(a_filt, a_filt, consts["tw"], consts["w2"])


def _hyena_spectra(seq_len, w1, b1, w2, b2, freq, w3):
    consts = _dft_constants(seq_len)
    filt = _hyena_filters(seq_len, w1, b1, w2, b2, freq, w3)
    a_filt = _fft_stage1(filt, consts["w1_real"], pair=False)
    return _filter_spectrum(a_filt, consts)


def _hyena(h3d, conv_w, conv_b, kf, hy_bias):
    seq_len = h3d.shape[1]
    consts = _dft_constants(seq_len)
    v, x1, x2 = _short_conv(h3d, conv_w, conv_b)
    z = v
    for n, gate in enumerate((x1, x2)):
        z = _hyena_conv(z, gate, kf, n, hy_bias[n], consts)
    return z


def _layernorm(y, g, b):
    mu = jnp.mean(y, axis=-1, keepdims=True)
    d = y - mu
    var = jnp.mean(d * d, axis=-1, keepdims=True)
    return d * lax.rsqrt(var + LN_EPS) * g + b


def _merge_kernel(g_ref, a_ref, hb_ref, c_ref, x_ref, wa_ref, wb_ref, wc_ref, wo_ref, lg_ref, lb_ref, o_ref):
    d = D_MODEL
    subs = [slice(r, r + MERGE_SUB) for r in range(0, x_ref.shape[0], MERGE_SUB)]
    merged = []
    for r in subs:
        acc = g_ref[r, 0:d].astype(F32) * jnp.dot(a_ref[r, :], wa_ref[...], preferred_element_type=F32)
        acc += g_ref[r, d:2 * d].astype(F32) * jnp.dot(hb_ref[r, :], wb_ref[...], preferred_element_type=F32)
        acc += g_ref[r, 2 * d:3 * d].astype(F32) * jnp.dot(c_ref[r, :], wc_ref[...], preferred_element_type=F32)
        merged.append(acc.astype(BF16))
    for r, mg in zip(subs, merged):
        mix = jnp.dot(mg, wo_ref[...], preferred_element_type=F32)
        o_ref[r, :] = _layernorm(DEEPNORM_ALPHA * x_ref[r, :] + mix, lg_ref[...], lb_ref[...])


def _merge(h2d, a, hb, c, x2d, wa, wb, wc, wo, ln_g, ln_b):
    m = x2d.shape[0]
    tm = MERGE_TM
    row = lambda width: pl.BlockSpec((tm, width), lambda i: (i, 0))
    return pl.pallas_call(
        _merge_kernel,
        grid=(m // tm,),
        in_specs=[
            row(N_BRANCH * D_MODEL), row(NA_WIDTH), row(HY_WIDTH), row(SWA_WIDTH), row(D_MODEL),
            _resident((NA_WIDTH, D_MODEL)), _resident((HY_WIDTH, D_MODEL)), _resident((SWA_WIDTH, D_MODEL)),
            _resident((D_MODEL, D_MODEL)), _resident((1, D_MODEL)), _resident((1, D_MODEL)),
        ],
        out_specs=row(D_MODEL),
        out_shape=jax.ShapeDtypeStruct((m, D_MODEL), F32),
        compiler_params=_cparams("parallel"),
        name="merge_ln",
    )(h2d, a, hb, c, x2d, wa, wb, wc, wo, ln_g, ln_b)


def _mlp_kernel(x_ref, wu_ref, bu_ref, wd_ref, bd_ref, lg_ref, lb_ref, o_ref):
    x = x_ref[...]
    xb = x.astype(BF16)
    acc = DEEPNORM_ALPHA * x + bd_ref[...]
    for c in range(D_FF // MLP_FF_CHUNK):
        sl = slice(c * MLP_FF_CHUNK, (c + 1) * MLP_FF_CHUNK)
        up = jnp.dot(xb, wu_ref[:, sl], preferred_element_type=F32) + bu_ref[:, sl]
        up = jnp.square(jnp.maximum(up, 0.0))
        acc += jnp.dot(up.astype(BF16), wd_ref[sl, :], preferred_element_type=F32)
    o_ref[...] = _layernorm(acc, lg_ref[...], lb_ref[...])


def _mlp(x2d, wu, bu, wd, bd, ln_g, ln_b):
    m = x2d.shape[0]
    tm = MLP_TM
    row = pl.BlockSpec((tm, D_MODEL), lambda i: (i, 0))
    return pl.pallas_call(
        _mlp_kernel,
        grid=(m // tm,),
        in_specs=[
            row, _resident((D_MODEL, D_FF)), _resident((1, D_FF)), _resident((D_FF, D_MODEL)),
            _resident((1, D_MODEL)), _resident((1, D_MODEL)), _resident((1, D_MODEL)),
        ],
        out_specs=row,
        out_shape=jax.ShapeDtypeStruct((m, D_MODEL), F32),
        compiler_params=_cparams("parallel"),
        name="mlp_ln",
    )(x2d, wu, bu, wd, bd, ln_g, ln_b)


def _permute_in_columns(a):
    q0 = OFF_SWA
    order = [h for j in range(SWA_GROUP) for h in (j, j + SWA_GROUP)]
    swa_q = [a[..., q0 + h * HEAD_DIM:q0 + (h + 1) * HEAD_DIM] for h in order]
    swa_k = a[..., q0 + SWA_WIDTH:q0 + SWA_WIDTH + SWA_KV_WIDTH]
    return jnp.concatenate([a[..., OFF_GATE:], a[..., OFF_HY:OFF_SWA], a[..., :2 * NA_WIDTH]] + swa_q + [swa_k],
                           axis=-1)


def _scale_query_columns(a):
    c = HEAD_DIM ** -0.5 * LOG2E
    return jnp.concatenate([a[..., :NA_WIDTH] * c, a[..., NA_WIDTH:OFF_SWA],
                            a[..., OFF_SWA:OFF_SWA + SWA_WIDTH] * c, a[..., OFF_SWA + SWA_WIDTH:]], axis=-1)


def _values_columns(a):
    v_swa = OFF_SWA + SWA_WIDTH + SWA_KV_WIDTH
    return jnp.concatenate([a[..., 2 * NA_WIDTH:3 * NA_WIDTH], a[..., v_swa:v_swa + SWA_KV_WIDTH]], axis=-1)


def _prepare_layer(l, p):
    row = lambda a: a.astype(F32).reshape(1, -1)
    return dict(
        w_in=_permute_in_columns(_scale_query_columns(p["w_in"][l])).astype(BF16),
        b_in=row(_permute_in_columns(_scale_query_columns(p["b_in"][l]))),
        conv_w=p["hy_conv_w"][l], conv_b=p["hy_conv_b"][l], hy_bias=p["hy_bias"][l],
        filt=(p["hy_filt_w1"][l], p["hy_filt_b1"][l], p["hy_filt_w2"][l], p["hy_filt_b2"][l],
              p["hy_filt_freq"][l], p["hy_filt_w3"][l]),
        w_vt=_values_columns(p["w_in"][l]).T.astype(BF16),
        b_vt=_values_columns(p["b_in"][l]).astype(F32).reshape(VT_WIDTH, 1),
        na_bias=_na_bias_table(p["na_rpb"][l]),
        sink=p["swa_sink"][l],
        wa=p["w_branch_a"][l].astype(BF16), wb=p["w_branch_b"][l].astype(BF16),
        wc=p["w_branch_c"][l].astype(BF16), wo=p["w_out"][l].astype(BF16),
        ln1_g=row(p["ln1_g"][l]), ln1_b=row(p["ln1_b"][l]),
        wu=p["w_up"][l].astype(BF16), bu=row(p["b_up"][l]),
        wd=p["w_down"][l].astype(BF16), bd=row(p["b_down"][l]),
        ln2_g=row(p["ln2_g"][l]), ln2_b=row(p["ln2_b"][l]),
    )


def _encoder_block(x, lp, swa_table):
    b_sz, seq_len, _ = x.shape
    m = b_sz * seq_len
    x2d = x.reshape(m, D_MODEL)
    h2d, vt = _inproj(x2d, lp["w_in"], lp["b_in"], lp["w_vt"], lp["b_vt"])
    h3d = h2d.reshape(b_sz, seq_len, D_TOK)
    a = _na_attention(h3d, vt, lp["na_bias"])
    kf = _hyena_spectra(seq_len, *lp["filt"])
    hb = _hyena(h3d, lp["conv_w"], lp["conv_b"], kf, lp["hy_bias"])
    c = _swa_attention(h3d, vt, lp["sink"], swa_table)
    x1 = _merge(h2d, a.reshape(m, NA_WIDTH), hb.reshape(m, HY_WIDTH), c.reshape(m, SWA_WIDTH), x2d,
                lp["wa"], lp["wb"], lp["wc"], lp["wo"], lp["ln1_g"], lp["ln1_b"])
    x2 = _mlp(x1, lp["wu"], lp["bu"], lp["wd"], lp["bd"], lp["ln2_g"], lp["ln2_b"])
    return x2.reshape(b_sz, seq_len, D_MODEL)


def kernel(x_prompt, x_sample, w_in, b_in, hy_conv_w, hy_conv_b, hy_filt_w1, hy_filt_b1, hy_filt_w2,
           hy_filt_b2, hy_filt_freq, hy_filt_w3, hy_bias, na_rpb, swa_sink, w_branch_a, w_branch_b,
           w_branch_c, w_out, ln1_g, ln1_b, w_up, b_up, w_down, b_down, ln2_g, ln2_b):
    params = dict(w_in=w_in, b_in=b_in, hy_conv_w=hy_conv_w, hy_conv_b=hy_conv_b, hy_filt_w1=hy_filt_w1,
                  hy_filt_b1=hy_filt_b1, hy_filt_w2=hy_filt_w2, hy_filt_b2=hy_filt_b2,
                  hy_filt_freq=hy_filt_freq, hy_filt_w3=hy_filt_w3, hy_bias=hy_bias, na_rpb=na_rpb,
                  swa_sink=swa_sink, w_branch_a=w_branch_a, w_branch_b=w_branch_b, w_branch_c=w_branch_c,
                  w_out=w_out, ln1_g=ln1_g, ln1_b=ln1_b, w_up=w_up, b_up=b_up, w_down=w_down,
                  b_down=b_down, ln2_g=ln2_g, ln2_b=ln2_b)
    swa_table = _swa_table()
    y_prompt = x_prompt
    y_sample = x_sample
    for l in range(DEPTH):
        lp = _prepare_layer(l, params)
        y_prompt = _encoder_block(y_prompt, lp, swa_table)
        y_sample = _encoder_block(y_sample, lp, swa_table)
    return (y_prompt, y_sample)
```

```python
import functools
import math

import numpy as np
import jax
import jax.numpy as jnp
from jax import lax
from jax.experimental import pallas as pl
from jax.experimental.pallas import tpu as pltpu

F32 = jnp.float32
BF16 = jnp.bfloat16

D_MODEL = 1024
DEPTH = 2
HEAD_DIM = 64
GRID_W = 64
NA_HEADS = 8
NA_WIN_ROWS = 8
NA_WIN_COLS = 16
NA_WIDTH = NA_HEADS * HEAD_DIM
HY_WIDTH = D_MODEL // 2
HY_ORDER = 2
HY_SHORT_CONV = 3
HY_POS_BANDS = 16
HY_EMB_DIM = 1 + 2 * HY_POS_BANDS
HY_EMB_PAD = 128
HY_FILTER_HIDDEN = 64
HY_FAST_DECAY_PCT = 0.3
HY_SLOW_DECAY_PCT = 1.5
HY_DECAY_TARGET = 1e-2
SWA_HEADS = 8
SWA_KV_HEADS = 2
SWA_GROUP = SWA_HEADS // SWA_KV_HEADS
SWA_WIDTH = SWA_HEADS * HEAD_DIM
SWA_KV_WIDTH = SWA_KV_HEADS * HEAD_DIM
SWA_WINDOW = 128
SWA_BLOCK = 128
N_BRANCH = 3
D_FF = 4 * D_MODEL
OFF_HY = 3 * NA_WIDTH
OFF_SWA = OFF_HY + 3 * HY_WIDTH
OFF_GATE = OFF_SWA + SWA_WIDTH + 2 * SWA_KV_WIDTH
D_IN = OFF_GATE + N_BRANCH * D_MODEL
DEEPNORM_ALPHA = (2 * DEPTH) ** 0.25
LN_EPS = 1e-5
NEG_BIG = -1e30
LOG2E = 1.4426950408889634

P_GATE = 0
P_HY = N_BRANCH * D_MODEL
P_NA = P_HY + 3 * HY_WIDTH
P_SWA = P_NA + 2 * NA_WIDTH
D_TOK = P_SWA + SWA_WIDTH + SWA_KV_WIDTH
VT_WIDTH = NA_WIDTH + SWA_KV_WIDTH

LANE = 128
VMEM_LIMIT_BYTES = 56 * 1024 * 1024

NA_PAIR_ROWS = NA_WIN_ROWS + 2
ATTN_BLOCKS_PER_STEP = 2
INPROJ_TM = 512
INPROJ_TN = 768
MERGE_TM = 1024
MERGE_SUB = 256
MLP_TM = 512
MLP_FF_CHUNK = 1024
CONV_TL = 256
FILT_TL = 512
HY_TW_SPLIT = 8


def _cparams(*sem):
    return pltpu.CompilerParams(dimension_semantics=sem, vmem_limit_bytes=VMEM_LIMIT_BYTES)


def _resident(shape):
    nd = len(shape)
    return pl.BlockSpec(shape, lambda *_: (0,) * nd, pipeline_mode=pl.Buffered(1))


def _inproj_kernel(x_ref, w_ref, b_ref, wvt_ref, bvt_ref, o_ref, vt_ref):
    xb = x_ref[...].astype(BF16)
    vt = lax.dot_general(wvt_ref[...], xb, (((1,), (1,)), ((), ())), preferred_element_type=F32)
    vt_ref[...] = (vt + bvt_ref[...]).astype(vt_ref.dtype)
    for c0 in range(0, D_TOK, INPROJ_TN):
        cols = slice(c0, min(c0 + INPROJ_TN, D_TOK))
        acc = jnp.dot(xb, w_ref[:, cols], preferred_element_type=F32) + b_ref[:, cols]
        if cols.stop <= P_HY:
            acc = 1.0 / (1.0 + jnp.exp(-acc))
        o_ref[:, cols] = acc.astype(o_ref.dtype)


def _inproj(x2d, w_bf, b_row, wvt_bf, bvt_col):
    m = x2d.shape[0]
    tm = min(INPROJ_TM, m)
    assert m % tm == 0 and P_HY % INPROJ_TN == 0
    return pl.pallas_call(
        _inproj_kernel,
        grid=(m // tm,),
        in_specs=[
            pl.BlockSpec((tm, D_MODEL), lambda i: (i, 0)),
            _resident((D_MODEL, D_TOK)),
            _resident((1, D_TOK)),
            _resident((VT_WIDTH, D_MODEL)),
            _resident((VT_WIDTH, 1)),
        ],
        out_specs=[pl.BlockSpec((tm, D_TOK), lambda i: (i, 0)),
                   pl.BlockSpec((VT_WIDTH, tm), lambda i: (0, i))],
        out_shape=[jax.ShapeDtypeStruct((m, D_TOK), BF16), jax.ShapeDtypeStruct((VT_WIDTH, m), BF16)],
        compiler_params=_cparams("parallel"),
        name="inproj",
    )(x2d, w_bf, b_row, wvt_bf, bvt_col)


def _na_bias_table(rpb):
    kr = NA_WIN_ROWS
    pad = GRID_W - NA_WIN_COLS
    p = jnp.pad(rpb.astype(F32) * LOG2E, ((0, 0), (0, 0), (pad, pad)))
    cols = jnp.stack([p[:, :, GRID_W - 1 - w:2 * GRID_W - 1 - w] for w in range(GRID_W)], axis=2)
    t = jnp.stack([cols[:, kr - 1 - d:2 * kr - 1 - d] for d in range(kr)], axis=0)
    w = np.arange(GRID_W)[:, None]
    kc = np.arange(GRID_W)[None, :]
    col_start = np.clip(w - NA_WIN_COLS // 2, 0, GRID_W - NA_WIN_COLS)
    valid = (kc >= col_start) & (kc < col_start + NA_WIN_COLS)
    t = jnp.where(valid[None, None, None], t, NEG_BIG)
    t = jnp.transpose(t, (0, 1, 2, 4, 3))
    return t.reshape(kr, NA_HEADS, kr * GRID_W, GRID_W)


def _na_pair_geometry(p, rows):
    base = min(max(2 * p - NA_WIN_ROWS // 2, 0), rows - NA_PAIR_ROWS)
    geo = []
    for j in range(2):
        r = 2 * p + j
        rs = min(max(r - NA_WIN_ROWS // 2, 0), rows - NA_WIN_ROWS)
        assert 0 <= rs - base <= NA_PAIR_ROWS - NA_WIN_ROWS
        geo.append((rs - base, r - rs))
    return base, geo


def _na_pair_table(table, rows):
    n_pairs = rows // 2
    variants = []
    for p in (0, 1, 2, n_pairs - 2, n_pairs - 1):
        _, geo = _na_pair_geometry(p, rows)
        cols = []
        for off, delta in geo:
            after = NA_PAIR_ROWS - NA_WIN_ROWS - off
            cols.append(jnp.pad(table[delta], ((0, 0), (off * GRID_W, after * GRID_W), (0, 0)),
                                constant_values=NEG_BIG))
        variants.append(jnp.concatenate(cols, axis=-1))
    return jnp.stack(variants, axis=0)


def _softmax_keys(s, sink=None):
    m = jnp.max(s, axis=0, keepdims=True)
    if sink is not None:
        m = jnp.maximum(m, sink)
    p = jnp.exp2(s - m)
    l = jnp.sum(p, axis=0, keepdims=True)
    if sink is not None:
        l = l + jnp.exp2(sink - m)
    return p.astype(BF16), l


def _half_lanes(x, upper):
    lane = lax.broadcasted_iota(jnp.int32, (1, LANE), 1)
    mine = (lane >= HEAD_DIM) if upper else (lane < HEAD_DIM)
    return jnp.where(mine, x, jnp.zeros((), x.dtype))


def _na_kernel(q_ref, k_ref, vt_ref, *rest, rows, pairs_per_step):
    bias_refs, o_ref = rest[:-1], rest[-1]
    span = NA_PAIR_ROWS * GRID_W
    work = []
    for u in range(pairs_per_step):
        pair = pl.program_id(1) * pairs_per_step + u
        base = jnp.clip(2 * pair - NA_WIN_ROWS // 2, 0, rows - NA_PAIR_ROWS)
        start = pl.multiple_of(base * GRID_W, 2 * GRID_W)
        q = q_ref[0, u * LANE:(u + 1) * LANE, :]
        k = k_ref[0, pl.ds(start, span), :]
        vt = vt_ref[:, pl.ds(start, span)]
        for h in range(NA_HEADS):
            grp = slice((h // 2) * LANE, (h // 2 + 1) * LANE)
            s = lax.dot_general(k[:, grp], _half_lanes(q[:, grp], h % 2 == 1), (((1,), (1,)), ((), ())),
                                preferred_element_type=F32)
            work.append((vt[h * HEAD_DIM:(h + 1) * HEAD_DIM, :],) + _softmax_keys(s + bias_refs[u][0, h]))
    outs = [jnp.dot(v, p, preferred_element_type=F32) / l for v, p, l in work]
    for u in range(pairs_per_step):
        o_t = jnp.concatenate(outs[u * NA_HEADS:(u + 1) * NA_HEADS], axis=0)
        o_ref[0, u * LANE:(u + 1) * LANE, :] = o_t.T.astype(o_ref.dtype)


def _na_attention(h3d, vt, bias_table):
    b_sz, seq_len, _ = h3d.shape
    rows = seq_len // GRID_W
    n_pairs = rows // 2
    assert rows % 2 == 0 and n_pairs >= 5
    cb = P_NA // NA_WIDTH
    span = NA_PAIR_ROWS * GRID_W
    pair_table = _na_pair_table(bias_table, rows)

    def variant(p):
        return jnp.where(p < 2, p, jnp.where(p > n_pairs - 3, p - (n_pairs - 5), 2))

    pps = ATTN_BLOCKS_PER_STEP
    assert n_pairs % pps == 0 and 2 * GRID_W == LANE
    bias_specs = [pl.BlockSpec((1, NA_HEADS, span, LANE), lambda b, p, u=u: (variant(p * pps + u), 0, 0, 0))
                  for u in range(pps)]
    return pl.pallas_call(
        functools.partial(_na_kernel, rows=rows, pairs_per_step=pps),
        grid=(b_sz, n_pairs // pps),
        in_specs=[
            pl.BlockSpec((1, pps * LANE, NA_WIDTH), lambda b, p: (b, p, cb)),
            pl.BlockSpec((1, seq_len, NA_WIDTH), lambda b, p: (b, 0, cb + 1), pipeline_mode=pl.Buffered(1)),
            pl.BlockSpec((NA_WIDTH, seq_len), lambda b, p: (0, b), pipeline_mode=pl.Buffered(1)),
        ] + bias_specs,
        out_specs=pl.BlockSpec((1, pps * LANE, NA_WIDTH), lambda b, p: (b, p, 0)),
        out_shape=jax.ShapeDtypeStruct((b_sz, seq_len, NA_WIDTH), BF16),
        compiler_params=_cparams("parallel", "arbitrary"),
        name="na_attention",
    )(h3d, h3d, vt, *([pair_table] * pps))


def _swa_table():
    span = SWA_BLOCK + 2 * SWA_WINDOW
    slopes = 2.0 ** (-8.0 * (np.arange(SWA_HEADS, dtype=np.float64) + 1.0) / SWA_HEADS)
    kk = np.arange(span)[:, None]
    t = np.arange(SWA_BLOCK)[None, :]
    variants = []
    for off in (0, SWA_BLOCK, 2 * SWA_BLOCK):
        rel = np.abs(kk - off - t)
        a = -slopes[:, None, None] * rel[None].astype(np.float64) * LOG2E
        variants.append(np.where((rel <= SWA_WINDOW)[None], a, NEG_BIG))
    return jnp.asarray(np.stack(variants), F32)


def _swa_kernel(sink_ref, q_ref, k_ref, vt_ref, *rest, seq_len, blocks_per_step):
    tab_refs, o_ref = rest[:-1], rest[-1]
    span = SWA_BLOCK + 2 * SWA_WINDOW
    work = []
    for u in range(blocks_per_step):
        blk = pl.program_id(1) * blocks_per_step + u
        start = pl.multiple_of(jnp.clip((blk - 1) * SWA_BLOCK, 0, seq_len - span), SWA_BLOCK)
        q = q_ref[0, u * SWA_BLOCK:(u + 1) * SWA_BLOCK, :]
        k = k_ref[0, pl.ds(start, span), :]
        vt = vt_ref[:, pl.ds(start, span)]
        for h in range(SWA_HEADS):
            g = h // SWA_GROUP
            grp = slice((h % SWA_GROUP) * LANE, (h % SWA_GROUP + 1) * LANE)
            s = lax.dot_general(k, _half_lanes(q[:, grp], g == 1), (((1,), (1,)), ((), ())),
                                preferred_element_type=F32)
            work.append((vt[g * HEAD_DIM:(g + 1) * HEAD_DIM, :],)
                        + _softmax_keys(s + tab_refs[u][0, h], sink_ref[h] * LOG2E))
    outs = [jnp.dot(v, p, preferred_element_type=F32) / l for v, p, l in work]
    for u in range(blocks_per_step):
        o_t = jnp.concatenate(outs[u * SWA_HEADS:(u + 1) * SWA_HEADS], axis=0)
        o_ref[0, u * SWA_BLOCK:(u + 1) * SWA_BLOCK, :] = o_t.T.astype(o_ref.dtype)


def _swa_attention(h3d, vt, sink, table):
    b_sz, seq_len, _ = h3d.shape
    nb = seq_len // SWA_BLOCK
    span = SWA_BLOCK + 2 * SWA_WINDOW
    assert SWA_KV_HEADS == 2 and SWA_BLOCK == LANE and nb >= 3
    qb = P_SWA // SWA_WIDTH
    kb = (P_SWA + SWA_WIDTH) // SWA_KV_WIDTH

    def variant(i):
        return jnp.where(i == 0, 0, jnp.where(i == nb - 1, 2, 1))

    bps = ATTN_BLOCKS_PER_STEP
    assert nb % bps == 0
    tab_specs = [pl.BlockSpec((1, SWA_HEADS, span, SWA_BLOCK), lambda b, i, u=u: (variant(i * bps + u), 0, 0, 0))
                 for u in range(bps)]
    return pl.pallas_call(
        functools.partial(_swa_kernel, seq_len=seq_len, blocks_per_step=bps),
        grid=(b_sz, nb // bps),
        in_specs=[
            pl.BlockSpec(memory_space=pltpu.SMEM),
            pl.BlockSpec((1, bps * SWA_BLOCK, SWA_WIDTH), lambda b, i: (b, i, qb)),
            pl.BlockSpec((1, seq_len, SWA_KV_WIDTH), lambda b, i: (b, 0, kb)),
            pl.BlockSpec((SWA_KV_WIDTH, seq_len), lambda b, i: (NA_WIDTH // SWA_KV_WIDTH, b)),
        ] + tab_specs,
        out_specs=pl.BlockSpec((1, bps * SWA_BLOCK, SWA_WIDTH), lambda b, i: (b, i, 0)),
        out_shape=jax.ShapeDtypeStruct((b_sz, seq_len, SWA_WIDTH), BF16),
        compiler_params=_cparams("parallel", "arbitrary"),
        name="swa_attention",
    )(sink.astype(F32), h3d, h3d, vt, *([table] * bps))


def _conv_kernel(x_ref, prev_ref, next_ref, w_ref, b_ref, v_ref, x1_ref, x2_ref, *, n_tiles):
    i = pl.program_id(1)
    tl = x_ref.shape[1]
    halo = prev_ref.shape[1]
    row = lax.broadcasted_iota(jnp.int32, (tl, 1), 0)
    outs = (v_ref, x1_ref, x2_ref)
    for c in range(3):
        sl = slice(c * HY_WIDTH, (c + 1) * HY_WIDTH)
        x = x_ref[0, :, sl].astype(F32)
        prev_row = jnp.where(i > 0, prev_ref[0, halo - 1:halo, sl].astype(F32), 0.0)
        next_row = jnp.where(i < n_tiles - 1, next_ref[0, 0:1, sl].astype(F32), 0.0)
        xm = jnp.where(row == 0, prev_row, pltpu.roll(x, 1, axis=0))
        xp = jnp.where(row == tl - 1, next_row, pltpu.roll(x, tl - 1, axis=0))
        u = xm * w_ref[0:1, sl] + x * w_ref[1:2, sl] + xp * w_ref[2:3, sl] + b_ref[:, sl]
        outs[c][0] = u.astype(BF16)


def _short_conv(h3d, conv_w, conv_b):
    b_sz, seq_len, _ = h3d.shape
    tl = CONV_TL
    halo = 16
    n_tiles = seq_len // tl
    width = 3 * HY_WIDTH
    cb = P_HY // width
    assert P_HY % width == 0 and seq_len % tl == 0
    per = tl // halo
    out = jax.ShapeDtypeStruct((b_sz, seq_len, HY_WIDTH), BF16)
    ospec = pl.BlockSpec((1, tl, HY_WIDTH), lambda b, i: (b, i, 0))
    return pl.pallas_call(
        functools.partial(_conv_kernel, n_tiles=n_tiles),
        grid=(b_sz, n_tiles),
        in_specs=[
            pl.BlockSpec((1, tl, width), lambda b, i: (b, i, cb)),
            pl.BlockSpec((1, halo, width), lambda b, i: (b, jnp.maximum(i * per - 1, 0), cb)),
            pl.BlockSpec((1, halo, width), lambda b, i: (b, jnp.minimum((i + 1) * per, seq_len // halo - 1), cb)),
            _resident((HY_SHORT_CONV, width)),
            _resident((1, width)),
        ],
        out_specs=[ospec, ospec, ospec],
        out_shape=[out, out, out],
        compiler_params=_cparams("parallel", "arbitrary"),
        name="hyena_short_conv",
    )(h3d, h3d, h3d, conv_w.astype(F32), conv_b.astype(F32).reshape(1, width))


def _filter_embedding(seq_len):
    t = np.linspace(0.0, 1.0, seq_len, dtype=np.float32).astype(np.float64)[:, None]
    w = (2.0 * math.pi * np.arange(seq_len, dtype=np.float32) / seq_len).astype(np.float32)
    bands = np.linspace(1e-4, HY_POS_BANDS - 1, HY_POS_BANDS, dtype=np.float32)
    ang = (w[:, None] * bands[None, :]).astype(np.float32).astype(np.float64)
    z = np.concatenate([t, np.cos(ang), -np.sin(ang)], axis=-1)
    zp = np.zeros((2 * seq_len, HY_EMB_PAD), np.float32)
    zp[:seq_len, :HY_EMB_DIM] = z
    zp[seq_len + 1:, :HY_EMB_DIM] = z[:0:-1]
    return zp


def _filter_deltas():
    min_decay = math.log(HY_DECAY_TARGET) / HY_SLOW_DECAY_PCT
    max_decay = math.log(HY_DECAY_TARGET) / HY_FAST_DECAY_PCT
    return np.abs(np.linspace(min_decay, max_decay, HY_WIDTH, dtype=np.float32))[None, :]


def _filter_kernel(z_ref, w1_ref, b1_ref, w2_ref, b2_ref, fr_ref, w3_ref, dl_ref, o_ref, *, seq_len):
    hi = lax.Precision.HIGHEST
    tl = z_ref.shape[0]
    z = z_ref[...]
    fr = fr_ref[...]
    h = jnp.sin(fr * (jnp.dot(z, w1_ref[...], precision=hi, preferred_element_type=F32) + b1_ref[...]))
    h = jnp.sin(fr * (jnp.dot(h, w2_ref[...], precision=hi, preferred_element_type=F32) + b2_ref[...]))
    t = z[:, 0:1]
    window = jnp.exp(-t * dl_ref[...])
    first_row = pl.program_id(0) * tl
    row = first_row + lax.broadcasted_iota(jnp.int32, (tl, 1), 0)

    def emit(direction):
        for o in range(HY_ORDER):
            c0 = (2 * o + direction) * HY_WIDTH
            f = jnp.dot(h, w3_ref[:, c0:c0 + HY_WIDTH], precision=hi, preferred_element_type=F32) * window
            if direction == 1:
                f = jnp.where(row > seq_len, f, 0.0)
            o_ref[o] = f.astype(o_ref.dtype)

    pl.when(first_row < seq_len)(lambda: emit(0))
    pl.when(first_row >= seq_len)(lambda: emit(1))


def _hyena_kernels(seq_len, w1, b1, w2, b2, freq, w3):
    tl = min(FILT_TL, seq_len)
    assert seq_len % tl == 0
    z = jnp.asarray(_filter_embedding(seq_len))
    w1p = jnp.zeros((HY_EMB_PAD, HY_FILTER_HIDDEN), F32).at[:HY_EMB_DIM].set(w1.astype(F32))
    hid = HY_FILTER_HIDDEN
    return pl.pallas_call(
        functools.partial(_filter_kernel, seq_len=seq_len),
        grid=(2 * seq_len // tl,),
        in_specs=[
            pl.BlockSpec((tl, HY_EMB_PAD), lambda i: (i, 0)),
            _resident((HY_EMB_PAD, hid)), _resident((1, hid)),
            _resident((hid, hid)), _resident((1, hid)), _resident((1, hid)),
            _resident((hid, 2 * HY_ORDER * HY_WIDTH)), _resident((1, HY_WIDTH)),
        ],
        out_specs=pl.BlockSpec((HY_ORDER, tl, HY_WIDTH), lambda i: (0, i, 0)),
        out_shape=jax.ShapeDtypeStruct((HY_ORDER, 2 * seq_len, HY_WIDTH), BF16),
        compiler_params=_cparams("parallel"),
        name="hyena_filters",
    )(z, w1p, b1.astype(F32).reshape(1, hid), w2.astype(F32), b2.astype(F32).reshape(1, hid),
      freq.astype(F32).reshape(1, hid), w3.astype(F32), jnp.asarray(_filter_deltas()))


def _fft_dims(seq_len):
    n = 2 * seq_len
    n1 = 1 << ((n.bit_length() - 1) // 2)
    n2 = n // n1
    assert n1 * n2 == n and n1 == n2, "sequence length must give a square transform"
    return n1, n2


def _stack_complex(fr, fi):
    return np.block([[fr, -fi], [fi, fr]])


@functools.lru_cache(maxsize=None)
def _dft_constants(seq_len):
    n1, n2 = _fft_dims(seq_len)
    n = n1 * n2
    k1 = np.arange(n1)[:, None].astype(np.float64)
    t1 = np.arange(n1 // 2)[None, :].astype(np.float64)
    a1 = -2.0 * math.pi * k1 * t1 / n1
    f1r, f1i = np.cos(a1), np.sin(a1)
    w1_complex = _stack_complex(f1r, f1i)
    a1_full = -2.0 * math.pi * k1 * np.arange(n1)[None, :].astype(np.float64) / n1
    w1_real = np.concatenate([np.cos(a1_full), np.sin(a1_full)], axis=0)
    k2 = np.arange(n2)[:, None].astype(np.float64)
    t2 = np.arange(n2)[None, :].astype(np.float64)
    a2 = -2.0 * math.pi * k2 * t2 / n2
    f2r, f2i = np.cos(a2), np.sin(a2)
    g1r, g1i = f1r.T / n, -f1i.T / n
    w3 = _stack_complex(g1r, g1i)
    kb = np.arange(HY_TW_SPLIT)[:, None].astype(np.float64)
    ab = -2.0 * math.pi * kb * t2 / n
    tbr, tbi = np.cos(ab), np.sin(ab)
    w2f = np.stack([_stack_complex(f2r * tbr[b] - f2i * tbi[b], f2r * tbi[b] + f2i * tbr[b])
                    for b in range(HY_TW_SPLIT)])
    w2i = np.stack([_stack_complex(tbr[b][:, None] * f2r - tbi[b][:, None] * f2i,
                                   -(tbr[b][:, None] * f2i + tbi[b][:, None] * f2r))
                    for b in range(HY_TW_SPLIT)])
    ka = (np.arange(n1 // HY_TW_SPLIT) * HY_TW_SPLIT)[:, None].astype(np.float64)
    aa = -2.0 * math.pi * ka * t2 / n
    ta = np.stack([np.cos(aa), np.sin(aa)], axis=0)[..., None]
    ta = np.broadcast_to(ta, (2, n1 // HY_TW_SPLIT, n2, LANE)).astype(np.float32)
    as_bf = lambda a: jnp.asarray(a, F32).astype(BF16)
    return dict(w1_complex=as_bf(w1_complex), w1_real=as_bf(w1_real), w2f=as_bf(w2f), w2i=as_bf(w2i),
                ta=jnp.asarray(ta), w3=as_bf(w3))


def _pack_pair(a, b):
    return pltpu.pack_elementwise([a, b], packed_dtype=BF16)


def _unpack_pair(w):
    return (pltpu.unpack_elementwise(w, index=0, packed_dtype=BF16, unpacked_dtype=F32),
            pltpu.unpack_elementwise(w, index=1, packed_dtype=BF16, unpacked_dtype=F32))


def _conv_pitch(n2):
    return n2 + 8


def _hyena_conv_kernel(z_ref, g_ref, kf_ref, w1_ref, w2f_ref, w2i_ref, w3_ref, ta_ref, bias_ref, o_ref,
                       x_scr, a_scr, *, n1, n2):
    pitch = _conv_pitch(n2)
    half = n1 // 2
    groups = n1 // HY_TW_SPLIT

    def pack_in(t1, carry):
        rows = pl.ds(pl.multiple_of(t1 * n2, n2), n2)
        x_scr[pl.ds(pl.multiple_of(t1 * pitch, 8), n2), :] = _pack_pair(
            z_ref[0, rows, :].astype(F32), z_ref[1, rows, :].astype(F32))
        return carry

    lax.fori_loop(0, half, pack_in, 0)

    def stage1(j, carry):
        cols = []
        for u in range(2):
            xr, xi = _unpack_pair(x_scr[pl.ds(2 * j + u, half, stride=pitch), :])
            cols.append(jnp.concatenate([xr, xi], axis=0).astype(BF16))
        out = jnp.dot(w1_ref[...], jnp.concatenate(cols, axis=1), preferred_element_type=F32)
        for u in range(2):
            o = out[:, u * LANE:(u + 1) * LANE]
            a_scr[pl.ds(2 * j + u, n1, stride=pitch), :] = _pack_pair(o[:n1], o[n1:])
        return carry

    lax.fori_loop(0, n2 // 2, stage1, 0, unroll=4)

    for kb in range(HY_TW_SPLIT):
        def stage2(jp, carry, kb=kb):
            rows, tws, stacks = [], [], []
            for u in range(2):
                ka = 2 * jp + u
                k1 = ka * HY_TW_SPLIT + kb
                rows.append(pl.ds(pl.multiple_of(k1 * pitch, 8), n2))
                ar, ai = _unpack_pair(a_scr[rows[u], :])
                twr, twi = ta_ref[0, ka], ta_ref[1, ka]
                tws.append((twr, twi))
                stacks.append(jnp.concatenate([ar * twr - ai * twi, ar * twi + ai * twr], axis=0).astype(BF16))
            spec = jnp.dot(w2f_ref[kb], jnp.concatenate(stacks, axis=1), preferred_element_type=F32)
            prods = []
            for u in range(2):
                k1 = (2 * jp + u) * HY_TW_SPLIT + kb
                sr, si = spec[:n2, u * LANE:(u + 1) * LANE], spec[n2:, u * LANE:(u + 1) * LANE]
                kr = kf_ref[0, 0, k1].astype(F32)
                ki = kf_ref[0, 1, k1].astype(F32)
                prods.append(jnp.concatenate([sr * kr - si * ki, sr * ki + si * kr], axis=0).astype(BF16))
            back = jnp.dot(w2i_ref[kb], jnp.concatenate(prods, axis=1), preferred_element_type=F32)
            for u in range(2):
                br, bi = back[:n2, u * LANE:(u + 1) * LANE], back[n2:, u * LANE:(u + 1) * LANE]
                twr, twi = tws[u]
                a_scr[rows[u], :] = _pack_pair(br * twr + bi * twi, bi * twr - br * twi)
            return carry

        lax.fori_loop(0, groups // 2, stage2, 0, unroll=4)

    def stage3(j, carry):
        cols = []
        for u in range(2):
            br, bi = _unpack_pair(a_scr[pl.ds(2 * j + u, n1, stride=pitch), :])
            cols.append(jnp.concatenate([br, bi], axis=0).astype(BF16))
        y = jnp.dot(w3_ref[...], jnp.concatenate(cols, axis=1), preferred_element_type=F32)
        for u in range(2):
            sl = pl.ds(2 * j + u, half, stride=pitch)
            za, zb = _unpack_pair(x_scr[sl, :])
            yu = y[:, u * LANE:(u + 1) * LANE]
            x_scr[sl, :] = _pack_pair(yu[:half] + bias_ref[...] * za, yu[half:] + bias_ref[...] * zb)
        return carry

    lax.fori_loop(0, n2 // 2, stage3, 0, unroll=4)

    def gate_out(t1, carry):
        rows = pl.ds(pl.multiple_of(t1 * n2, n2), n2)
        ta, tb = _unpack_pair(x_scr[pl.ds(pl.multiple_of(t1 * pitch, 8), n2), :])
        o_ref[0, rows, :] = (g_ref[0, rows, :].astype(F32) * ta).astype(o_ref.dtype)
        o_ref[1, rows, :] = (g_ref[1, rows, :].astype(F32) * tb).astype(o_ref.dtype)
        return carry

    lax.fori_loop(0, half, gate_out, 0)


def _hyena_conv(z, gate, kf, order, bias, consts):
    s, seq_len, c = z.shape
    n1, n2 = _fft_dims(seq_len)
    pitch = _conv_pitch(n2)
    assert s % 2 == 0 and c % LANE == 0 and (n1 // HY_TW_SPLIT) % 2 == 0
    seq_blk = pl.BlockSpec((2, seq_len, LANE), lambda cc, p: (p, 0, cc))
    return pl.pallas_call(
        functools.partial(_hyena_conv_kernel, n1=n1, n2=n2),
        grid=(c // LANE, s // 2),
        in_specs=[
            seq_blk,
            seq_blk,
            pl.BlockSpec((1, 2, n1, n2, LANE), lambda cc, p: (order, 0, 0, 0, cc), pipeline_mode=pl.Buffered(1)),
            _resident(consts["w1_complex"].shape),
            _resident(consts["w2f"].shape),
            _resident(consts["w2i"].shape),
            _resident(consts["w3"].shape),
            _resident(consts["ta"].shape),
            pl.BlockSpec((1, LANE), lambda cc, p: (0, cc)),
        ],
        out_specs=seq_blk,
        out_shape=jax.ShapeDtypeStruct(z.shape, BF16),
        scratch_shapes=[pltpu.VMEM((n1 // 2 * pitch, LANE), jnp.uint32),
                        pltpu.VMEM((n1 * pitch, LANE), jnp.uint32)],
        compiler_params=_cparams("parallel", "arbitrary"),
        name="hyena_conv",
    )(z, gate, kf, consts["w1_complex"], consts["w2f"], consts["w2i"], consts["w3"], consts["ta"],
      bias.astype(F32).reshape(1, c))


def _spectrum_kernel(k_ref, w1_ref, w2f_ref, ta_ref, o_ref, x_scr, a_scr, *, n1, n2):
    pitch = _conv_pitch(n2)
    groups = n1 // HY_TW_SPLIT

    def pack_in(t1, carry):
        x_scr[pl.ds(pl.multiple_of(t1 * pitch, 8), n2), :] = (
            k_ref[0, pl.ds(pl.multiple_of(t1 * n2, n2), n2), :].astype(F32))
        return carry

    lax.fori_loop(0, n1, pack_in, 0)

    def stage1(j, carry):
        cols = [x_scr[pl.ds(2 * j + u, n1, stride=pitch), :].astype(BF16) for u in range(2)]
        out = jnp.dot(w1_ref[...], jnp.concatenate(cols, axis=1), preferred_element_type=F32)
        for u in range(2):
            o = out[:, u * LANE:(u + 1) * LANE]
            a_scr[pl.ds(2 * j + u, n1, stride=pitch), :] = _pack_pair(o[:n1], o[n1:])
        return carry

    lax.fori_loop(0, n2 // 2, stage1, 0, unroll=4)

    for kb in range(HY_TW_SPLIT):
        def stage2(jp, carry, kb=kb):
            stacks = []
            for u in range(2):
                ka = 2 * jp + u
                k1 = ka * HY_TW_SPLIT + kb
                ar, ai = _unpack_pair(a_scr[pl.ds(pl.multiple_of(k1 * pitch, 8), n2), :])
                twr, twi = ta_ref[0, ka], ta_ref[1, ka]
                stacks.append(jnp.concatenate([ar * twr - ai * twi, ar * twi + ai * twr], axis=0).astype(BF16))
            spec = jnp.dot(w2f_ref[kb], jnp.concatenate(stacks, axis=1), preferred_element_type=F32)
            for u in range(2):
                k1 = (2 * jp + u) * HY_TW_SPLIT + kb
                o_ref[0, 0, k1] = spec[:n2, u * LANE:(u + 1) * LANE].astype(o_ref.dtype)
                o_ref[0, 1, k1] = spec[n2:, u * LANE:(u + 1) * LANE].astype(o_ref.dtype)
            return carry

        lax.fori_loop(0, groups // 2, stage2, 0, unroll=2)


def _filter_spectrum(kern, consts):
    _, n, c = kern.shape
    n1, n2 = _fft_dims(n // 2)
    pitch = _conv_pitch(n2)
    return pl.pallas_call(
        functools.partial(_spectrum_kernel, n1=n1, n2=n2),
        grid=(HY_ORDER, c // LANE),
        in_specs=[
            pl.BlockSpec((1, n, LANE), lambda o, cc: (o, 0, cc), pipeline_mode=pl.Buffered(1)),
            _resident(consts["w1_real"].shape),
            _resident(consts["w2f"].shape),
            _resident(consts["ta"].shape),
        ],
        out_specs=pl.BlockSpec((1, 2, n1, n2, LANE), lambda o, cc: (o, 0, 0, 0, cc)),
        out_shape=jax.ShapeDtypeStruct((HY_ORDER, 2, n1, n2, c), BF16),
        scratch_shapes=[pltpu.VMEM((n1 * pitch, LANE), F32), pltpu.VMEM((n1 * pitch, LANE), jnp.uint32)],
        compiler_params=_cparams("parallel", "arbitrary"),
        name="hyena_filter_spectrum",
    )(kern, consts["w1_real"], consts["w2f"], consts["ta"])


def _hyena_spectra(seq_len, w1, b1, w2, b2, freq, w3):
    consts = _dft_constants(seq_len)
    return _filter_spectrum(_hyena_kernels(seq_len, w1, b1, w2, b2, freq, w3), consts)


def _hyena(h3d, conv_w, conv_b, kf, hy_bias):
    seq_len = h3d.shape[1]
    consts = _dft_constants(seq_len)
    v, x1, x2 = _short_conv(h3d, conv_w, conv_b)
    z = v
    for n, gate in enumerate((x1, x2)):
        z = _hyena_conv(z, gate, kf, n, hy_bias[n], consts)
    return z


def _layernorm(y, g, b):
    mu = jnp.mean(y, axis=-1, keepdims=True)
    d = y - mu
    var = jnp.mean(d * d, axis=-1, keepdims=True)
    return d * lax.rsqrt(var + LN_EPS) * g + b


def _merge_kernel(g_ref, a_ref, hb_ref, c_ref, x_ref, wa_ref, wb_ref, wc_ref, wo_ref, lg_ref, lb_ref, o_ref):
    d = D_MODEL
    subs = [slice(r, r + MERGE_SUB) for r in range(0, x_ref.shape[0], MERGE_SUB)]
    merged = []
    for r in subs:
        acc = g_ref[r, 0:d].astype(F32) * jnp.dot(a_ref[r, :], wa_ref[...], preferred_element_type=F32)
        acc += g_ref[r, d:2 * d].astype(F32) * jnp.dot(hb_ref[r, :], wb_ref[...], preferred_element_type=F32)
        acc += g_ref[r, 2 * d:3 * d].astype(F32) * jnp.dot(c_ref[r, :], wc_ref[...], preferred_element_type=F32)
        merged.append(acc.astype(BF16))
    for r, mg in zip(subs, merged):
        mix = jnp.dot(mg, wo_ref[...], preferred_element_type=F32)
        o_ref[r, :] = _layernorm(DEEPNORM_ALPHA * x_ref[r, :] + mix, lg_ref[...], lb_ref[...])


def _merge(h2d, a, hb, c, x2d, wa, wb, wc, wo, ln_g, ln_b):
    m = x2d.shape[0]
    tm = MERGE_TM
    row = lambda width: pl.BlockSpec((tm, width), lambda i: (i, 0))
    return pl.pallas_call(
        _merge_kernel,
        grid=(m // tm,),
        in_specs=[
            row(N_BRANCH * D_MODEL), row(NA_WIDTH), row(HY_WIDTH), row(SWA_WIDTH), row(D_MODEL),
            _resident((NA_WIDTH, D_MODEL)), _resident((HY_WIDTH, D_MODEL)), _resident((SWA_WIDTH, D_MODEL)),
            _resident((D_MODEL, D_MODEL)), _resident((1, D_MODEL)), _resident((1, D_MODEL)),
        ],
        out_specs=row(D_MODEL),
        out_shape=jax.ShapeDtypeStruct((m, D_MODEL), F32),
        compiler_params=_cparams("parallel"),
        name="merge_ln",
    )(h2d, a, hb, c, x2d, wa, wb, wc, wo, ln_g, ln_b)


def _mlp_kernel(x_ref, wu_ref, bu_ref, wd_ref, bd_ref, lg_ref, lb_ref, o_ref):
    x = x_ref[...]
    xb = x.astype(BF16)
    acc = DEEPNORM_ALPHA * x + bd_ref[...]
    for c in range(D_FF // MLP_FF_CHUNK):
        sl = slice(c * MLP_FF_CHUNK, (c + 1) * MLP_FF_CHUNK)
        up = jnp.dot(xb, wu_ref[:, sl], preferred_element_type=F32) + bu_ref[:, sl]
        up = jnp.square(jnp.maximum(up, 0.0))
        acc += jnp.dot(up.astype(BF16), wd_ref[sl, :], preferred_element_type=F32)
    o_ref[...] = _layernorm(acc, lg_ref[...], lb_ref[...])


def _mlp(x2d, wu, bu, wd, bd, ln_g, ln_b):
    m = x2d.shape[0]
    tm = MLP_TM
    row = pl.BlockSpec((tm, D_MODEL), lambda i: (i, 0))
    return pl.pallas_call(
        _mlp_kernel,
        grid=(m // tm,),
        in_specs=[
            row, _resident((D_MODEL, D_FF)), _resident((1, D_FF)), _resident((D_FF, D_MODEL)),
            _resident((1, D_MODEL)), _resident((1, D_MODEL)), _resident((1, D_MODEL)),
        ],
        out_specs=row,
        out_shape=jax.ShapeDtypeStruct((m, D_MODEL), F32),
        compiler_params=_cparams("parallel"),
        name="mlp_ln",
    )(x2d, wu, bu, wd, bd, ln_g, ln_b)


def _permute_in_columns(a):
    q0 = OFF_SWA
    order = [h for j in range(SWA_GROUP) for h in (j, j + SWA_GROUP)]
    swa_q = [a[..., q0 + h * HEAD_DIM:q0 + (h + 1) * HEAD_DIM] for h in order]
    swa_k = a[..., q0 + SWA_WIDTH:q0 + SWA_WIDTH + SWA_KV_WIDTH]
    return jnp.concatenate([a[..., OFF_GATE:], a[..., OFF_HY:OFF_SWA], a[..., :2 * NA_WIDTH]] + swa_q + [swa_k],
                           axis=-1)


def _scale_query_columns(a):
    c = HEAD_DIM ** -0.5 * LOG2E
    return jnp.concatenate([a[..., :NA_WIDTH] * c, a[..., NA_WIDTH:OFF_SWA],
                            a[..., OFF_SWA:OFF_SWA + SWA_WIDTH] * c, a[..., OFF_SWA + SWA_WIDTH:]], axis=-1)


def _values_columns(a):
    v_swa = OFF_SWA + SWA_WIDTH + SWA_KV_WIDTH
    return jnp.concatenate([a[..., 2 * NA_WIDTH:3 * NA_WIDTH], a[..., v_swa:v_swa + SWA_KV_WIDTH]], axis=-1)


def _prepare_layer(l, p):
    row = lambda a: a.astype(F32).reshape(1, -1)
    return dict(
        w_in=_permute_in_columns(_scale_query_columns(p["w_in"][l])).astype(BF16),
        b_in=row(_permute_in_columns(_scale_query_columns(p["b_in"][l]))),
        conv_w=p["hy_conv_w"][l], conv_b=p["hy_conv_b"][l], hy_bias=p["hy_bias"][l],
        filt=(p["hy_filt_w1"][l], p["hy_filt_b1"][l], p["hy_filt_w2"][l], p["hy_filt_b2"][l],
              p["hy_filt_freq"][l], p["hy_filt_w3"][l]),
        w_vt=_values_columns(p["w_in"][l]).T.astype(BF16),
        b_vt=_values_columns(p["b_in"][l]).astype(F32).reshape(VT_WIDTH, 1),
        na_bias=_na_bias_table(p["na_rpb"][l]),
        sink=p["swa_sink"][l],
        wa=p["w_branch_a"][l].astype(BF16), wb=p["w_branch_b"][l].astype(BF16),
        wc=p["w_branch_c"][l].astype(BF16), wo=p["w_out"][l].astype(BF16),
        ln1_g=row(p["ln1_g"][l]), ln1_b=row(p["ln1_b"][l]),
        wu=p["w_up"][l].astype(BF16), bu=row(p["b_up"][l]),
        wd=p["w_down"][l].astype(BF16), bd=row(p["b_down"][l]),
        ln2_g=row(p["ln2_g"][l]), ln2_b=row(p["ln2_b"][l]),
    )


def _encoder_block(x, lp, swa_table):
    b_sz, seq_len, _ = x.shape
    m = b_sz * seq_len
    x2d = x.reshape(m, D_MODEL)
    h2d, vt = _inproj(x2d, lp["w_in"], lp["b_in"], lp["w_vt"], lp["b_vt"])
    h3d = h2d.reshape(b_sz, seq_len, D_TOK)
    a = _na_attention(h3d, vt, lp["na_bias"])
    kf = _hyena_spectra(seq_len, *lp["filt"])
    hb = _hyena(h3d, lp["conv_w"], lp["conv_b"], kf, lp["hy_bias"])
    c = _swa_attention(h3d, vt, lp["sink"], swa_table)
    x1 = _merge(h2d, a.reshape(m, NA_WIDTH), hb.reshape(m, HY_WIDTH), c.reshape(m, SWA_WIDTH), x2d,
                lp["wa"], lp["wb"], lp["wc"], lp["wo"], lp["ln1_g"], lp["ln1_b"])
    x2 = _mlp(x1, lp["wu"], lp["bu"], lp["wd"], lp["bd"], lp["ln2_g"], lp["ln2_b"])
    return x2.reshape(b_sz, seq_len, D_MODEL)


def kernel(x_prompt, x_sample, w_in, b_in, hy_conv_w, hy_conv_b, hy_filt_w1, hy_filt_b1, hy_filt_w2,
           hy_filt_b2, hy_filt_freq, hy_filt_w3, hy_bias, na_rpb, swa_sink, w_branch_a, w_branch_b,
           w_branch_c, w_out, ln1_g, ln1_b, w_up, b_up, w_down, b_down, ln2_g, ln2_b):
    params = dict(w_in=w_in, b_in=b_in, hy_conv_w=hy_conv_w, hy_conv_b=hy_conv_b, hy_filt_w1=hy_filt_w1,
                  hy_filt_b1=hy_filt_b1, hy_filt_w2=hy_filt_w2, hy_filt_b2=hy_filt_b2,
                  hy_filt_freq=hy_filt_freq, hy_filt_w3=hy_filt_w3, hy_bias=hy_bias, na_rpb=na_rpb,
                  swa_sink=swa_sink, w_branch_a=w_branch_a, w_branch_b=w_branch_b, w_branch_c=w_branch_c,
                  w_out=w_out, ln1_g=ln1_g, ln1_b=ln1_b, w_up=w_up, b_up=b_up, w_down=w_down,
                  b_down=b_down, ln2_g=ln2_g, ln2_b=ln2_b)
    swa_table = _swa_table()
    y_prompt = x_prompt
    y_sample = x_sample
    for l in range(DEPTH):
        lp = _prepare_layer(l, params)
        y_prompt = _encoder_block(y_prompt, lp, swa_table)
        y_sample = _encoder_block(y_sample, lp, swa_table)
    return (y_prompt, y_sample)
```

```python
import functools
import math

import numpy as np
import jax
import jax.numpy as jnp
from jax import lax
from jax.experimental import pallas as pl
from jax.experimental.pallas import tpu as pltpu

F32 = jnp.float32
BF16 = jnp.bfloat16

D_MODEL = 1024
DEPTH = 2
HEAD_DIM = 64
GRID_W = 64
NA_HEADS = 8
NA_WIN_ROWS = 8
NA_WIN_COLS = 16
NA_WIDTH = NA_HEADS * HEAD_DIM
HY_WIDTH = D_MODEL // 2
HY_ORDER = 2
HY_SHORT_CONV = 3
HY_POS_BANDS = 16
HY_EMB_DIM = 1 + 2 * HY_POS_BANDS
HY_EMB_PAD = 128
HY_FILTER_HIDDEN = 64
HY_FAST_DECAY_PCT = 0.3
HY_SLOW_DECAY_PCT = 1.5
HY_DECAY_TARGET = 1e-2
SWA_HEADS = 8
SWA_KV_HEADS = 2
SWA_GROUP = SWA_HEADS // SWA_KV_HEADS
SWA_WIDTH = SWA_HEADS * HEAD_DIM
SWA_KV_WIDTH = SWA_KV_HEADS * HEAD_DIM
SWA_WINDOW = 128
SWA_BLOCK = 128
N_BRANCH = 3
D_FF = 4 * D_MODEL
OFF_HY = 3 * NA_WIDTH
OFF_SWA = OFF_HY + 3 * HY_WIDTH
OFF_GATE = OFF_SWA + SWA_WIDTH + 2 * SWA_KV_WIDTH
D_IN = OFF_GATE + N_BRANCH * D_MODEL
DEEPNORM_ALPHA = (2 * DEPTH) ** 0.25
LN_EPS = 1e-5
NEG_BIG = -1e30
LOG2E = 1.4426950408889634

P_GATE = 0
P_HY = N_BRANCH * D_MODEL
P_NA = P_HY + 3 * HY_WIDTH
P_SWA = P_NA + 2 * NA_WIDTH
D_TOK = P_SWA + SWA_WIDTH + SWA_KV_WIDTH
VT_WIDTH = NA_WIDTH + SWA_KV_WIDTH

LANE = 128
SUBLANE_PACK = 16
VMEM_LIMIT_BYTES = 56 * 1024 * 1024

NA_PAIR_ROWS = NA_WIN_ROWS + 2
ATTN_BLOCKS_PER_STEP = 2
INPROJ_TM = 512
INPROJ_TN = 768
MERGE_TM = 1024
MERGE_SUB = 256
MLP_TM = 1024
MLP_SUB = 512
MLP_FF_CHUNK = 1024
CONV_TL = 256
FILT_TL = 512
HY_TW_SPLIT = 8


def _cparams(*sem):
    return pltpu.CompilerParams(dimension_semantics=sem, vmem_limit_bytes=VMEM_LIMIT_BYTES)


def _resident(shape):
    nd = len(shape)
    return pl.BlockSpec(shape, lambda *_: (0,) * nd, pipeline_mode=pl.Buffered(1))


def _inproj_kernel(x_ref, w_ref, b_ref, wvt_ref, bvt_ref, o_ref, vt_ref):
    xb = x_ref[...].astype(BF16)
    vt = lax.dot_general(wvt_ref[...], xb, (((1,), (1,)), ((), ())), preferred_element_type=F32)
    vt_ref[...] = (vt + bvt_ref[...]).astype(vt_ref.dtype)
    for c0 in range(0, D_TOK, INPROJ_TN):
        cols = slice(c0, min(c0 + INPROJ_TN, D_TOK))
        acc = jnp.dot(xb, w_ref[:, cols], preferred_element_type=F32) + b_ref[:, cols]
        if cols.stop <= P_HY:
            acc = 1.0 / (1.0 + jnp.exp(-acc))
        o_ref[:, cols] = acc.astype(o_ref.dtype)


def _inproj(x2d, w_bf, b_row, wvt_bf, bvt_col):
    m = x2d.shape[0]
    tm = min(INPROJ_TM, m)
    assert m % tm == 0 and P_HY % INPROJ_TN == 0
    return pl.pallas_call(
        _inproj_kernel,
        grid=(m // tm,),
        in_specs=[
            pl.BlockSpec((tm, D_MODEL), lambda i: (i, 0)),
            _resident((D_MODEL, D_TOK)),
            _resident((1, D_TOK)),
            _resident((VT_WIDTH, D_MODEL)),
            _resident((VT_WIDTH, 1)),
        ],
        out_specs=[pl.BlockSpec((tm, D_TOK), lambda i: (i, 0)),
                   pl.BlockSpec((VT_WIDTH, tm), lambda i: (0, i))],
        out_shape=[jax.ShapeDtypeStruct((m, D_TOK), BF16), jax.ShapeDtypeStruct((VT_WIDTH, m), BF16)],
        compiler_params=_cparams("parallel"),
        name="inproj",
    )(x2d, w_bf, b_row, wvt_bf, bvt_col)


def _na_bias_table(rpb):
    kr = NA_WIN_ROWS
    pad = GRID_W - NA_WIN_COLS
    p = jnp.pad(rpb.astype(F32) * LOG2E, ((0, 0), (0, 0), (pad, pad)))
    cols = jnp.stack([p[:, :, GRID_W - 1 - w:2 * GRID_W - 1 - w] for w in range(GRID_W)], axis=2)
    t = jnp.stack([cols[:, kr - 1 - d:2 * kr - 1 - d] for d in range(kr)], axis=0)
    w = np.arange(GRID_W)[:, None]
    kc = np.arange(GRID_W)[None, :]
    col_start = np.clip(w - NA_WIN_COLS // 2, 0, GRID_W - NA_WIN_COLS)
    valid = (kc >= col_start) & (kc < col_start + NA_WIN_COLS)
    t = jnp.where(valid[None, None, None], t, NEG_BIG)
    t = jnp.transpose(t, (0, 1, 2, 4, 3))
    return t.reshape(kr, NA_HEADS, kr * GRID_W, GRID_W)


def _na_pair_geometry(p, rows):
    base = min(max(2 * p - NA_WIN_ROWS // 2, 0), rows - NA_PAIR_ROWS)
    geo = []
    for j in range(2):
        r = 2 * p + j
        rs = min(max(r - NA_WIN_ROWS // 2, 0), rows - NA_WIN_ROWS)
        assert 0 <= rs - base <= NA_PAIR_ROWS - NA_WIN_ROWS
        geo.append((rs - base, r - rs))
    return base, geo


def _na_pair_table(table, rows):
    n_pairs = rows // 2
    variants = []
    for p in (0, 1, 2, n_pairs - 2, n_pairs - 1):
        _, geo = _na_pair_geometry(p, rows)
        cols = []
        for off, delta in geo:
            after = NA_PAIR_ROWS - NA_WIN_ROWS - off
            cols.append(jnp.pad(table[delta], ((0, 0), (off * GRID_W, after * GRID_W), (0, 0)),
                                constant_values=NEG_BIG))
        variants.append(jnp.concatenate(cols, axis=-1))
    return jnp.stack(variants, axis=0)


def _scores_t_pair(k, q):
    lane = lax.broadcasted_iota(jnp.int32, (1, LANE), 1)
    zero = jnp.zeros((), q.dtype)
    q_both = jnp.concatenate([jnp.where(lane < HEAD_DIM, q, zero), jnp.where(lane >= HEAD_DIM, q, zero)], axis=0)
    s = lax.dot_general(k, q_both, (((1,), (1,)), ((), ())), preferred_element_type=F32)
    return s[:, :q.shape[0]], s[:, q.shape[0]:]


def _weights_t(s, sink=None):
    m = jnp.max(s, axis=0, keepdims=True)
    if sink is None:
        return jnp.exp2(s - m).astype(BF16), None
    m = jnp.maximum(m, sink)
    return jnp.exp2(s - m).astype(BF16), jnp.exp2(sink - m)


def _weighted_values_t(vt, p, sink_term=None):
    ones = jnp.ones((SUBLANE_PACK, vt.shape[1]), vt.dtype)
    acc = jnp.dot(jnp.concatenate([vt, ones], axis=0), p, preferred_element_type=F32)
    total = acc[vt.shape[0]:vt.shape[0] + 1]
    if sink_term is not None:
        total = total + sink_term
    return acc[:vt.shape[0]] / total


def _na_kernel(q_ref, k_ref, vt_ref, *rest, rows, pairs_per_step):
    bias_refs, o_ref = rest[:-1], rest[-1]
    span = NA_PAIR_ROWS * GRID_W
    work = []
    for u in range(pairs_per_step):
        pair = pl.program_id(1) * pairs_per_step + u
        base = jnp.clip(2 * pair - NA_WIN_ROWS // 2, 0, rows - NA_PAIR_ROWS)
        start = pl.multiple_of(base * GRID_W, 2 * GRID_W)
        q = q_ref[0, u * LANE:(u + 1) * LANE, :]
        k = k_ref[0, pl.ds(start, span), :]
        vt = vt_ref[:, pl.ds(start, span)]
        for j in range(NA_HEADS // 2):
            grp = slice(j * LANE, (j + 1) * LANE)
            for h, s in zip((2 * j, 2 * j + 1), _scores_t_pair(k[:, grp], q[:, grp])):
                work.append((vt[h * HEAD_DIM:(h + 1) * HEAD_DIM, :],) + _weights_t(s + bias_refs[u][0, h]))
    outs = [_weighted_values_t(*w) for w in work]
    for u in range(pairs_per_step):
        o_t = jnp.concatenate(outs[u * NA_HEADS:(u + 1) * NA_HEADS], axis=0)
        o_ref[0, u * LANE:(u + 1) * LANE, :] = o_t.T.astype(o_ref.dtype)


def _na_attention(h3d, vt, bias_table):
    b_sz, seq_len, _ = h3d.shape
    rows = seq_len // GRID_W
    n_pairs = rows // 2
    assert rows % 2 == 0 and n_pairs >= 5
    cb = P_NA // NA_WIDTH
    span = NA_PAIR_ROWS * GRID_W
    pair_table = _na_pair_table(bias_table, rows)

    def variant(p):
        return jnp.where(p < 2, p, jnp.where(p > n_pairs - 3, p - (n_pairs - 5), 2))

    pps = ATTN_BLOCKS_PER_STEP
    assert n_pairs % pps == 0 and 2 * GRID_W == LANE
    bias_specs = [pl.BlockSpec((1, NA_HEADS, span, LANE), lambda b, p, u=u: (variant(p * pps + u), 0, 0, 0))
                  for u in range(pps)]
    return pl.pallas_call(
        functools.partial(_na_kernel, rows=rows, pairs_per_step=pps),
        grid=(b_sz, n_pairs // pps),
        in_specs=[
            pl.BlockSpec((1, pps * LANE, NA_WIDTH), lambda b, p: (b, p, cb)),
            pl.BlockSpec((1, seq_len, NA_WIDTH), lambda b, p: (b, 0, cb + 1), pipeline_mode=pl.Buffered(1)),
            pl.BlockSpec((NA_WIDTH, seq_len), lambda b, p: (0, b), pipeline_mode=pl.Buffered(1)),
        ] + bias_specs,
        out_specs=pl.BlockSpec((1, pps * LANE, NA_WIDTH), lambda b, p: (b, p, 0)),
        out_shape=jax.ShapeDtypeStruct((b_sz, seq_len, NA_WIDTH), BF16),
        compiler_params=_cparams("parallel", "arbitrary"),
        name="na_attention",
    )(h3d, h3d, vt, *([pair_table] * pps))


def _swa_table():
    span = SWA_BLOCK + 2 * SWA_WINDOW
    slopes = 2.0 ** (-8.0 * (np.arange(SWA_HEADS, dtype=np.float64) + 1.0) / SWA_HEADS)
    kk = np.arange(span)[:, None]
    t = np.arange(SWA_BLOCK)[None, :]
    variants = []
    for off in (0, SWA_BLOCK, 2 * SWA_BLOCK):
        rel = np.abs(kk - off - t)
        a = -slopes[:, None, None] * rel[None].astype(np.float64) * LOG2E
        variants.append(np.where((rel <= SWA_WINDOW)[None], a, NEG_BIG))
    return jnp.asarray(np.stack(variants), F32)


def _swa_kernel(sink_ref, q_ref, k_ref, vt_ref, *rest, seq_len, blocks_per_step):
    tab_refs, o_ref = rest[:-1], rest[-1]
    span = SWA_BLOCK + 2 * SWA_WINDOW
    work = []
    for u in range(blocks_per_step):
        blk = pl.program_id(1) * blocks_per_step + u
        start = pl.multiple_of(jnp.clip((blk - 1) * SWA_BLOCK, 0, seq_len - span), SWA_BLOCK)
        q = q_ref[0, u * SWA_BLOCK:(u + 1) * SWA_BLOCK, :]
        k = k_ref[0, pl.ds(start, span), :]
        vt = vt_ref[:, pl.ds(start, span)]
        per_head = {}
        for j in range(SWA_GROUP):
            grp = slice(j * LANE, (j + 1) * LANE)
            for g, s in enumerate(_scores_t_pair(k, q[:, grp])):
                h = j + g * SWA_GROUP
                per_head[h] = ((vt[g * HEAD_DIM:(g + 1) * HEAD_DIM, :],)
                               + _weights_t(s + tab_refs[u][0, h], sink_ref[h] * LOG2E))
        work.extend(per_head[h] for h in range(SWA_HEADS))
    outs = [_weighted_values_t(*w) for w in work]
    for u in range(blocks_per_step):
        o_t = jnp.concatenate(outs[u * SWA_HEADS:(u + 1) * SWA_HEADS], axis=0)
        o_ref[0, u * SWA_BLOCK:(u + 1) * SWA_BLOCK, :] = o_t.T.astype(o_ref.dtype)


def _swa_attention(h3d, vt, sink, table):
    b_sz, seq_len, _ = h3d.shape
    nb = seq_len // SWA_BLOCK
    span = SWA_BLOCK + 2 * SWA_WINDOW
    assert SWA_KV_HEADS == 2 and SWA_BLOCK == LANE and nb >= 3
    qb = P_SWA // SWA_WIDTH
    kb = (P_SWA + SWA_WIDTH) // SWA_KV_WIDTH

    def variant(i):
        return jnp.where(i == 0, 0, jnp.where(i == nb - 1, 2, 1))

    bps = ATTN_BLOCKS_PER_STEP
    assert nb % bps == 0
    tab_specs = [pl.BlockSpec((1, SWA_HEADS, span, SWA_BLOCK), lambda b, i, u=u: (variant(i * bps + u), 0, 0, 0))
                 for u in range(bps)]
    return pl.pallas_call(
        functools.partial(_swa_kernel, seq_len=seq_len, blocks_per_step=bps),
        grid=(b_sz, nb // bps),
        in_specs=[
            pl.BlockSpec(memory_space=pltpu.SMEM),
            pl.BlockSpec((1, bps * SWA_BLOCK, SWA_WIDTH), lambda b, i: (b, i, qb)),
            pl.BlockSpec((1, seq_len, SWA_KV_WIDTH), lambda b, i: (b, 0, kb)),
            pl.BlockSpec((SWA_KV_WIDTH, seq_len), lambda b, i: (NA_WIDTH // SWA_KV_WIDTH, b)),
        ] + tab_specs,
        out_specs=pl.BlockSpec((1, bps * SWA_BLOCK, SWA_WIDTH), lambda b, i: (b, i, 0)),
        out_shape=jax.ShapeDtypeStruct((b_sz, seq_len, SWA_WIDTH), BF16),
        compiler_params=_cparams("parallel", "arbitrary"),
        name="swa_attention",
    )(sink.astype(F32), h3d, h3d, vt, *([table] * bps))


def _conv_kernel(x_ref, prev_ref, next_ref, w_ref, b_ref, v_ref, x1_ref, x2_ref, *, n_tiles):
    i = pl.program_id(1)
    tl = x_ref.shape[1]
    halo = prev_ref.shape[1]
    row = lax.broadcasted_iota(jnp.int32, (tl, 1), 0)
    outs = (v_ref, x1_ref, x2_ref)
    for c in range(3):
        sl = slice(c * HY_WIDTH, (c + 1) * HY_WIDTH)
        x = x_ref[0, :, sl].astype(F32)
        prev_row = jnp.where(i > 0, prev_ref[0, halo - 1:halo, sl].astype(F32), 0.0)
        next_row = jnp.where(i < n_tiles - 1, next_ref[0, 0:1, sl].astype(F32), 0.0)
        xm = jnp.where(row == 0, prev_row, pltpu.roll(x, 1, axis=0))
        xp = jnp.where(row == tl - 1, next_row, pltpu.roll(x, tl - 1, axis=0))
        u = xm * w_ref[0:1, sl] + x * w_ref[1:2, sl] + xp * w_ref[2:3, sl] + b_ref[:, sl]
        outs[c][0] = u.astype(BF16)


def _short_conv(h3d, conv_w, conv_b):
    b_sz, seq_len, _ = h3d.shape
    tl = CONV_TL
    halo = 16
    n_tiles = seq_len // tl
    width = 3 * HY_WIDTH
    cb = P_HY // width
    assert P_HY % width == 0 and seq_len % tl == 0
    per = tl // halo
    out = jax.ShapeDtypeStruct((b_sz, seq_len, HY_WIDTH), BF16)
    ospec = pl.BlockSpec((1, tl, HY_WIDTH), lambda b, i: (b, i, 0))
    return pl.pallas_call(
        functools.partial(_conv_kernel, n_tiles=n_tiles),
        grid=(b_sz, n_tiles),
        in_specs=[
            pl.BlockSpec((1, tl, width), lambda b, i: (b, i, cb)),
            pl.BlockSpec((1, halo, width), lambda b, i: (b, jnp.maximum(i * per - 1, 0), cb)),
            pl.BlockSpec((1, halo, width), lambda b, i: (b, jnp.minimum((i + 1) * per, seq_len // halo - 1), cb)),
            _resident((HY_SHORT_CONV, width)),
            _resident((1, width)),
        ],
        out_specs=[ospec, ospec, ospec],
        out_shape=[out, out, out],
        compiler_params=_cparams("parallel", "arbitrary"),
        name="hyena_short_conv",
    )(h3d, h3d, h3d, conv_w.astype(F32), conv_b.astype(F32).reshape(1, width))


def _filter_embedding(seq_len):
    t = np.linspace(0.0, 1.0, seq_len, dtype=np.float32).astype(np.float64)[:, None]
    w = (2.0 * math.pi * np.arange(seq_len, dtype=np.float32) / seq_len).astype(np.float32)
    bands = np.linspace(1e-4, HY_POS_BANDS - 1, HY_POS_BANDS, dtype=np.float32)
    ang = (w[:, None] * bands[None, :]).astype(np.float32).astype(np.float64)
    z = np.concatenate([t, np.cos(ang), -np.sin(ang)], axis=-1)
    zp = np.zeros((2 * seq_len, HY_EMB_PAD), np.float32)
    zp[:seq_len, :HY_EMB_DIM] = z
    zp[seq_len + 1:, :HY_EMB_DIM] = z[:0:-1]
    return zp


def _filter_deltas():
    min_decay = math.log(HY_DECAY_TARGET) / HY_SLOW_DECAY_PCT
    max_decay = math.log(HY_DECAY_TARGET) / HY_FAST_DECAY_PCT
    return np.abs(np.linspace(min_decay, max_decay, HY_WIDTH, dtype=np.float32))[None, :]


def _filter_kernel(z_ref, w1_ref, b1_ref, w2_ref, b2_ref, fr_ref, w3_ref, dl_ref, o_ref, *, seq_len):
    hi = lax.Precision.HIGHEST
    tl = z_ref.shape[0]
    z = z_ref[...]
    fr = fr_ref[...]
    h = jnp.sin(fr * (jnp.dot(z, w1_ref[...], precision=hi, preferred_element_type=F32) + b1_ref[...]))
    h = jnp.sin(fr * (jnp.dot(h, w2_ref[...], precision=hi, preferred_element_type=F32) + b2_ref[...]))
    t = z[:, 0:1]
    window = jnp.exp(-t * dl_ref[...])
    first_row = pl.program_id(0) * tl
    row = first_row + lax.broadcasted_iota(jnp.int32, (tl, 1), 0)

    def emit(direction):
        for o in range(HY_ORDER):
            c0 = (2 * o + direction) * HY_WIDTH
            f = jnp.dot(h, w3_ref[:, c0:c0 + HY_WIDTH], precision=hi, preferred_element_type=F32) * window
            if direction == 1:
                f = jnp.where(row > seq_len, f, 0.0)
            o_ref[o] = f.astype(o_ref.dtype)

    pl.when(first_row < seq_len)(lambda: emit(0))
    pl.when(first_row >= seq_len)(lambda: emit(1))


def _hyena_kernels(seq_len, w1, b1, w2, b2, freq, w3):
    tl = min(FILT_TL, seq_len)
    assert seq_len % tl == 0
    z = jnp.asarray(_filter_embedding(seq_len))
    w1p = jnp.zeros((HY_EMB_PAD, HY_FILTER_HIDDEN), F32).at[:HY_EMB_DIM].set(w1.astype(F32))
    hid = HY_FILTER_HIDDEN
    return pl.pallas_call(
        functools.partial(_filter_kernel, seq_len=seq_len),
        grid=(2 * seq_len // tl,),
        in_specs=[
            pl.BlockSpec((tl, HY_EMB_PAD), lambda i: (i, 0)),
            _resident((HY_EMB_PAD, hid)), _resident((1, hid)),
            _resident((hid, hid)), _resident((1, hid)), _resident((1, hid)),
            _resident((hid, 2 * HY_ORDER * HY_WIDTH)), _resident((1, HY_WIDTH)),
        ],
        out_specs=pl.BlockSpec((HY_ORDER, tl, HY_WIDTH), lambda i: (0, i, 0)),
        out_shape=jax.ShapeDtypeStruct((HY_ORDER, 2 * seq_len, HY_WIDTH), BF16),
        compiler_params=_cparams("parallel"),
        name="hyena_filters",
    )(z, w1p, b1.astype(F32).reshape(1, hid), w2.astype(F32), b2.astype(F32).reshape(1, hid),
      freq.astype(F32).reshape(1, hid), w3.astype(F32), jnp.asarray(_filter_deltas()))


def _fft_dims(seq_len):
    n = 2 * seq_len
    n1 = 1 << ((n.bit_length() - 1) // 2)
    n2 = n // n1
    assert n1 * n2 == n and n1 == n2, "sequence length must give a square transform"
    return n1, n2


def _stack_complex(fr, fi):
    return np.block([[fr, -fi], [fi, fr]])


@functools.lru_cache(maxsize=None)
def _dft_constants(seq_len):
    n1, n2 = _fft_dims(seq_len)
    n = n1 * n2
    k1 = np.arange(n1)[:, None].astype(np.float64)
    t1 = np.arange(n1 // 2)[None, :].astype(np.float64)
    a1 = -2.0 * math.pi * k1 * t1 / n1
    f1r, f1i = np.cos(a1), np.sin(a1)
    w1_complex = _stack_complex(f1r, f1i)
    a1_full = -2.0 * math.pi * k1 * np.arange(n1)[None, :].astype(np.float64) / n1
    w1_real = np.concatenate([np.cos(a1_full), np.sin(a1_full)], axis=0)
    k2 = np.arange(n2)[:, None].astype(np.float64)
    t2 = np.arange(n2)[None, :].astype(np.float64)
    a2 = -2.0 * math.pi * k2 * t2 / n2
    f2r, f2i = np.cos(a2), np.sin(a2)
    g1r, g1i = f1r.T / n, -f1i.T / n
    w3 = _stack_complex(g1r, g1i)
    kb = np.arange(HY_TW_SPLIT)[:, None].astype(np.float64)
    ab = -2.0 * math.pi * kb * t2 / n
    tbr, tbi = np.cos(ab), np.sin(ab)
    w2f = np.stack([_stack_complex(f2r * tbr[b] - f2i * tbi[b], f2r * tbi[b] + f2i * tbr[b])
                    for b in range(HY_TW_SPLIT)])
    w2i = np.stack([_stack_complex(tbr[b][:, None] * f2r - tbi[b][:, None] * f2i,
                                   -(tbr[b][:, None] * f2i + tbi[b][:, None] * f2r))
                    for b in range(HY_TW_SPLIT)])
    ka = (np.arange(n1 // HY_TW_SPLIT) * HY_TW_SPLIT)[:, None].astype(np.float64)
    aa = -2.0 * math.pi * ka * t2 / n
    ta = np.stack([np.cos(aa), np.sin(aa)], axis=0)[..., None]
    ta = np.broadcast_to(ta, (2, n1 // HY_TW_SPLIT, n2, LANE)).astype(np.float32)
    as_bf = lambda a: jnp.asarray(a, F32).astype(BF16)
    return dict(w1_complex=as_bf(w1_complex), w1_real=as_bf(w1_real), w2f=as_bf(w2f), w2i=as_bf(w2i),
                ta=jnp.asarray(ta), w3=as_bf(w3))


def _pack_pair(a, b):
    return pltpu.pack_elementwise([a, b], packed_dtype=BF16)


def _unpack_pair(w):
    return (pltpu.unpack_elementwise(w, index=0, packed_dtype=BF16, unpacked_dtype=F32),
            pltpu.unpack_elementwise(w, index=1, packed_dtype=BF16, unpacked_dtype=F32))


def _conv_pitch(n2):
    return n2 + 8


def _hyena_conv_kernel(z_ref, g_ref, kf_ref, w1_ref, w2f_ref, w2i_ref, w3_ref, ta_ref, bias_ref, o_ref,
                       x_scr, a_scr, *, n1, n2):
    pitch = _conv_pitch(n2)
    half = n1 // 2
    groups = n1 // HY_TW_SPLIT

    def pack_in(t1, carry):
        rows = pl.ds(pl.multiple_of(t1 * n2, n2), n2)
        x_scr[pl.ds(pl.multiple_of(t1 * pitch, 8), n2), :] = _pack_pair(
            z_ref[0, rows, :].astype(F32), z_ref[1, rows, :].astype(F32))
        return carry

    lax.fori_loop(0, half, pack_in, 0)

    def stage1(j, carry):
        cols = []
        for u in range(2):
            xr, xi = _unpack_pair(x_scr[pl.ds(2 * j + u, half, stride=pitch), :])
            cols.append(jnp.concatenate([xr, xi], axis=0).astype(BF16))
        out = jnp.dot(w1_ref[...], jnp.concatenate(cols, axis=1), preferred_element_type=F32)
        for u in range(2):
            o = out[:, u * LANE:(u + 1) * LANE]
            a_scr[pl.ds(2 * j + u, n1, stride=pitch), :] = _pack_pair(o[:n1], o[n1:])
        return carry

    lax.fori_loop(0, n2 // 2, stage1, 0, unroll=4)

    for kb in range(HY_TW_SPLIT):
        def stage2(jp, carry, kb=kb):
            rows, tws, stacks = [], [], []
            for u in range(2):
                ka = 2 * jp + u
                k1 = ka * HY_TW_SPLIT + kb
                rows.append(pl.ds(pl.multiple_of(k1 * pitch, 8), n2))
                ar, ai = _unpack_pair(a_scr[rows[u], :])
                twr, twi = ta_ref[0, ka], ta_ref[1, ka]
                tws.append((twr, twi))
                stacks.append(jnp.concatenate([ar * twr - ai * twi, ar * twi + ai * twr], axis=0).astype(BF16))
            spec = jnp.dot(w2f_ref[kb], jnp.concatenate(stacks, axis=1), preferred_element_type=F32)
            prods = []
            for u in range(2):
                k1 = (2 * jp + u) * HY_TW_SPLIT + kb
                sr, si = spec[:n2, u * LANE:(u + 1) * LANE], spec[n2:, u * LANE:(u + 1) * LANE]
                kr = kf_ref[0, 0, k1].astype(F32)
                ki = kf_ref[0, 1, k1].astype(F32)
                prods.append(jnp.concatenate([sr * kr - si * ki, sr * ki + si * kr], axis=0).astype(BF16))
            back = jnp.dot(w2i_ref[kb], jnp.concatenate(prods, axis=1), preferred_element_type=F32)
            for u in range(2):
                br, bi = back[:n2, u * LANE:(u + 1) * LANE], back[n2:, u * LANE:(u + 1) * LANE]
                twr, twi = tws[u]
                a_scr[rows[u], :] = _pack_pair(br * twr + bi * twi, bi * twr - br * twi)
            return carry

        lax.fori_loop(0, groups // 2, stage2, 0, unroll=4)

    def stage3(j, carry):
        cols = []
        for u in range(2):
            br, bi = _unpack_pair(a_scr[pl.ds(2 * j + u, n1, stride=pitch), :])
            cols.append(jnp.concatenate([br, bi], axis=0).astype(BF16))
        y = jnp.dot(w3_ref[...], jnp.concatenate(cols, axis=1), preferred_element_type=F32)
        for u in range(2):
            sl = pl.ds(2 * j + u, half, stride=pitch)
            za, zb = _unpack_pair(x_scr[sl, :])
            yu = y[:, u * LANE:(u + 1) * LANE]
            x_scr[sl, :] = _pack_pair(yu[:half] + bias_ref[...] * za, yu[half:] + bias_ref[...] * zb)
        return carry

    lax.fori_loop(0, n2 // 2, stage3, 0, unroll=4)

    def gate_out(t1, carry):
        rows = pl.ds(pl.multiple_of(t1 * n2, n2), n2)
        ta, tb = _unpack_pair(x_scr[pl.ds(pl.multiple_of(t1 * pitch, 8), n2), :])
        o_ref[0, rows, :] = (g_ref[0, rows, :].astype(F32) * ta).astype(o_ref.dtype)
        o_ref[1, rows, :] = (g_ref[1, rows, :].astype(F32) * tb).astype(o_ref.dtype)
        return carry

    lax.fori_loop(0, half, gate_out, 0)


def _hyena_conv(z, gate, kf, order, bias, consts):
    s, seq_len, c = z.shape
    n1, n2 = _fft_dims(seq_len)
    pitch = _conv_pitch(n2)
    assert s % 2 == 0 and c % LANE == 0 and (n1 // HY_TW_SPLIT) % 2 == 0
    seq_blk = pl.BlockSpec((2, seq_len, LANE), lambda cc, p: (p, 0, cc))
    return pl.pallas_call(
        functools.partial(_hyena_conv_kernel, n1=n1, n2=n2),
        grid=(c // LANE, s // 2),
        in_specs=[
            seq_blk,
            seq_blk,
            pl.BlockSpec((1, 2, n1, n2, LANE), lambda cc, p: (order, 0, 0, 0, cc), pipeline_mode=pl.Buffered(1)),
            _resident(consts["w1_complex"].shape),
            _resident(consts["w2f"].shape),
            _resident(consts["w2i"].shape),
            _resident(consts["w3"].shape),
            _resident(consts["ta"].shape),
            pl.BlockSpec((1, LANE), lambda cc, p: (0, cc)),
        ],
        out_specs=seq_blk,
        out_shape=jax.ShapeDtypeStruct(z.shape, BF16),
        scratch_shapes=[pltpu.VMEM((n1 // 2 * pitch, LANE), jnp.uint32),
                        pltpu.VMEM((n1 * pitch, LANE), jnp.uint32)],
        compiler_params=_cparams("parallel", "arbitrary"),
        name="hyena_conv",
    )(z, gate, kf, consts["w1_complex"], consts["w2f"], consts["w2i"], consts["w3"], consts["ta"],
      bias.astype(F32).reshape(1, c))


def _spectrum_kernel(k_ref, w1_ref, w2f_ref, ta_ref, o_ref, x_scr, a_scr, *, n1, n2):
    pitch = _conv_pitch(n2)
    groups = n1 // HY_TW_SPLIT

    def pack_in(t1, carry):
        x_scr[pl.ds(pl.multiple_of(t1 * pitch, 8), n2), :] = (
            k_ref[0, pl.ds(pl.multiple_of(t1 * n2, n2), n2), :].astype(F32))
        return carry

    lax.fori_loop(0, n1, pack_in, 0)

    def stage1(j, carry):
        cols = [x_scr[pl.ds(2 * j + u, n1, stride=pitch), :].astype(BF16) for u in range(2)]
        out = jnp.dot(w1_ref[...], jnp.concatenate(cols, axis=1), preferred_element_type=F32)
        for u in range(2):
            o = out[:, u * LANE:(u + 1) * LANE]
            a_scr[pl.ds(2 * j + u, n1, stride=pitch), :] = _pack_pair(o[:n1], o[n1:])
        return carry

    lax.fori_loop(0, n2 // 2, stage1, 0, unroll=4)

    for kb in range(HY_TW_SPLIT):
        def stage2(jp, carry, kb=kb):
            stacks = []
            for u in range(2):
                ka = 2 * jp + u
                k1 = ka * HY_TW_SPLIT + kb
                ar, ai = _unpack_pair(a_scr[pl.ds(pl.multiple_of(k1 * pitch, 8), n2), :])
                twr, twi = ta_ref[0, ka], ta_ref[1, ka]
                stacks.append(jnp.concatenate([ar * twr - ai * twi, ar * twi + ai * twr], axis=0).astype(BF16))
            spec = jnp.dot(w2f_ref[kb], jnp.concatenate(stacks, axis=1), preferred_element_type=F32)
            for u in range(2):
                k1 = (2 * jp + u) * HY_TW_SPLIT + kb
                o_ref[0, 0, k1] = spec[:n2, u * LANE:(u + 1) * LANE].astype(o_ref.dtype)
                o_ref[0, 1, k1] = spec[n2:, u * LANE:(u + 1) * LANE].astype(o_ref.dtype)
            return carry

        lax.fori_loop(0, groups // 2, stage2, 0, unroll=2)


def _filter_spectrum(kern, consts):
    _, n, c = kern.shape
    n1, n2 = _fft_dims(n // 2)
    pitch = _conv_pitch(n2)
    return pl.pallas_call(
        functools.partial(_spectrum_kernel, n1=n1, n2=n2),
        grid=(HY_ORDER, c // LANE),
        in_specs=[
            pl.BlockSpec((1, n, LANE), lambda o, cc: (o, 0, cc), pipeline_mode=pl.Buffered(1)),
            _resident(consts["w1_real"].shape),
            _resident(consts["w2f"].shape),
            _resident(consts["ta"].shape),
        ],
        out_specs=pl.BlockSpec((1, 2, n1, n2, LANE), lambda o, cc: (o, 0, 0, 0, cc)),
        out_shape=jax.ShapeDtypeStruct((HY_ORDER, 2, n1, n2, c), BF16),
        scratch_shapes=[pltpu.VMEM((n1 * pitch, LANE), F32), pltpu.VMEM((n1 * pitch, LANE), jnp.uint32)],
        compiler_params=_cparams("parallel", "arbitrary"),
        name="hyena_filter_spectrum",
    )(kern, consts["w1_real"], consts["w2f"], consts["ta"])


def _hyena_spectra(seq_len, w1, b1, w2, b2, freq, w3):
    consts = _dft_constants(seq_len)
    return _filter_spectrum(_hyena_kernels(seq_len, w1, b1, w2, b2, freq, w3), consts)


def _hyena(h3d, conv_w, conv_b, kf, hy_bias):
    seq_len = h3d.shape[1]
    consts = _dft_constants(seq_len)
    v, x1, x2 = _short_conv(h3d, conv_w, conv_b)
    z = v
    for n, gate in enumerate((x1, x2)):
        z = _hyena_conv(z, gate, kf, n, hy_bias[n], consts)
    return z


def _layernorm(y, g, b):
    mu = jnp.mean(y, axis=-1, keepdims=True)
    d = y - mu
    var = jnp.mean(d * d, axis=-1, keepdims=True)
    return d * lax.rsqrt(var + LN_EPS) * g + b


def _merge_kernel(g_ref, a_ref, hb_ref, c_ref, x_ref, wa_ref, wb_ref, wc_ref, wo_ref, lg_ref, lb_ref, o_ref):
    d = D_MODEL
    subs = [slice(r, r + MERGE_SUB) for r in range(0, x_ref.shape[0], MERGE_SUB)]
    merged = []
    for r in subs:
        acc = g_ref[r, 0:d].astype(F32) * jnp.dot(a_ref[r, :], wa_ref[...], preferred_element_type=F32)
        acc += g_ref[r, d:2 * d].astype(F32) * jnp.dot(hb_ref[r, :], wb_ref[...], preferred_element_type=F32)
        acc += g_ref[r, 2 * d:3 * d].astype(F32) * jnp.dot(c_ref[r, :], wc_ref[...], preferred_element_type=F32)
        merged.append(acc.astype(BF16))
    for r, mg in zip(subs, merged):
        mix = jnp.dot(mg, wo_ref[...], preferred_element_type=F32)
        o_ref[r, :] = _layernorm(DEEPNORM_ALPHA * x_ref[r, :] + mix, lg_ref[...], lb_ref[...])


def _merge(h2d, a, hb, c, x2d, wa, wb, wc, wo, ln_g, ln_b):
    m = x2d.shape[0]
    tm = MERGE_TM
    row = lambda width: pl.BlockSpec((tm, width), lambda i: (i, 0))
    return pl.pallas_call(
        _merge_kernel,
        grid=(m // tm,),
        in_specs=[
            row(N_BRANCH * D_MODEL), row(NA_WIDTH), row(HY_WIDTH), row(SWA_WIDTH), row(D_MODEL),
            _resident((NA_WIDTH, D_MODEL)), _resident((HY_WIDTH, D_MODEL)), _resident((SWA_WIDTH, D_MODEL)),
            _resident((D_MODEL, D_MODEL)), _resident((1, D_MODEL)), _resident((1, D_MODEL)),
        ],
        out_specs=row(D_MODEL),
        out_shape=jax.ShapeDtypeStruct((m, D_MODEL), F32),
        compiler_params=_cparams("parallel"),
        name="merge_ln",
    )(h2d, a, hb, c, x2d, wa, wb, wc, wo, ln_g, ln_b)


def _mlp_kernel(x_ref, wu_ref, bu_ref, wd_ref, bd_ref, lg_ref, lb_ref, o_ref):
    for r0 in range(0, x_ref.shape[0], MLP_SUB):
        rows = slice(r0, r0 + MLP_SUB)
        x = x_ref[rows, :]
        xb = x.astype(BF16)
        acc = DEEPNORM_ALPHA * x + bd_ref[...]
        for c in range(D_FF // MLP_FF_CHUNK):
            sl = slice(c * MLP_FF_CHUNK, (c + 1) * MLP_FF_CHUNK)
            up = jnp.dot(xb, wu_ref[:, sl], preferred_element_type=F32) + bu_ref[:, sl]
            up = jnp.square(jnp.maximum(up, 0.0))
            acc += jnp.dot(up.astype(BF16), wd_ref[sl, :], preferred_element_type=F32)
        o_ref[rows, :] = _layernorm(acc, lg_ref[...], lb_ref[...])


def _mlp(x2d, wu, bu, wd, bd, ln_g, ln_b):
    m = x2d.shape[0]
    tm = MLP_TM
    row = pl.BlockSpec((tm, D_MODEL), lambda i: (i, 0))
    return pl.pallas_call(
        _mlp_kernel,
        grid=(m // tm,),
        in_specs=[
            row, _resident((D_MODEL, D_FF)), _resident((1, D_FF)), _resident((D_FF, D_MODEL)),
            _resident((1, D_MODEL)), _resident((1, D_MODEL)), _resident((1, D_MODEL)),
        ],
        out_specs=row,
        out_shape=jax.ShapeDtypeStruct((m, D_MODEL), F32),
        compiler_params=_cparams("parallel"),
        name="mlp_ln",
    )(x2d, wu, bu, wd, bd, ln_g, ln_b)


def _permute_in_columns(a):
    q0 = OFF_SWA
    order = [h for j in range(SWA_GROUP) for h in (j, j + SWA_GROUP)]
    swa_q = [a[..., q0 + h * HEAD_DIM:q0 + (h + 1) * HEAD_DIM] for h in order]
    swa_k = a[..., q0 + SWA_WIDTH:q0 + SWA_WIDTH + SWA_KV_WIDTH]
    return jnp.concatenate([a[..., OFF_GATE:], a[..., OFF_HY:OFF_SWA], a[..., :2 * NA_WIDTH]] + swa_q + [swa_k],
                           axis=-1)


def _scale_query_columns(a):
    c = HEAD_DIM ** -0.5 * LOG2E
    return jnp.concatenate([a[..., :NA_WIDTH] * c, a[..., NA_WIDTH:OFF_SWA],
                            a[..., OFF_SWA:OFF_SWA + SWA_WIDTH] * c, a[..., OFF_SWA + SWA_WIDTH:]], axis=-1)


def _values_columns(a):
    v_swa = OFF_SWA + SWA_WIDTH + SWA_KV_WIDTH
    return jnp.concatenate([a[..., 2 * NA_WIDTH:3 * NA_WIDTH], a[..., v_swa:v_swa + SWA_KV_WIDTH]], axis=-1)


def _prepare_layer(l, p):
    row = lambda a: a.astype(F32).reshape(1, -1)
    return dict(
        w_in=_permute_in_columns(_scale_query_columns(p["w_in"][l])).astype(BF16),
        b_in=row(_permute_in_columns(_scale_query_columns(p["b_in"][l]))),
        conv_w=p["hy_conv_w"][l], conv_b=p["hy_conv_b"][l], hy_bias=p["hy_bias"][l],
        filt=(p["hy_filt_w1"][l], p["hy_filt_b1"][l], p["hy_filt_w2"][l], p["hy_filt_b2"][l],
              p["hy_filt_freq"][l], p["hy_filt_w3"][l]),
        w_vt=_values_columns(p["w_in"][l]).T.astype(BF16),
        b_vt=_values_columns(p["b_in"][l]).astype(F32).reshape(VT_WIDTH, 1),
        na_bias=_na_bias_table(p["na_rpb"][l]),
        sink=p["swa_sink"][l],
        wa=p["w_branch_a"][l].astype(BF16), wb=p["w_branch_b"][l].astype(BF16),
        wc=p["w_branch_c"][l].astype(BF16), wo=p["w_out"][l].astype(BF16),
        ln1_g=row(p["ln1_g"][l]), ln1_b=row(p["ln1_b"][l]),
        wu=p["w_up"][l].astype(BF16), bu=row(p["b_up"][l]),
        wd=p["w_down"][l].astype(BF16), bd=row(p["b_down"][l]),
        ln2_g=row(p["ln2_g"][l]), ln2_b=row(p["ln2_b"][l]),
    )


def _encoder_block(x, lp, swa_table):
    b_sz, seq_len, _ = x.shape
    m = b_sz * seq_len
    x2d = x.reshape(m, D_MODEL)
    h2d, vt = _inproj(x2d, lp["w_in"], lp["b_in"], lp["w_vt"], lp["b_vt"])
    h3d = h2d.reshape(b_sz, seq_len, D_TOK)
    a = _na_attention(h3d, vt, lp["na_bias"])
    kf = _hyena_spectra(seq_len, *lp["filt"])
    hb = _hyena(h3d, lp["conv_w"], lp["conv_b"], kf, lp["hy_bias"])
    c = _swa_attention(h3d, vt, lp["sink"], swa_table)
    x1 = _merge(h2d, a.reshape(m, NA_WIDTH), hb.reshape(m, HY_WIDTH), c.reshape(m, SWA_WIDTH), x2d,
                lp["wa"], lp["wb"], lp["wc"], lp["wo"], lp["ln1_g"], lp["ln1_b"])
    x2 = _mlp(x1, lp["wu"], lp["bu"], lp["wd"], lp["bd"], lp["ln2_g"], lp["ln2_b"])
    return x2.reshape(b_sz, seq_len, D_MODEL)


def kernel(x_prompt, x_sample, w_in, b_in, hy_conv_w, hy_conv_b, hy_filt_w1, hy_filt_b1, hy_filt_w2,
           hy_filt_b2, hy_filt_freq, hy_filt_w3, hy_bias, na_rpb, swa_sink, w_branch_a, w_branch_b,
           w_branch_c, w_out, ln1_g, ln1_b, w_up, b_up, w_down, b_down, ln2_g, ln2_b):
    params = dict(w_in=w_in, b_in=b_in, hy_conv_w=hy_conv_w, hy_conv_b=hy_conv_b, hy_filt_w1=hy_filt_w1,
                  hy_filt_b1=hy_filt_b1, hy_filt_w2=hy_filt_w2, hy_filt_b2=hy_filt_b2,
                  hy_filt_freq=hy_filt_freq, hy_filt_w3=hy_filt_w3, hy_bias=hy_bias, na_rpb=na_rpb,
                  swa_sink=swa_sink, w_branch_a=w_branch_a, w_branch_b=w_branch_b, w_branch_c=w_branch_c,
                  w_out=w_out, ln1_g=ln1_g, ln1_b=ln1_b, w_up=w_up, b_up=b_up, w_down=w_down,
                  b_down=b_down, ln2_g=ln2_g, ln2_b=ln2_b)
    swa_table = _swa_table()
    y_prompt = x_prompt
    y_sample = x_sample
    for l in range(DEPTH):
        lp = _prepare_layer(l, params)
        y_prompt = _encoder_block(y_prompt, lp, swa_table)
        y_sample = _encoder_block(y_sample, lp, swa_table)
    return (y_prompt, y_sample)
```

```python
import functools
import math

import numpy as np
import jax
import jax.numpy as jnp
from jax import lax
from jax.experimental import pallas as pl
from jax.experimental.pallas import tpu as pltpu

F32 = jnp.float32
BF16 = jnp.bfloat16

D_MODEL = 1024
DEPTH = 2
HEAD_DIM = 64
GRID_W = 64
NA_HEADS = 8
NA_WIN_ROWS = 8
NA_WIN_COLS = 16
NA_WIDTH = NA_HEADS * HEAD_DIM
HY_WIDTH = D_MODEL // 2
HY_ORDER = 2
HY_SHORT_CONV = 3
HY_POS_BANDS = 16
HY_EMB_DIM = 1 + 2 * HY_POS_BANDS
HY_EMB_PAD = 128
HY_FILTER_HIDDEN = 64
HY_FAST_DECAY_PCT = 0.3
HY_SLOW_DECAY_PCT = 1.5
HY_DECAY_TARGET = 1e-2
SWA_HEADS = 8
SWA_KV_HEADS = 2
SWA_GROUP = SWA_HEADS // SWA_KV_HEADS
SWA_WIDTH = SWA_HEADS * HEAD_DIM
SWA_KV_WIDTH = SWA_KV_HEADS * HEAD_DIM
SWA_WINDOW = 128
SWA_BLOCK = 128
N_BRANCH = 3
D_FF = 4 * D_MODEL
OFF_HY = 3 * NA_WIDTH
OFF_SWA = OFF_HY + 3 * HY_WIDTH
OFF_GATE = OFF_SWA + SWA_WIDTH + 2 * SWA_KV_WIDTH
D_IN = OFF_GATE + N_BRANCH * D_MODEL
DEEPNORM_ALPHA = (2 * DEPTH) ** 0.25
LN_EPS = 1e-5
NEG_BIG = -1e30
LOG2E = 1.4426950408889634

P_GATE = 0
P_HY = N_BRANCH * D_MODEL
P_NA = P_HY + 3 * HY_WIDTH
P_SWA = P_NA + 2 * NA_WIDTH
D_TOK = P_SWA + SWA_WIDTH + SWA_KV_WIDTH
VT_WIDTH = NA_WIDTH + SWA_KV_WIDTH

LANE = 128
SUBLANE_PACK = 16
VMEM_LIMIT_BYTES = 56 * 1024 * 1024

NA_PAIR_ROWS = NA_WIN_ROWS + 2
ATTN_BLOCKS_PER_STEP = 2
INPROJ_TM = 512
INPROJ_TN = 768
MERGE_TM = 1024
MERGE_SUB = 256
MLP_TM = 1024
MLP_SUB = 512
MLP_FF_CHUNK = 1024
CONV_TL = 512
CONV_SUB = 128
FILT_TL = 512
HY_TW_SPLIT = 8


def _cparams(*sem):
    return pltpu.CompilerParams(dimension_semantics=sem, vmem_limit_bytes=VMEM_LIMIT_BYTES)


def _resident(shape):
    nd = len(shape)
    return pl.BlockSpec(shape, lambda *_: (0,) * nd, pipeline_mode=pl.Buffered(1))


def _inproj_kernel(x_ref, w_ref, b_ref, wvt_ref, bvt_ref, o_ref, vt_ref):
    xb = x_ref[...].astype(BF16)
    vt = lax.dot_general(wvt_ref[...], xb, (((1,), (1,)), ((), ())), preferred_element_type=F32)
    vt_ref[...] = (vt + bvt_ref[...]).astype(vt_ref.dtype)
    for c0 in range(0, D_TOK, INPROJ_TN):
        cols = slice(c0, min(c0 + INPROJ_TN, D_TOK))
        acc = jnp.dot(xb, w_ref[:, cols], preferred_element_type=F32) + b_ref[:, cols]
        if cols.stop <= P_HY:
            acc = 1.0 / (1.0 + jnp.exp(-acc))
        o_ref[:, cols] = acc.astype(o_ref.dtype)


def _inproj(x2d, w_bf, b_row, wvt_bf, bvt_col):
    m = x2d.shape[0]
    tm = min(INPROJ_TM, m)
    assert m % tm == 0 and P_HY % INPROJ_TN == 0
    return pl.pallas_call(
        _inproj_kernel,
        grid=(m // tm,),
        in_specs=[
            pl.BlockSpec((tm, D_MODEL), lambda i: (i, 0)),
            _resident((D_MODEL, D_TOK)),
            _resident((1, D_TOK)),
            _resident((VT_WIDTH, D_MODEL)),
            _resident((VT_WIDTH, 1)),
        ],
        out_specs=[pl.BlockSpec((tm, D_TOK), lambda i: (i, 0)),
                   pl.BlockSpec((VT_WIDTH, tm), lambda i: (0, i))],
        out_shape=[jax.ShapeDtypeStruct((m, D_TOK), BF16), jax.ShapeDtypeStruct((VT_WIDTH, m), BF16)],
        compiler_params=_cparams("parallel"),
        name="inproj",
    )(x2d, w_bf, b_row, wvt_bf, bvt_col)


def _na_bias_table(rpb):
    kr = NA_WIN_ROWS
    pad = GRID_W - NA_WIN_COLS
    p = jnp.pad(rpb.astype(F32) * LOG2E, ((0, 0), (0, 0), (pad, pad)))
    cols = jnp.stack([p[:, :, GRID_W - 1 - w:2 * GRID_W - 1 - w] for w in range(GRID_W)], axis=2)
    t = jnp.stack([cols[:, kr - 1 - d:2 * kr - 1 - d] for d in range(kr)], axis=0)
    w = np.arange(GRID_W)[:, None]
    kc = np.arange(GRID_W)[None, :]
    col_start = np.clip(w - NA_WIN_COLS // 2, 0, GRID_W - NA_WIN_COLS)
    valid = (kc >= col_start) & (kc < col_start + NA_WIN_COLS)
    t = jnp.where(valid[None, None, None], t, NEG_BIG)
    t = jnp.transpose(t, (0, 1, 2, 4, 3))
    return t.reshape(kr, NA_HEADS, kr * GRID_W, GRID_W)


def _na_pair_geometry(p, rows):
    base = min(max(2 * p - NA_WIN_ROWS // 2, 0), rows - NA_PAIR_ROWS)
    geo = []
    for j in range(2):
        r = 2 * p + j
        rs = min(max(r - NA_WIN_ROWS // 2, 0), rows - NA_WIN_ROWS)
        assert 0 <= rs - base <= NA_PAIR_ROWS - NA_WIN_ROWS
        geo.append((rs - base, r - rs))
    return base, geo


def _na_pair_table(table, rows):
    n_pairs = rows // 2
    variants = []
    for p in (0, 1, 2, n_pairs - 2, n_pairs - 1):
        _, geo = _na_pair_geometry(p, rows)
        cols = []
        for off, delta in geo:
            after = NA_PAIR_ROWS - NA_WIN_ROWS - off
            cols.append(jnp.pad(table[delta], ((0, 0), (off * GRID_W, after * GRID_W), (0, 0)),
                                constant_values=NEG_BIG))
        variants.append(jnp.concatenate(cols, axis=-1))
    return jnp.stack(variants, axis=0)


def _scores_t_pair(k, q):
    lane = lax.broadcasted_iota(jnp.int32, (1, LANE), 1)
    zero = jnp.zeros((), q.dtype)
    q_both = jnp.concatenate([jnp.where(lane < HEAD_DIM, q, zero), jnp.where(lane >= HEAD_DIM, q, zero)], axis=0)
    s = lax.dot_general(k, q_both, (((1,), (1,)), ((), ())), preferred_element_type=F32)
    return s[:, :q.shape[0]], s[:, q.shape[0]:]


def _weights_t(s, sink=None):
    m = jnp.max(s, axis=0, keepdims=True)
    if sink is None:
        return jnp.exp2(s - m).astype(BF16), None
    m = jnp.maximum(m, sink)
    return jnp.exp2(s - m).astype(BF16), jnp.exp2(sink - m)


def _weighted_values_t(vt, p, sink_term=None):
    ones = jnp.ones((SUBLANE_PACK, vt.shape[1]), vt.dtype)
    acc = jnp.dot(jnp.concatenate([vt, ones], axis=0), p, preferred_element_type=F32)
    total = acc[vt.shape[0]:vt.shape[0] + 1]
    if sink_term is not None:
        total = total + sink_term
    return acc[:vt.shape[0]] / total


def _na_kernel(q_ref, k_ref, vt_ref, *rest, rows, pairs_per_step):
    bias_refs, o_ref = rest[:-1], rest[-1]
    span = NA_PAIR_ROWS * GRID_W
    work = []
    for u in range(pairs_per_step):
        pair = pl.program_id(1) * pairs_per_step + u
        base = jnp.clip(2 * pair - NA_WIN_ROWS // 2, 0, rows - NA_PAIR_ROWS)
        start = pl.multiple_of(base * GRID_W, 2 * GRID_W)
        q = q_ref[0, u * LANE:(u + 1) * LANE, :]
        k = k_ref[0, pl.ds(start, span), :]
        vt = vt_ref[:, pl.ds(start, span)]
        for j in range(NA_HEADS // 2):
            grp = slice(j * LANE, (j + 1) * LANE)
            for h, s in zip((2 * j, 2 * j + 1), _scores_t_pair(k[:, grp], q[:, grp])):
                work.append((vt[h * HEAD_DIM:(h + 1) * HEAD_DIM, :],) + _weights_t(s + bias_refs[u][0, h]))
    outs = [_weighted_values_t(*w) for w in work]
    for u in range(pairs_per_step):
        o_t = jnp.concatenate(outs[u * NA_HEADS:(u + 1) * NA_HEADS], axis=0)
        o_ref[0, u * LANE:(u + 1) * LANE, :] = o_t.T.astype(o_ref.dtype)


def _na_attention(h3d, vt, bias_table):
    b_sz, seq_len, _ = h3d.shape
    rows = seq_len // GRID_W
    n_pairs = rows // 2
    assert rows % 2 == 0 and n_pairs >= 5
    cb = P_NA // NA_WIDTH
    span = NA_PAIR_ROWS * GRID_W
    pair_table = _na_pair_table(bias_table, rows)

    def variant(p):
        return jnp.where(p < 2, p, jnp.where(p > n_pairs - 3, p - (n_pairs - 5), 2))

    pps = ATTN_BLOCKS_PER_STEP
    assert n_pairs % pps == 0 and 2 * GRID_W == LANE
    bias_specs = [pl.BlockSpec((1, NA_HEADS, span, LANE), lambda b, p, u=u: (variant(p * pps + u), 0, 0, 0))
                  for u in range(pps)]
    return pl.pallas_call(
        functools.partial(_na_kernel, rows=rows, pairs_per_step=pps),
        grid=(b_sz, n_pairs // pps),
        in_specs=[
            pl.BlockSpec((1, pps * LANE, NA_WIDTH), lambda b, p: (b, p, cb)),
            pl.BlockSpec((1, seq_len, NA_WIDTH), lambda b, p: (b, 0, cb + 1), pipeline_mode=pl.Buffered(1)),
            pl.BlockSpec((NA_WIDTH, seq_len), lambda b, p: (0, b), pipeline_mode=pl.Buffered(1)),
        ] + bias_specs,
        out_specs=pl.BlockSpec((1, pps * LANE, NA_WIDTH), lambda b, p: (b, p, 0)),
        out_shape=jax.ShapeDtypeStruct((b_sz, seq_len, NA_WIDTH), BF16),
        compiler_params=_cparams("parallel", "arbitrary"),
        name="na_attention",
    )(h3d, h3d, vt, *([pair_table] * pps))


def _swa_table():
    span = SWA_BLOCK + 2 * SWA_WINDOW
    slopes = 2.0 ** (-8.0 * (np.arange(SWA_HEADS, dtype=np.float64) + 1.0) / SWA_HEADS)
    kk = np.arange(span)[:, None]
    t = np.arange(SWA_BLOCK)[None, :]
    variants = []
    for off in (0, SWA_BLOCK, 2 * SWA_BLOCK):
        rel = np.abs(kk - off - t)
        a = -slopes[:, None, None] * rel[None].astype(np.float64) * LOG2E
        variants.append(np.where((rel <= SWA_WINDOW)[None], a, NEG_BIG))
    return jnp.asarray(np.stack(variants), F32)


def _swa_kernel(sink_ref, q_ref, k_ref, vt_ref, *rest, seq_len, blocks_per_step):
    tab_refs, o_ref = rest[:-1], rest[-1]
    span = SWA_BLOCK + 2 * SWA_WINDOW
    work = []
    for u in range(blocks_per_step):
        blk = pl.program_id(1) * blocks_per_step + u
        start = pl.multiple_of(jnp.clip((blk - 1) * SWA_BLOCK, 0, seq_len - span), SWA_BLOCK)
        q = q_ref[0, u * SWA_BLOCK:(u + 1) * SWA_BLOCK, :]
        k = k_ref[0, pl.ds(start, span), :]
        vt = vt_ref[:, pl.ds(start, span)]
        per_head = {}
        for j in range(SWA_GROUP):
            grp = slice(j * LANE, (j + 1) * LANE)
            for g, s in enumerate(_scores_t_pair(k, q[:, grp])):
                h = j + g * SWA_GROUP
                per_head[h] = ((vt[g * HEAD_DIM:(g + 1) * HEAD_DIM, :],)
                               + _weights_t(s + tab_refs[u][0, h], sink_ref[h] * LOG2E))
        work.extend(per_head[h] for h in range(SWA_HEADS))
    outs = [_weighted_values_t(*w) for w in work]
    for u in range(blocks_per_step):
        o_t = jnp.concatenate(outs[u * SWA_HEADS:(u + 1) * SWA_HEADS], axis=0)
        o_ref[0, u * SWA_BLOCK:(u + 1) * SWA_BLOCK, :] = o_t.T.astype(o_ref.dtype)


def _swa_attention(h3d, vt, sink, table):
    b_sz, seq_len, _ = h3d.shape
    nb = seq_len // SWA_BLOCK
    span = SWA_BLOCK + 2 * SWA_WINDOW
    assert SWA_KV_HEADS == 2 and SWA_BLOCK == LANE and nb >= 3
    qb = P_SWA // SWA_WIDTH
    kb = (P_SWA + SWA_WIDTH) // SWA_KV_WIDTH

    def variant(i):
        return jnp.where(i == 0, 0, jnp.where(i == nb - 1, 2, 1))

    bps = ATTN_BLOCKS_PER_STEP
    assert nb % bps == 0
    tab_specs = [pl.BlockSpec((1, SWA_HEADS, span, SWA_BLOCK), lambda b, i, u=u: (variant(i * bps + u), 0, 0, 0))
                 for u in range(bps)]
    return pl.pallas_call(
        functools.partial(_swa_kernel, seq_len=seq_len, blocks_per_step=bps),
        grid=(b_sz, nb // bps),
        in_specs=[
            pl.BlockSpec(memory_space=pltpu.SMEM),
            pl.BlockSpec((1, bps * SWA_BLOCK, SWA_WIDTH), lambda b, i: (b, i, qb)),
            pl.BlockSpec((1, seq_len, SWA_KV_WIDTH), lambda b, i: (b, 0, kb)),
            pl.BlockSpec((SWA_KV_WIDTH, seq_len), lambda b, i: (NA_WIDTH // SWA_KV_WIDTH, b)),
        ] + tab_specs,
        out_specs=pl.BlockSpec((1, bps * SWA_BLOCK, SWA_WIDTH), lambda b, i: (b, i, 0)),
        out_shape=jax.ShapeDtypeStruct((b_sz, seq_len, SWA_WIDTH), BF16),
        compiler_params=_cparams("parallel", "arbitrary"),
        name="swa_attention",
    )(sink.astype(F32), h3d, h3d, vt, *([table] * bps))


def _shift_matrix(sub, halo):
    s = np.zeros((2 * sub, sub + 2 * halo), np.float32)
    t = np.arange(sub)
    s[t, halo + t - 1] = 1.0
    s[sub + t, halo + t + 1] = 1.0
    return jnp.asarray(s).astype(BF16)


def _conv_kernel(x_ref, prev_ref, next_ref, s_ref, w_ref, b_ref, v_ref, x1_ref, x2_ref, *, n_tiles):
    i = pl.program_id(1)
    tl = x_ref.shape[1]
    halo = prev_ref.shape[1]
    sub = s_ref.shape[0] // 2
    prev = (prev_ref[0].astype(F32) * jnp.where(i > 0, 1.0, 0.0)).astype(BF16)
    nxt = (next_ref[0].astype(F32) * jnp.where(i < n_tiles - 1, 1.0, 0.0)).astype(BF16)
    xe = jnp.concatenate([prev, x_ref[0], nxt], axis=0)
    outs = (v_ref, x1_ref, x2_ref)
    for r0 in range(0, tl, sub):
        y = jnp.dot(s_ref[...], xe[r0:r0 + sub + 2 * halo], preferred_element_type=F32)
        xc = xe[halo + r0:halo + r0 + sub].astype(F32)
        u = y[:sub] * w_ref[0:1, :] + xc * w_ref[1:2, :] + y[sub:] * w_ref[2:3, :] + b_ref[...]
        for c in range(3):
            outs[c][0, r0:r0 + sub, :] = u[:, c * HY_WIDTH:(c + 1) * HY_WIDTH].astype(BF16)


def _short_conv(h3d, conv_w, conv_b):
    b_sz, seq_len, _ = h3d.shape
    tl = CONV_TL
    halo = 16
    n_tiles = seq_len // tl
    width = 3 * HY_WIDTH
    cb = P_HY // width
    assert P_HY % width == 0 and seq_len % tl == 0
    per = tl // halo
    out = jax.ShapeDtypeStruct((b_sz, seq_len, HY_WIDTH), BF16)
    ospec = pl.BlockSpec((1, tl, HY_WIDTH), lambda b, i: (b, i, 0))
    return pl.pallas_call(
        functools.partial(_conv_kernel, n_tiles=n_tiles),
        grid=(b_sz, n_tiles),
        in_specs=[
            pl.BlockSpec((1, tl, width), lambda b, i: (b, i, cb)),
            pl.BlockSpec((1, halo, width), lambda b, i: (b, jnp.maximum(i * per - 1, 0), cb)),
            pl.BlockSpec((1, halo, width), lambda b, i: (b, jnp.minimum((i + 1) * per, seq_len // halo - 1), cb)),
            _resident((2 * CONV_SUB, CONV_SUB + 2 * halo)),
            _resident((HY_SHORT_CONV, width)),
            _resident((1, width)),
        ],
        out_specs=[ospec, ospec, ospec],
        out_shape=[out, out, out],
        compiler_params=_cparams("parallel", "arbitrary"),
        name="hyena_short_conv",
    )(h3d, h3d, h3d, _shift_matrix(CONV_SUB, halo), conv_w.astype(F32), conv_b.astype(F32).reshape(1, width))


def _filter_embedding(seq_len):
    t = np.linspace(0.0, 1.0, seq_len, dtype=np.float32).astype(np.float64)[:, None]
    w = (2.0 * math.pi * np.arange(seq_len, dtype=np.float32) / seq_len).astype(np.float32)
    bands = np.linspace(1e-4, HY_POS_BANDS - 1, HY_POS_BANDS, dtype=np.float32)
    ang = (w[:, None] * bands[None, :]).astype(np.float32).astype(np.float64)
    z = np.concatenate([t, np.cos(ang), -np.sin(ang)], axis=-1)
    zp = np.zeros((2 * seq_len, HY_EMB_PAD), np.float32)
    zp[:seq_len, :HY_EMB_DIM] = z
    zp[seq_len + 1:, :HY_EMB_DIM] = z[:0:-1]
    return zp


def _filter_deltas():
    min_decay = math.log(HY_DECAY_TARGET) / HY_SLOW_DECAY_PCT
    max_decay = math.log(HY_DECAY_TARGET) / HY_FAST_DECAY_PCT
    return np.abs(np.linspace(min_decay, max_decay, HY_WIDTH, dtype=np.float32))[None, :]


def _filter_kernel(z_ref, w1_ref, b1_ref, w2_ref, b2_ref, fr_ref, w3_ref, dl_ref, o_ref, *, seq_len):
    hi = lax.Precision.HIGHEST
    tl = z_ref.shape[0]
    z = z_ref[...]
    fr = fr_ref[...]
    h = jnp.sin(fr * (jnp.dot(z, w1_ref[...], precision=hi, preferred_element_type=F32) + b1_ref[...]))
    h = jnp.sin(fr * (jnp.dot(h, w2_ref[...], precision=hi, preferred_element_type=F32) + b2_ref[...]))
    t = z[:, 0:1]
    window = jnp.exp(-t * dl_ref[...])
    first_row = pl.program_id(0) * tl
    row = first_row + lax.broadcasted_iota(jnp.int32, (tl, 1), 0)

    def emit(direction):
        for o in range(HY_ORDER):
            c0 = (2 * o + direction) * HY_WIDTH
            f = jnp.dot(h, w3_ref[:, c0:c0 + HY_WIDTH], precision=hi, preferred_element_type=F32) * window
            if direction == 1:
                f = jnp.where(row > seq_len, f, 0.0)
            o_ref[o] = f.astype(o_ref.dtype)

    pl.when(first_row < seq_len)(lambda: emit(0))
    pl.when(first_row >= seq_len)(lambda: emit(1))


def _hyena_kernels(seq_len, w1, b1, w2, b2, freq, w3):
    tl = min(FILT_TL, seq_len)
    assert seq_len % tl == 0
    z = jnp.asarray(_filter_embedding(seq_len))
    w1p = jnp.zeros((HY_EMB_PAD, HY_FILTER_HIDDEN), F32).at[:HY_EMB_DIM].set(w1.astype(F32))
    hid = HY_FILTER_HIDDEN
    return pl.pallas_call(
        functools.partial(_filter_kernel, seq_len=seq_len),
        grid=(2 * seq_len // tl,),
        in_specs=[
            pl.BlockSpec((tl, HY_EMB_PAD), lambda i: (i, 0)),
            _resident((HY_EMB_PAD, hid)), _resident((1, hid)),
            _resident((hid, hid)), _resident((1, hid)), _resident((1, hid)),
            _resident((hid, 2 * HY_ORDER * HY_WIDTH)), _resident((1, HY_WIDTH)),
        ],
        out_specs=pl.BlockSpec((HY_ORDER, tl, HY_WIDTH), lambda i: (0, i, 0)),
        out_shape=jax.ShapeDtypeStruct((HY_ORDER, 2 * seq_len, HY_WIDTH), BF16),
        compiler_params=_cparams("parallel"),
        name="hyena_filters",
    )(z, w1p, b1.astype(F32).reshape(1, hid), w2.astype(F32), b2.astype(F32).reshape(1, hid),
      freq.astype(F32).reshape(1, hid), w3.astype(F32), jnp.asarray(_filter_deltas()))


def _fft_dims(seq_len):
    n = 2 * seq_len
    n1 = 1 << ((n.bit_length() - 1) // 2)
    n2 = n // n1
    assert n1 * n2 == n and n1 == n2, "sequence length must give a square transform"
    return n1, n2


def _stack_complex(fr, fi):
    return np.block([[fr, -fi], [fi, fr]])


@functools.lru_cache(maxsize=None)
def _dft_constants(seq_len):
    n1, n2 = _fft_dims(seq_len)
    n = n1 * n2
    k1 = np.arange(n1)[:, None].astype(np.float64)
    t1 = np.arange(n1 // 2)[None, :].astype(np.float64)
    a1 = -2.0 * math.pi * k1 * t1 / n1
    f1r, f1i = np.cos(a1), np.sin(a1)
    w1_complex = _stack_complex(f1r, f1i)
    a1_full = -2.0 * math.pi * k1 * np.arange(n1)[None, :].astype(np.float64) / n1
    w1_real = np.concatenate([np.cos(a1_full), np.sin(a1_full)], axis=0)
    k2 = np.arange(n2)[:, None].astype(np.float64)
    t2 = np.arange(n2)[None, :].astype(np.float64)
    a2 = -2.0 * math.pi * k2 * t2 / n2
    f2r, f2i = np.cos(a2), np.sin(a2)
    g1r, g1i = f1r.T / n, -f1i.T / n
    w3 = _stack_complex(g1r, g1i)
    kb = np.arange(HY_TW_SPLIT)[:, None].astype(np.float64)
    ab = -2.0 * math.pi * kb * t2 / n
    tbr, tbi = np.cos(ab), np.sin(ab)
    w2f = np.stack([_stack_complex(f2r * tbr[b] - f2i * tbi[b], f2r * tbi[b] + f2i * tbr[b])
                    for b in range(HY_TW_SPLIT)])
    w2i = np.stack([_stack_complex(tbr[b][:, None] * f2r - tbi[b][:, None] * f2i,
                                   -(tbr[b][:, None] * f2i + tbi[b][:, None] * f2r))
                    for b in range(HY_TW_SPLIT)])
    ka = (np.arange(n1 // HY_TW_SPLIT) * HY_TW_SPLIT)[:, None].astype(np.float64)
    aa = -2.0 * math.pi * ka * t2 / n
    ta = np.stack([np.cos(aa), np.sin(aa)], axis=0)[..., None]
    ta = np.broadcast_to(ta, (2, n1 // HY_TW_SPLIT, n2, LANE)).astype(np.float32)
    as_bf = lambda a: jnp.asarray(a, F32).astype(BF16)
    return dict(w1_complex=as_bf(w1_complex), w1_real=as_bf(w1_real), w2f=as_bf(w2f), w2i=as_bf(w2i),
                ta=jnp.asarray(ta), w3=as_bf(w3))


def _pack_pair(a, b):
    return pltpu.pack_elementwise([a, b], packed_dtype=BF16)


def _unpack_pair(w):
    return (pltpu.unpack_elementwise(w, index=0, packed_dtype=BF16, unpacked_dtype=F32),
            pltpu.unpack_elementwise(w, index=1, packed_dtype=BF16, unpacked_dtype=F32))


def _conv_pitch(n2):
    return n2 + 8


def _hyena_conv_kernel(z_ref, g_ref, kf_ref, w1_ref, w2f_ref, w2i_ref, w3_ref, ta_ref, bias_ref, o_ref,
                       x_scr, a_scr, *, n1, n2):
    pitch = _conv_pitch(n2)
    half = n1 // 2
    groups = n1 // HY_TW_SPLIT

    def pack_in(t1, carry):
        rows = pl.ds(pl.multiple_of(t1 * n2, n2), n2)
        x_scr[pl.ds(pl.multiple_of(t1 * pitch, 8), n2), :] = _pack_pair(
            z_ref[0, rows, :].astype(F32), z_ref[1, rows, :].astype(F32))
        return carry

    lax.fori_loop(0, half, pack_in, 0)

    def stage1(j, carry):
        cols = []
        for u in range(2):
            xr, xi = _unpack_pair(x_scr[pl.ds(2 * j + u, half, stride=pitch), :])
            cols.append(jnp.concatenate([xr, xi], axis=0).astype(BF16))
        out = jnp.dot(w1_ref[...], jnp.concatenate(cols, axis=1), preferred_element_type=F32)
        for u in range(2):
            o = out[:, u * LANE:(u + 1) * LANE]
            a_scr[pl.ds(2 * j + u, n1, stride=pitch), :] = _pack_pair(o[:n1], o[n1:])
        return carry

    lax.fori_loop(0, n2 // 2, stage1, 0, unroll=4)

    for kb in range(HY_TW_SPLIT):
        def stage2(jp, carry, kb=kb):
            rows, tws, stacks = [], [], []
            for u in range(2):
                ka = 2 * jp + u
                k1 = ka * HY_TW_SPLIT + kb
                rows.append(pl.ds(pl.multiple_of(k1 * pitch, 8), n2))
                ar, ai = _unpack_pair(a_scr[rows[u], :])
                twr, twi = ta_ref[0, ka], ta_ref[1, ka]
                tws.append((twr, twi))
                stacks.append(jnp.concatenate([ar * twr - ai * twi, ar * twi + ai * twr], axis=0).astype(BF16))
            spec = jnp.dot(w2f_ref[kb], jnp.concatenate(stacks, axis=1), preferred_element_type=F32)
            prods = []
            for u in range(2):
                k1 = (2 * jp + u) * HY_TW_SPLIT + kb
                sr, si = spec[:n2, u * LANE:(u + 1) * LANE], spec[n2:, u * LANE:(u + 1) * LANE]
                kr = kf_ref[0, 0, k1].astype(F32)
                ki = kf_ref[0, 1, k1].astype(F32)
                prods.append(jnp.concatenate([sr * kr - si * ki, sr * ki + si * kr], axis=0).astype(BF16))
            back = jnp.dot(w2i_ref[kb], jnp.concatenate(prods, axis=1), preferred_element_type=F32)
            for u in range(2):
                br, bi = back[:n2, u * LANE:(u + 1) * LANE], back[n2:, u * LANE:(u + 1) * LANE]
                twr, twi = tws[u]
                a_scr[rows[u], :] = _pack_pair(br * twr + bi * twi, bi * twr - br * twi)
            return carry

        lax.fori_loop(0, groups // 2, stage2, 0, unroll=4)

    def stage3(j, carry):
        cols = []
        for u in range(2):
            br, bi = _unpack_pair(a_scr[pl.ds(2 * j + u, n1, stride=pitch), :])
            cols.append(jnp.concatenate([br, bi], axis=0).astype(BF16))
        y = jnp.dot(w3_ref[...], jnp.concatenate(cols, axis=1), preferred_element_type=F32)
        for u in range(2):
            sl = pl.ds(2 * j + u, half, stride=pitch)
            za, zb = _unpack_pair(x_scr[sl, :])
            yu = y[:, u * LANE:(u + 1) * LANE]
            x_scr[sl, :] = _pack_pair(yu[:half] + bias_ref[...] * za, yu[half:] + bias_ref[...] * zb)
        return carry

    lax.fori_loop(0, n2 // 2, stage3, 0, unroll=4)

    def gate_out(t1, carry):
        rows = pl.ds(pl.multiple_of(t1 * n2, n2), n2)
        ta, tb = _unpack_pair(x_scr[pl.ds(pl.multiple_of(t1 * pitch, 8), n2), :])
        o_ref[0, rows, :] = (g_ref[0, rows, :].astype(F32) * ta).astype(o_ref.dtype)
        o_ref[1, rows, :] = (g_ref[1, rows, :].astype(F32) * tb).astype(o_ref.dtype)
        return carry

    lax.fori_loop(0, half, gate_out, 0)


def _hyena_conv(z, gate, kf, order, bias, consts):
    s, seq_len, c = z.shape
    n1, n2 = _fft_dims(seq_len)
    pitch = _conv_pitch(n2)
    assert s % 2 == 0 and c % LANE == 0 and (n1 // HY_TW_SPLIT) % 2 == 0
    seq_blk = pl.BlockSpec((2, seq_len, LANE), lambda cc, p: (p, 0, cc))
    return pl.pallas_call(
        functools.partial(_hyena_conv_kernel, n1=n1, n2=n2),
        grid=(c // LANE, s // 2),
        in_specs=[
            seq_blk,
            seq_blk,
            pl.BlockSpec((1, 2, n1, n2, LANE), lambda cc, p: (order, 0, 0, 0, cc), pipeline_mode=pl.Buffered(1)),
            _resident(consts["w1_complex"].shape),
            _resident(consts["w2f"].shape),
            _resident(consts["w2i"].shape),
            _resident(consts["w3"].shape),
            _resident(consts["ta"].shape),
            pl.BlockSpec((1, LANE), lambda cc, p: (0, cc)),
        ],
        out_specs=seq_blk,
        out_shape=jax.ShapeDtypeStruct(z.shape, BF16),
        scratch_shapes=[pltpu.VMEM((n1 // 2 * pitch, LANE), jnp.uint32),
                        pltpu.VMEM((n1 * pitch, LANE), jnp.uint32)],
        compiler_params=_cparams("parallel", "arbitrary"),
        name="hyena_conv",
    )(z, gate, kf, consts["w1_complex"], consts["w2f"], consts["w2i"], consts["w3"], consts["ta"],
      bias.astype(F32).reshape(1, c))


def _spectrum_kernel(k_ref, w1_ref, w2f_ref, ta_ref, o_ref, x_scr, a_scr, *, n1, n2):
    pitch = _conv_pitch(n2)
    groups = n1 // HY_TW_SPLIT

    def pack_in(t1, carry):
        x_scr[pl.ds(pl.multiple_of(t1 * pitch, 8), n2), :] = (
            k_ref[0, pl.ds(pl.multiple_of(t1 * n2, n2), n2), :].astype(F32))
        return carry

    lax.fori_loop(0, n1, pack_in, 0)

    def stage1(j, carry):
        cols = [x_scr[pl.ds(2 * j + u, n1, stride=pitch), :].astype(BF16) for u in range(2)]
        out = jnp.dot(w1_ref[...], jnp.concatenate(cols, axis=1), preferred_element_type=F32)
        for u in range(2):
            o = out[:, u * LANE:(u + 1) * LANE]
            a_scr[pl.ds(2 * j + u, n1, stride=pitch), :] = _pack_pair(o[:n1], o[n1:])
        return carry

    lax.fori_loop(0, n2 // 2, stage1, 0, unroll=4)

    for kb in range(HY_TW_SPLIT):
        def stage2(jp, carry, kb=kb):
            stacks = []
            for u in range(2):
                ka = 2 * jp + u
                k1 = ka * HY_TW_SPLIT + kb
                ar, ai = _unpack_pair(a_scr[pl.ds(pl.multiple_of(k1 * pitch, 8), n2), :])
                twr, twi = ta_ref[0, ka], ta_ref[1, ka]
                stacks.append(jnp.concatenate([ar * twr - ai * twi, ar * twi + ai * twr], axis=0).astype(BF16))
            spec = jnp.dot(w2f_ref[kb], jnp.concatenate(stacks, axis=1), preferred_element_type=F32)
            for u in range(2):
                k1 = (2 * jp + u) * HY_TW_SPLIT + kb
                o_ref[0, 0, k1] = spec[:n2, u * LANE:(u + 1) * LANE].astype(o_ref.dtype)
                o_ref[0, 1, k1] = spec[n2:, u * LANE:(u + 1) * LANE].astype(o_ref.dtype)
            return carry

        lax.fori_loop(0, groups // 2, stage2, 0, unroll=2)


def _filter_spectrum(kern, consts):
    _, n, c = kern.shape
    n1, n2 = _fft_dims(n // 2)
    pitch = _conv_pitch(n2)
    return pl.pallas_call(
        functools.partial(_spectrum_kernel, n1=n1, n2=n2),
        grid=(HY_ORDER, c // LANE),
        in_specs=[
            pl.BlockSpec((1, n, LANE), lambda o, cc: (o, 0, cc), pipeline_mode=pl.Buffered(1)),
            _resident(consts["w1_real"].shape),
            _resident(consts["w2f"].shape),
            _resident(consts["ta"].shape),
        ],
        out_specs=pl.BlockSpec((1, 2, n1, n2, LANE), lambda o, cc: (o, 0, 0, 0, cc)),
        out_shape=jax.ShapeDtypeStruct((HY_ORDER, 2, n1, n2, c), BF16),
        scratch_shapes=[pltpu.VMEM((n1 * pitch, LANE), F32), pltpu.VMEM((n1 * pitch, LANE), jnp.uint32)],
        compiler_params=_cparams("parallel", "arbitrary"),
        name="hyena_filter_spectrum",
    )(kern, consts["w1_real"], consts["w2f"], consts["ta"])


def _hyena_spectra(seq_len, w1, b1, w2, b2, freq, w3):
    consts = _dft_constants(seq_len)
    return _filter_spectrum(_hyena_kernels(seq_len, w1, b1, w2, b2, freq, w3), consts)


def _hyena(h3d, conv_w, conv_b, kf, hy_bias):
    seq_len = h3d.shape[1]
    consts = _dft_constants(seq_len)
    v, x1, x2 = _short_conv(h3d, conv_w, conv_b)
    z = v
    for n, gate in enumerate((x1, x2)):
        z = _hyena_conv(z, gate, kf, n, hy_bias[n], consts)
    return z


def _layernorm(y, g, b):
    mu = jnp.mean(y, axis=-1, keepdims=True)
    d = y - mu
    var = jnp.mean(d * d, axis=-1, keepdims=True)
    return d * lax.rsqrt(var + LN_EPS) * g + b


def _merge_kernel(g_ref, a_ref, hb_ref, c_ref, x_ref, wa_ref, wb_ref, wc_ref, wo_ref, lg_ref, lb_ref, o_ref):
    d = D_MODEL
    subs = [slice(r, r + MERGE_SUB) for r in range(0, x_ref.shape[0], MERGE_SUB)]
    merged = []
    for r in subs:
        acc = g_ref[r, 0:d].astype(F32) * jnp.dot(a_ref[r, :], wa_ref[...], preferred_element_type=F32)
        acc += g_ref[r, d:2 * d].astype(F32) * jnp.dot(hb_ref[r, :], wb_ref[...], preferred_element_type=F32)
        acc += g_ref[r, 2 * d:3 * d].astype(F32) * jnp.dot(c_ref[r, :], wc_ref[...], preferred_element_type=F32)
        merged.append(acc.astype(BF16))
    for r, mg in zip(subs, merged):
        mix = jnp.dot(mg, wo_ref[...], preferred_element_type=F32)
        o_ref[r, :] = _layernorm(DEEPNORM_ALPHA * x_ref[r, :] + mix, lg_ref[...], lb_ref[...])


def _merge(h2d, a, hb, c, x2d, wa, wb, wc, wo, ln_g, ln_b):
    m = x2d.shape[0]
    tm = MERGE_TM
    row = lambda width: pl.BlockSpec((tm, width), lambda i: (i, 0))
    return pl.pallas_call(
        _merge_kernel,
        grid=(m // tm,),
        in_specs=[
            row(N_BRANCH * D_MODEL), row(NA_WIDTH), row(HY_WIDTH), row(SWA_WIDTH), row(D_MODEL),
            _resident((NA_WIDTH, D_MODEL)), _resident((HY_WIDTH, D_MODEL)), _resident((SWA_WIDTH, D_MODEL)),
            _resident((D_MODEL, D_MODEL)), _resident((1, D_MODEL)), _resident((1, D_MODEL)),
        ],
        out_specs=row(D_MODEL),
        out_shape=jax.ShapeDtypeStruct((m, D_MODEL), F32),
        compiler_params=_cparams("parallel"),
        name="merge_ln",
    )(h2d, a, hb, c, x2d, wa, wb, wc, wo, ln_g, ln_b)


def _mlp_kernel(x_ref, wu_ref, bu_ref, wd_ref, bd_ref, lg_ref, lb_ref, o_ref):
    for r0 in range(0, x_ref.shape[0], MLP_SUB):
        rows = slice(r0, r0 + MLP_SUB)
        x = x_ref[rows, :]
        xb = x.astype(BF16)
        acc = DEEPNORM_ALPHA * x + bd_ref[...]
        for c in range(D_FF // MLP_FF_CHUNK):
            sl = slice(c * MLP_FF_CHUNK, (c + 1) * MLP_FF_CHUNK)
            up = jnp.dot(xb, wu_ref[:, sl], preferred_element_type=F32) + bu_ref[:, sl]
            up = jnp.square(jnp.maximum(up, 0.0))
            acc += jnp.dot(up.astype(BF16), wd_ref[sl, :], preferred_element_type=F32)
        o_ref[rows, :] = _layernorm(acc, lg_ref[...], lb_ref[...])


def _mlp(x2d, wu, bu, wd, bd, ln_g, ln_b):
    m = x2d.shape[0]
    tm = MLP_TM
    row = pl.BlockSpec((tm, D_MODEL), lambda i: (i, 0))
    return pl.pallas_call(
        _mlp_kernel,
        grid=(m // tm,),
        in_specs=[
            row, _resident((D_MODEL, D_FF)), _resident((1, D_FF)), _resident((D_FF, D_MODEL)),
            _resident((1, D_MODEL)), _resident((1, D_MODEL)), _resident((1, D_MODEL)),
        ],
        out_specs=row,
        out_shape=jax.ShapeDtypeStruct((m, D_MODEL), F32),
        compiler_params=_cparams("parallel"),
        name="mlp_ln",
    )(x2d, wu, bu, wd, bd, ln_g, ln_b)


def _permute_in_columns(a):
    q0 = OFF_SWA
    order = [h for j in range(SWA_GROUP) for h in (j, j + SWA_GROUP)]
    swa_q = [a[..., q0 + h * HEAD_DIM:q0 + (h + 1) * HEAD_DIM] for h in order]
    swa_k = a[..., q0 + SWA_WIDTH:q0 + SWA_WIDTH + SWA_KV_WIDTH]
    return jnp.concatenate([a[..., OFF_GATE:], a[..., OFF_HY:OFF_SWA], a[..., :2 * NA_WIDTH]] + swa_q + [swa_k],
                           axis=-1)


def _scale_query_columns(a):
    c = HEAD_DIM ** -0.5 * LOG2E
    return jnp.concatenate([a[..., :NA_WIDTH] * c, a[..., NA_WIDTH:OFF_SWA],
                            a[..., OFF_SWA:OFF_SWA + SWA_WIDTH] * c, a[..., OFF_SWA + SWA_WIDTH:]], axis=-1)


def _values_columns(a):
    v_swa = OFF_SWA + SWA_WIDTH + SWA_KV_WIDTH
    return jnp.concatenate([a[..., 2 * NA_WIDTH:3 * NA_WIDTH], a[..., v_swa:v_swa + SWA_KV_WIDTH]], axis=-1)


def _prepare_layer(l, p):
    row = lambda a: a.astype(F32).reshape(1, -1)
    return dict(
        w_in=_permute_in_columns(_scale_query_columns(p["w_in"][l])).astype(BF16),
        b_in=row(_permute_in_columns(_scale_query_columns(p["b_in"][l]))),
        conv_w=p["hy_conv_w"][l], conv_b=p["hy_conv_b"][l], hy_bias=p["hy_bias"][l],
        filt=(p["hy_filt_w1"][l], p["hy_filt_b1"][l], p["hy_filt_w2"][l], p["hy_filt_b2"][l],
              p["hy_filt_freq"][l], p["hy_filt_w3"][l]),
        w_vt=_values_columns(p["w_in"][l]).T.astype(BF16),
        b_vt=_values_columns(p["b_in"][l]).astype(F32).reshape(VT_WIDTH, 1),
        na_bias=_na_bias_table(p["na_rpb"][l]),
        sink=p["swa_sink"][l],
        wa=p["w_branch_a"][l].astype(BF16), wb=p["w_branch_b"][l].astype(BF16),
        wc=p["w_branch_c"][l].astype(BF16), wo=p["w_out"][l].astype(BF16),
        ln1_g=row(p["ln1_g"][l]), ln1_b=row(p["ln1_b"][l]),
        wu=p["w_up"][l].astype(BF16), bu=row(p["b_up"][l]),
        wd=p["w_down"][l].astype(BF16), bd=row(p["b_down"][l]),
        ln2_g=row(p["ln2_g"][l]), ln2_b=row(p["ln2_b"][l]),
    )


def _encoder_block(x, lp, swa_table):
    b_sz, seq_len, _ = x.shape
    m = b_sz * seq_len
    x2d = x.reshape(m, D_MODEL)
    h2d, vt = _inproj(x2d, lp["w_in"], lp["b_in"], lp["w_vt"], lp["b_vt"])
    h3d = h2d.reshape(b_sz, seq_len, D_TOK)
    a = _na_attention(h3d, vt, lp["na_bias"])
    kf = _hyena_spectra(seq_len, *lp["filt"])
    hb = _hyena(h3d, lp["conv_w"], lp["conv_b"], kf, lp["hy_bias"])
    c = _swa_attention(h3d, vt, lp["sink"], swa_table)
    x1 = _merge(h2d, a.reshape(m, NA_WIDTH), hb.reshape(m, HY_WIDTH), c.reshape(m, SWA_WIDTH), x2d,
                lp["wa"], lp["wb"], lp["wc"], lp["wo"], lp["ln1_g"], lp["ln1_b"])
    x2 = _mlp(x1, lp["wu"], lp["bu"], lp["wd"], lp["bd"], lp["ln2_g"], lp["ln2_b"])
    return x2.reshape(b_sz, seq_len, D_MODEL)


def kernel(x_prompt, x_sample, w_in, b_in, hy_conv_w, hy_conv_b, hy_filt_w1, hy_filt_b1, hy_filt_w2,
           hy_filt_b2, hy_filt_freq, hy_filt_w3, hy_bias, na_rpb, swa_sink, w_branch_a, w_branch_b,
           w_branch_c, w_out, ln1_g, ln1_b, w_up, b_up, w_down, b_down, ln2_g, ln2_b):
    params = dict(w_in=w_in, b_in=b_in, hy_conv_w=hy_conv_w, hy_conv_b=hy_conv_b, hy_filt_w1=hy_filt_w1,
                  hy_filt_b1=hy_filt_b1, hy_filt_w2=hy_filt_w2, hy_filt_b2=hy_filt_b2,
                  hy_filt_freq=hy_filt_freq, hy_filt_w3=hy_filt_w3, hy_bias=hy_bias, na_rpb=na_rpb,
                  swa_sink=swa_sink, w_branch_a=w_branch_a, w_branch_b=w_branch_b, w_branch_c=w_branch_c,
                  w_out=w_out, ln1_g=ln1_g, ln1_b=ln1_b, w_up=w_up, b_up=b_up, w_down=w_down,
                  b_down=b_down, ln2_g=ln2_g, ln2_b=ln2_b)
    swa_table = _swa_table()
    y_prompt = x_prompt
    y_sample = x_sample
    for l in range(DEPTH):
        lp = _prepare_layer(l, params)
        y_prompt = _encoder_block(y_prompt, lp, swa_table)
        y_sample = _encoder_block(y_sample, lp, swa_table)
    return (y_prompt, y_sample)
```

```python
import functools
import math

import numpy as np
import jax
import jax.numpy as jnp
from jax import lax
from jax.experimental import pallas as pl
from jax.experimental.pallas import tpu as pltpu

F32 = jnp.float32
BF16 = jnp.bfloat16

D_MODEL = 1024
DEPTH = 2
HEAD_DIM = 64
GRID_W = 64
NA_HEADS = 8
NA_WIN_ROWS = 8
NA_WIN_COLS = 16
NA_WIDTH = NA_HEADS * HEAD_DIM
HY_WIDTH = D_MODEL // 2
HY_ORDER = 2
HY_SHORT_CONV = 3
HY_POS_BANDS = 16
HY_EMB_DIM = 1 + 2 * HY_POS_BANDS
HY_EMB_PAD = 128
HY_FILTER_HIDDEN = 64
HY_FAST_DECAY_PCT = 0.3
HY_SLOW_DECAY_PCT = 1.5
HY_DECAY_TARGET = 1e-2
SWA_HEADS = 8
SWA_KV_HEADS = 2
SWA_GROUP = SWA_HEADS // SWA_KV_HEADS
SWA_WIDTH = SWA_HEADS * HEAD_DIM
SWA_KV_WIDTH = SWA_KV_HEADS * HEAD_DIM
SWA_WINDOW = 128
SWA_BLOCK = 128
N_BRANCH = 3
D_FF = 4 * D_MODEL
OFF_HY = 3 * NA_WIDTH
OFF_SWA = OFF_HY + 3 * HY_WIDTH
OFF_GATE = OFF_SWA + SWA_WIDTH + 2 * SWA_KV_WIDTH
D_IN = OFF_GATE + N_BRANCH * D_MODEL
DEEPNORM_ALPHA = (2 * DEPTH) ** 0.25
LN_EPS = 1e-5
NEG_BIG = -1e30
LOG2E = 1.4426950408889634

P_GATE = 0
P_HY = N_BRANCH * D_MODEL
P_NA = P_HY + 3 * HY_WIDTH
P_SWA = P_NA + 2 * NA_WIDTH
D_TOK = P_SWA + SWA_WIDTH + SWA_KV_WIDTH
VT_WIDTH = NA_WIDTH + SWA_KV_WIDTH

LANE = 128
SUBLANE_PACK = 16
VMEM_LIMIT_BYTES = 56 * 1024 * 1024

NA_PAIR_ROWS = NA_WIN_ROWS + 2
ATTN_BLOCKS_PER_STEP = 4
INPROJ_TM = 512
INPROJ_TN = 768
MERGE_TM = 1024
MERGE_SUB = 256
MLP_TM = 1024
MLP_SUB = 512
MLP_FF_CHUNK = 1024
CONV_TL = 512
CONV_SUB = 128
FILT_TL = 512
HY_TW_SPLIT = 8


def _cparams(*sem):
    return pltpu.CompilerParams(dimension_semantics=sem, vmem_limit_bytes=VMEM_LIMIT_BYTES)


def _resident(shape):
    nd = len(shape)
    return pl.BlockSpec(shape, lambda *_: (0,) * nd, pipeline_mode=pl.Buffered(1))


def _inproj_kernel(x_ref, w_ref, b_ref, wvt_ref, bvt_ref, o_ref, vt_ref):
    xb = x_ref[...].astype(BF16)
    vt = lax.dot_general(wvt_ref[...], xb, (((1,), (1,)), ((), ())), preferred_element_type=F32)
    vt_ref[...] = (vt + bvt_ref[...]).astype(vt_ref.dtype)
    for c0 in range(0, D_TOK, INPROJ_TN):
        cols = slice(c0, min(c0 + INPROJ_TN, D_TOK))
        acc = jnp.dot(xb, w_ref[:, cols], preferred_element_type=F32) + b_ref[:, cols]
        if cols.stop <= P_HY:
            acc = 1.0 / (1.0 + jnp.exp(-acc))
        o_ref[:, cols] = acc.astype(o_ref.dtype)


def _inproj(x2d, w_bf, b_row, wvt_bf, bvt_col):
    m = x2d.shape[0]
    tm = min(INPROJ_TM, m)
    assert m % tm == 0 and P_HY % INPROJ_TN == 0
    return pl.pallas_call(
        _inproj_kernel,
        grid=(m // tm,),
        in_specs=[
            pl.BlockSpec((tm, D_MODEL), lambda i: (i, 0)),
            _resident((D_MODEL, D_TOK)),
            _resident((1, D_TOK)),
            _resident((VT_WIDTH, D_MODEL)),
            _resident((VT_WIDTH, 1)),
        ],
        out_specs=[pl.BlockSpec((tm, D_TOK), lambda i: (i, 0)),
                   pl.BlockSpec((VT_WIDTH, tm), lambda i: (0, i))],
        out_shape=[jax.ShapeDtypeStruct((m, D_TOK), BF16), jax.ShapeDtypeStruct((VT_WIDTH, m), BF16)],
        compiler_params=_cparams("parallel"),
        name="inproj",
    )(x2d, w_bf, b_row, wvt_bf, bvt_col)


def _na_bias_table(rpb):
    kr = NA_WIN_ROWS
    pad = GRID_W - NA_WIN_COLS
    p = jnp.pad(rpb.astype(F32) * LOG2E, ((0, 0), (0, 0), (pad, pad)))
    cols = jnp.stack([p[:, :, GRID_W - 1 - w:2 * GRID_W - 1 - w] for w in range(GRID_W)], axis=2)
    t = jnp.stack([cols[:, kr - 1 - d:2 * kr - 1 - d] for d in range(kr)], axis=0)
    w = np.arange(GRID_W)[:, None]
    kc = np.arange(GRID_W)[None, :]
    col_start = np.clip(w - NA_WIN_COLS // 2, 0, GRID_W - NA_WIN_COLS)
    valid = (kc >= col_start) & (kc < col_start + NA_WIN_COLS)
    t = jnp.where(valid[None, None, None], t, NEG_BIG)
    t = jnp.transpose(t, (0, 1, 2, 4, 3))
    return t.reshape(kr, NA_HEADS, kr * GRID_W, GRID_W)


def _na_pair_geometry(p, rows):
    base = min(max(2 * p - NA_WIN_ROWS // 2, 0), rows - NA_PAIR_ROWS)
    geo = []
    for j in range(2):
        r = 2 * p + j
        rs = min(max(r - NA_WIN_ROWS // 2, 0), rows - NA_WIN_ROWS)
        assert 0 <= rs - base <= NA_PAIR_ROWS - NA_WIN_ROWS
        geo.append((rs - base, r - rs))
    return base, geo


def _na_pair_table(table, rows):
    n_pairs = rows // 2
    variants = []
    for p in (0, 1, 2, n_pairs - 2, n_pairs - 1):
        _, geo = _na_pair_geometry(p, rows)
        cols = []
        for off, delta in geo:
            after = NA_PAIR_ROWS - NA_WIN_ROWS - off
            cols.append(jnp.pad(table[delta], ((0, 0), (off * GRID_W, after * GRID_W), (0, 0)),
                                constant_values=NEG_BIG))
        variants.append(jnp.concatenate(cols, axis=-1))
    return jnp.stack(variants, axis=0)


def _scores_t_pair(k, q):
    lane = lax.broadcasted_iota(jnp.int32, (1, LANE), 1)
    zero = jnp.zeros((), q.dtype)
    q_both = jnp.concatenate([jnp.where(lane < HEAD_DIM, q, zero), jnp.where(lane >= HEAD_DIM, q, zero)], axis=0)
    s = lax.dot_general(k, q_both, (((1,), (1,)), ((), ())), preferred_element_type=F32)
    return s[:, :q.shape[0]], s[:, q.shape[0]:]


def _weights_t(s, sink=None):
    m = jnp.max(s, axis=0, keepdims=True)
    if sink is None:
        return jnp.exp2(s - m).astype(BF16), None
    m = jnp.maximum(m, sink)
    return jnp.exp2(s - m).astype(BF16), jnp.exp2(sink - m)


def _weighted_values_t(vt, p, sink_term=None):
    ones = jnp.ones((SUBLANE_PACK, vt.shape[1]), vt.dtype)
    acc = jnp.dot(jnp.concatenate([vt, ones], axis=0), p, preferred_element_type=F32)
    total = acc[vt.shape[0]:vt.shape[0] + 1]
    if sink_term is not None:
        total = total + sink_term
    return acc[:vt.shape[0]] / total


def _na_kernel(q_ref, k_ref, vt_ref, *rest, rows, pairs_per_step):
    bias_refs, o_ref = rest[:-1], rest[-1]
    span = NA_PAIR_ROWS * GRID_W
    work = []
    for u in range(pairs_per_step):
        pair = pl.program_id(1) * pairs_per_step + u
        base = jnp.clip(2 * pair - NA_WIN_ROWS // 2, 0, rows - NA_PAIR_ROWS)
        start = pl.multiple_of(base * GRID_W, 2 * GRID_W)
        q = q_ref[0, u * LANE:(u + 1) * LANE, :]
        k = k_ref[0, pl.ds(start, span), :]
        vt = vt_ref[:, pl.ds(start, span)]
        for j in range(NA_HEADS // 2):
            grp = slice(j * LANE, (j + 1) * LANE)
            for h, s in zip((2 * j, 2 * j + 1), _scores_t_pair(k[:, grp], q[:, grp])):
                work.append((vt[h * HEAD_DIM:(h + 1) * HEAD_DIM, :],) + _weights_t(s + bias_refs[u][0, h]))
    outs = [_weighted_values_t(*w) for w in work]
    for u in range(pairs_per_step):
        o_t = jnp.concatenate(outs[u * NA_HEADS:(u + 1) * NA_HEADS], axis=0)
        o_ref[0, u * LANE:(u + 1) * LANE, :] = o_t.T.astype(o_ref.dtype)


def _na_attention(h3d, vt, bias_table):
    b_sz, seq_len, _ = h3d.shape
    rows = seq_len // GRID_W
    n_pairs = rows // 2
    assert rows % 2 == 0 and n_pairs >= 5
    cb = P_NA // NA_WIDTH
    span = NA_PAIR_ROWS * GRID_W
    pair_table = _na_pair_table(bias_table, rows)

    def variant(p):
        return jnp.where(p < 2, p, jnp.where(p > n_pairs - 3, p - (n_pairs - 5), 2))

    pps = ATTN_BLOCKS_PER_STEP
    assert n_pairs % pps == 0 and 2 * GRID_W == LANE
    bias_specs = [pl.BlockSpec((1, NA_HEADS, span, LANE), lambda b, p, u=u: (variant(p * pps + u), 0, 0, 0))
                  for u in range(pps)]
    return pl.pallas_call(
        functools.partial(_na_kernel, rows=rows, pairs_per_step=pps),
        grid=(b_sz, n_pairs // pps),
        in_specs=[
            pl.BlockSpec((1, pps * LANE, NA_WIDTH), lambda b, p: (b, p, cb)),
            pl.BlockSpec((1, seq_len, NA_WIDTH), lambda b, p: (b, 0, cb + 1), pipeline_mode=pl.Buffered(1)),
            pl.BlockSpec((NA_WIDTH, seq_len), lambda b, p: (0, b), pipeline_mode=pl.Buffered(1)),
        ] + bias_specs,
        out_specs=pl.BlockSpec((1, pps * LANE, NA_WIDTH), lambda b, p: (b, p, 0)),
        out_shape=jax.ShapeDtypeStruct((b_sz, seq_len, NA_WIDTH), BF16),
        compiler_params=_cparams("parallel", "arbitrary"),
        name="na_attention",
    )(h3d, h3d, vt, *([pair_table] * pps))


def _swa_table():
    span = SWA_BLOCK + 2 * SWA_WINDOW
    slopes = 2.0 ** (-8.0 * (np.arange(SWA_HEADS, dtype=np.float64) + 1.0) / SWA_HEADS)
    kk = np.arange(span)[:, None]
    t = np.arange(SWA_BLOCK)[None, :]
    variants = []
    for off in (0, SWA_BLOCK, 2 * SWA_BLOCK):
        rel = np.abs(kk - off - t)
        a = -slopes[:, None, None] * rel[None].astype(np.float64) * LOG2E
        variants.append(np.where((rel <= SWA_WINDOW)[None], a, NEG_BIG))
    return jnp.asarray(np.stack(variants), F32)


def _swa_kernel(sink_ref, q_ref, k_ref, vt_ref, *rest, seq_len, blocks_per_step):
    tab_refs, o_ref = rest[:-1], rest[-1]
    span = SWA_BLOCK + 2 * SWA_WINDOW
    work = []
    for u in range(blocks_per_step):
        blk = pl.program_id(1) * blocks_per_step + u
        start = pl.multiple_of(jnp.clip((blk - 1) * SWA_BLOCK, 0, seq_len - span), SWA_BLOCK)
        q = q_ref[0, u * SWA_BLOCK:(u + 1) * SWA_BLOCK, :]
        k = k_ref[0, pl.ds(start, span), :]
        vt = vt_ref[:, pl.ds(start, span)]
        per_head = {}
        for j in range(SWA_GROUP):
            grp = slice(j * LANE, (j + 1) * LANE)
            for g, s in enumerate(_scores_t_pair(k, q[:, grp])):
                h = j + g * SWA_GROUP
                per_head[h] = ((vt[g * HEAD_DIM:(g + 1) * HEAD_DIM, :],)
                               + _weights_t(s + tab_refs[u][0, h], sink_ref[h] * LOG2E))
        work.extend(per_head[h] for h in range(SWA_HEADS))
    outs = [_weighted_values_t(*w) for w in work]
    for u in range(blocks_per_step):
        o_t = jnp.concatenate(outs[u * SWA_HEADS:(u + 1) * SWA_HEADS], axis=0)
        o_ref[0, u * SWA_BLOCK:(u + 1) * SWA_BLOCK, :] = o_t.T.astype(o_ref.dtype)


def _swa_attention(h3d, vt, sink, table):
    b_sz, seq_len, _ = h3d.shape
    nb = seq_len // SWA_BLOCK
    span = SWA_BLOCK + 2 * SWA_WINDOW
    assert SWA_KV_HEADS == 2 and SWA_BLOCK == LANE and nb >= 3
    qb = P_SWA // SWA_WIDTH
    kb = (P_SWA + SWA_WIDTH) // SWA_KV_WIDTH

    def variant(i):
        return jnp.where(i == 0, 0, jnp.where(i == nb - 1, 2, 1))

    bps = ATTN_BLOCKS_PER_STEP
    assert nb % bps == 0
    tab_specs = [pl.BlockSpec((1, SWA_HEADS, span, SWA_BLOCK), lambda b, i, u=u: (variant(i * bps + u), 0, 0, 0))
                 for u in range(bps)]
    return pl.pallas_call(
        functools.partial(_swa_kernel, seq_len=seq_len, blocks_per_step=bps),
        grid=(b_sz, nb // bps),
        in_specs=[
            pl.BlockSpec(memory_space=pltpu.SMEM),
            pl.BlockSpec((1, bps * SWA_BLOCK, SWA_WIDTH), lambda b, i: (b, i, qb)),
            pl.BlockSpec((1, seq_len, SWA_KV_WIDTH), lambda b, i: (b, 0, kb)),
            pl.BlockSpec((SWA_KV_WIDTH, seq_len), lambda b, i: (NA_WIDTH // SWA_KV_WIDTH, b)),
        ] + tab_specs,
        out_specs=pl.BlockSpec((1, bps * SWA_BLOCK, SWA_WIDTH), lambda b, i: (b, i, 0)),
        out_shape=jax.ShapeDtypeStruct((b_sz, seq_len, SWA_WIDTH), BF16),
        compiler_params=_cparams("parallel", "arbitrary"),
        name="swa_attention",
    )(sink.astype(F32), h3d, h3d, vt, *([table] * bps))


def _shift_matrix(sub, halo):
    s = np.zeros((2 * sub, sub + 2 * halo), np.float32)
    t = np.arange(sub)
    s[t, halo + t - 1] = 1.0
    s[sub + t, halo + t + 1] = 1.0
    return jnp.asarray(s).astype(BF16)


def _conv_kernel(x_ref, prev_ref, next_ref, s_ref, w_ref, b_ref, v_ref, x1_ref, x2_ref, *, n_tiles):
    i = pl.program_id(1)
    tl = x_ref.shape[1]
    halo = prev_ref.shape[1]
    sub = s_ref.shape[0] // 2
    prev = (prev_ref[0].astype(F32) * jnp.where(i > 0, 1.0, 0.0)).astype(BF16)
    nxt = (next_ref[0].astype(F32) * jnp.where(i < n_tiles - 1, 1.0, 0.0)).astype(BF16)
    xe = jnp.concatenate([prev, x_ref[0], nxt], axis=0)
    outs = (v_ref, x1_ref, x2_ref)
    for r0 in range(0, tl, sub):
        y = jnp.dot(s_ref[...], xe[r0:r0 + sub + 2 * halo], preferred_element_type=F32)
        xc = xe[halo + r0:halo + r0 + sub].astype(F32)
        u = y[:sub] * w_ref[0:1, :] + xc * w_ref[1:2, :] + y[sub:] * w_ref[2:3, :] + b_ref[...]
        for c in range(3):
            outs[c][0, r0:r0 + sub, :] = u[:, c * HY_WIDTH:(c + 1) * HY_WIDTH].astype(BF16)


def _short_conv(h3d, conv_w, conv_b):
    b_sz, seq_len, _ = h3d.shape
    tl = CONV_TL
    halo = 16
    n_tiles = seq_len // tl
    width = 3 * HY_WIDTH
    cb = P_HY // width
    assert P_HY % width == 0 and seq_len % tl == 0
    per = tl // halo
    out = jax.ShapeDtypeStruct((b_sz, seq_len, HY_WIDTH), BF16)
    ospec = pl.BlockSpec((1, tl, HY_WIDTH), lambda b, i: (b, i, 0))
    return pl.pallas_call(
        functools.partial(_conv_kernel, n_tiles=n_tiles),
        grid=(b_sz, n_tiles),
        in_specs=[
            pl.BlockSpec((1, tl, width), lambda b, i: (b, i, cb)),
            pl.BlockSpec((1, halo, width), lambda b, i: (b, jnp.maximum(i * per - 1, 0), cb)),
            pl.BlockSpec((1, halo, width), lambda b, i: (b, jnp.minimum((i + 1) * per, seq_len // halo - 1), cb)),
            _resident((2 * CONV_SUB, CONV_SUB + 2 * halo)),
            _resident((HY_SHORT_CONV, width)),
            _resident((1, width)),
        ],
        out_specs=[ospec, ospec, ospec],
        out_shape=[out, out, out],
        compiler_params=_cparams("parallel", "arbitrary"),
        name="hyena_short_conv",
    )(h3d, h3d, h3d, _shift_matrix(CONV_SUB, halo), conv_w.astype(F32), conv_b.astype(F32).reshape(1, width))


def _filter_embedding(seq_len):
    t = np.linspace(0.0, 1.0, seq_len, dtype=np.float32).astype(np.float64)[:, None]
    w = (2.0 * math.pi * np.arange(seq_len, dtype=np.float32) / seq_len).astype(np.float32)
    bands = np.linspace(1e-4, HY_POS_BANDS - 1, HY_POS_BANDS, dtype=np.float32)
    ang = (w[:, None] * bands[None, :]).astype(np.float32).astype(np.float64)
    z = np.concatenate([t, np.cos(ang), -np.sin(ang)], axis=-1)
    zp = np.zeros((2 * seq_len, HY_EMB_PAD), np.float32)
    zp[:seq_len, :HY_EMB_DIM] = z
    zp[seq_len + 1:, :HY_EMB_DIM] = z[:0:-1]
    return zp


def _filter_deltas():
    min_decay = math.log(HY_DECAY_TARGET) / HY_SLOW_DECAY_PCT
    max_decay = math.log(HY_DECAY_TARGET) / HY_FAST_DECAY_PCT
    return np.abs(np.linspace(min_decay, max_decay, HY_WIDTH, dtype=np.float32))[None, :]


def _filter_kernel(z_ref, w1_ref, b1_ref, w2_ref, b2_ref, fr_ref, w3_ref, dl_ref, o_ref, *, seq_len):
    hi = lax.Precision.HIGHEST
    tl = z_ref.shape[0]
    z = z_ref[...]
    fr = fr_ref[...]
    h = jnp.sin(fr * (jnp.dot(z, w1_ref[...], precision=hi, preferred_element_type=F32) + b1_ref[...]))
    h = jnp.sin(fr * (jnp.dot(h, w2_ref[...], precision=hi, preferred_element_type=F32) + b2_ref[...]))
    t = z[:, 0:1]
    window = jnp.exp(-t * dl_ref[...])
    first_row = pl.program_id(0) * tl
    row = first_row + lax.broadcasted_iota(jnp.int32, (tl, 1), 0)

    def emit(direction):
        for o in range(HY_ORDER):
            c0 = (2 * o + direction) * HY_WIDTH
            f = jnp.dot(h, w3_ref[:, c0:c0 + HY_WIDTH], precision=hi, preferred_element_type=F32) * window
            if direction == 1:
                f = jnp.where(row > seq_len, f, 0.0)
            o_ref[o] = f.astype(o_ref.dtype)

    pl.when(first_row < seq_len)(lambda: emit(0))
    pl.when(first_row >= seq_len)(lambda: emit(1))


def _hyena_kernels(seq_len, w1, b1, w2, b2, freq, w3):
    tl = min(FILT_TL, seq_len)
    assert seq_len % tl == 0
    z = jnp.asarray(_filter_embedding(seq_len))
    w1p = jnp.zeros((HY_EMB_PAD, HY_FILTER_HIDDEN), F32).at[:HY_EMB_DIM].set(w1.astype(F32))
    hid = HY_FILTER_HIDDEN
    return pl.pallas_call(
        functools.partial(_filter_kernel, seq_len=seq_len),
        grid=(2 * seq_len // tl,),
        in_specs=[
            pl.BlockSpec((tl, HY_EMB_PAD), lambda i: (i, 0)),
            _resident((HY_EMB_PAD, hid)), _resident((1, hid)),
            _resident((hid, hid)), _resident((1, hid)), _resident((1, hid)),
            _resident((hid, 2 * HY_ORDER * HY_WIDTH)), _resident((1, HY_WIDTH)),
        ],
        out_specs=pl.BlockSpec((HY_ORDER, tl, HY_WIDTH), lambda i: (0, i, 0)),
        out_shape=jax.ShapeDtypeStruct((HY_ORDER, 2 * seq_len, HY_WIDTH), BF16),
        compiler_params=_cparams("parallel"),
        name="hyena_filters",
    )(z, w1p, b1.astype(F32).reshape(1, hid), w2.astype(F32), b2.astype(F32).reshape(1, hid),
      freq.astype(F32).reshape(1, hid), w3.astype(F32), jnp.asarray(_filter_deltas()))


def _fft_dims(seq_len):
    n = 2 * seq_len
    n1 = 1 << ((n.bit_length() - 1) // 2)
    n2 = n // n1
    assert n1 * n2 == n and n1 == n2, "sequence length must give a square transform"
    return n1, n2


def _stack_complex(fr, fi):
    return np.block([[fr, -fi], [fi, fr]])


@functools.lru_cache(maxsize=None)
def _dft_constants(seq_len):
    n1, n2 = _fft_dims(seq_len)
    n = n1 * n2
    k1 = np.arange(n1)[:, None].astype(np.float64)
    t1 = np.arange(n1 // 2)[None, :].astype(np.float64)
    a1 = -2.0 * math.pi * k1 * t1 / n1
    f1r, f1i = np.cos(a1), np.sin(a1)
    w1_complex = _stack_complex(f1r, f1i)
    a1_full = -2.0 * math.pi * k1 * np.arange(n1)[None, :].astype(np.float64) / n1
    w1_real = np.concatenate([np.cos(a1_full), np.sin(a1_full)], axis=0)
    k2 = np.arange(n2)[:, None].astype(np.float64)
    t2 = np.arange(n2)[None, :].astype(np.float64)
    a2 = -2.0 * math.pi * k2 * t2 / n2
    f2r, f2i = np.cos(a2), np.sin(a2)
    g1r, g1i = f1r.T / n, -f1i.T / n
    w3 = _stack_complex(g1r, g1i)
    kb = np.arange(HY_TW_SPLIT)[:, None].astype(np.float64)
    ab = -2.0 * math.pi * kb * t2 / n
    tbr, tbi = np.cos(ab), np.sin(ab)
    w2f = np.stack([_stack_complex(f2r * tbr[b] - f2i * tbi[b], f2r * tbi[b] + f2i * tbr[b])
                    for b in range(HY_TW_SPLIT)])
    w2i = np.stack([_stack_complex(tbr[b][:, None] * f2r - tbi[b][:, None] * f2i,
                                   -(tbr[b][:, None] * f2i + tbi[b][:, None] * f2r))
                    for b in range(HY_TW_SPLIT)])
    ka = (np.arange(n1 // HY_TW_SPLIT) * HY_TW_SPLIT)[:, None].astype(np.float64)
    aa = -2.0 * math.pi * ka * t2 / n
    ta = np.stack([np.cos(aa), np.sin(aa)], axis=0)[..., None]
    ta = np.broadcast_to(ta, (2, n1 // HY_TW_SPLIT, n2, LANE)).astype(np.float32)
    as_bf = lambda a: jnp.asarray(a, F32).astype(BF16)
    return dict(w1_complex=as_bf(w1_complex), w1_real=as_bf(w1_real), w2f=as_bf(w2f), w2i=as_bf(w2i),
                ta=jnp.asarray(ta), w3=as_bf(w3))


def _pack_pair(a, b):
    return pltpu.pack_elementwise([a, b], packed_dtype=BF16)


def _unpack_pair(w):
    return (pltpu.unpack_elementwise(w, index=0, packed_dtype=BF16, unpacked_dtype=F32),
            pltpu.unpack_elementwise(w, index=1, packed_dtype=BF16, unpacked_dtype=F32))


def _conv_pitch(n2):
    return n2 + 8


def _hyena_conv_kernel(z_ref, g_ref, kf_ref, w1_ref, w2f_ref, w2i_ref, w3_ref, ta_ref, bias_ref, o_ref,
                       x_scr, a_scr, *, n1, n2):
    pitch = _conv_pitch(n2)
    half = n1 // 2
    groups = n1 // HY_TW_SPLIT

    def pack_in(t1, carry):
        rows = pl.ds(pl.multiple_of(t1 * n2, n2), n2)
        x_scr[pl.ds(pl.multiple_of(t1 * pitch, 8), n2), :] = _pack_pair(
            z_ref[0, rows, :].astype(F32), z_ref[1, rows, :].astype(F32))
        return carry

    lax.fori_loop(0, half, pack_in, 0)

    def stage1(j, carry):
        cols = []
        for u in range(2):
            xr, xi = _unpack_pair(x_scr[pl.ds(2 * j + u, half, stride=pitch), :])
            cols.append(jnp.concatenate([xr, xi], axis=0).astype(BF16))
        out = jnp.dot(w1_ref[...], jnp.concatenate(cols, axis=1), preferred_element_type=F32)
        for u in range(2):
            o = out[:, u * LANE:(u + 1) * LANE]
            a_scr[pl.ds(2 * j + u, n1, stride=pitch), :] = _pack_pair(o[:n1], o[n1:])
        return carry

    lax.fori_loop(0, n2 // 2, stage1, 0, unroll=4)

    for kb in range(HY_TW_SPLIT):
        def stage2(jp, carry, kb=kb):
            rows, tws, stacks = [], [], []
            for u in range(2):
                ka = 2 * jp + u
                k1 = ka * HY_TW_SPLIT + kb
                rows.append(pl.ds(pl.multiple_of(k1 * pitch, 8), n2))
                ar, ai = _unpack_pair(a_scr[rows[u], :])
                twr, twi = ta_ref[0, ka], ta_ref[1, ka]
                tws.append((twr, twi))
                stacks.append(jnp.concatenate([ar * twr - ai * twi, ar * twi + ai * twr], axis=0).astype(BF16))
            spec = jnp.dot(w2f_ref[kb], jnp.concatenate(stacks, axis=1), preferred_element_type=F32)
            prods = []
            for u in range(2):
                k1 = (2 * jp + u) * HY_TW_SPLIT + kb
                sr, si = spec[:n2, u * LANE:(u + 1) * LANE], spec[n2:, u * LANE:(u + 1) * LANE]
                kr = kf_ref[0, 0, k1].astype(F32)
                ki = kf_ref[0, 1, k1].astype(F32)
                prods.append(jnp.concatenate([sr * kr - si * ki, sr * ki + si * kr], axis=0).astype(BF16))
            back = jnp.dot(w2i_ref[kb], jnp.concatenate(prods, axis=1), preferred_element_type=F32)
            for u in range(2):
                br, bi = back[:n2, u * LANE:(u + 1) * LANE], back[n2:, u * LANE:(u + 1) * LANE]
                twr, twi = tws[u]
                a_scr[rows[u], :] = _pack_pair(br * twr + bi * twi, bi * twr - br * twi)
            return carry

        lax.fori_loop(0, groups // 2, stage2, 0, unroll=4)

    def stage3(j, carry):
        cols = []
        for u in range(2):
            br, bi = _unpack_pair(a_scr[pl.ds(2 * j + u, n1, stride=pitch), :])
            cols.append(jnp.concatenate([br, bi], axis=0).astype(BF16))
        y = jnp.dot(w3_ref[...], jnp.concatenate(cols, axis=1), preferred_element_type=F32)
        for u in range(2):
            sl = pl.ds(2 * j + u, half, stride=pitch)
            za, zb = _unpack_pair(x_scr[sl, :])
            yu = y[:, u * LANE:(u + 1) * LANE]
            x_scr[sl, :] = _pack_pair(yu[:half] + bias_ref[...] * za, yu[half:] + bias_ref[...] * zb)
        return carry

    lax.fori_loop(0, n2 // 2, stage3, 0, unroll=4)

    def gate_out(t1, carry):
        rows = pl.ds(pl.multiple_of(t1 * n2, n2), n2)
        ta, tb = _unpack_pair(x_scr[pl.ds(pl.multiple_of(t1 * pitch, 8), n2), :])
        o_ref[0, rows, :] = (g_ref[0, rows, :].astype(F32) * ta).astype(o_ref.dtype)
        o_ref[1, rows, :] = (g_ref[1, rows, :].astype(F32) * tb).astype(o_ref.dtype)
        return carry

    lax.fori_loop(0, half, gate_out, 0)


def _hyena_conv(z, gate, kf, order, bias, consts):
    s, seq_len, c = z.shape
    n1, n2 = _fft_dims(seq_len)
    pitch = _conv_pitch(n2)
    assert s % 2 == 0 and c % LANE == 0 and (n1 // HY_TW_SPLIT) % 2 == 0
    seq_blk = pl.BlockSpec((2, seq_len, LANE), lambda cc, p: (p, 0, cc))
    return pl.pallas_call(
        functools.partial(_hyena_conv_kernel, n1=n1, n2=n2),
        grid=(c // LANE, s // 2),
        in_specs=[
            seq_blk,
            seq_blk,
            pl.BlockSpec((1, 2, n1, n2, LANE), lambda cc, p: (order, 0, 0, 0, cc), pipeline_mode=pl.Buffered(1)),
            _resident(consts["w1_complex"].shape),
            _resident(consts["w2f"].shape),
            _resident(consts["w2i"].shape),
            _resident(consts["w3"].shape),
            _resident(consts["ta"].shape),
            pl.BlockSpec((1, LANE), lambda cc, p: (0, cc)),
        ],
        out_specs=seq_blk,
        out_shape=jax.ShapeDtypeStruct(z.shape, BF16),
        scratch_shapes=[pltpu.VMEM((n1 // 2 * pitch, LANE), jnp.uint32),
                        pltpu.VMEM((n1 * pitch, LANE), jnp.uint32)],
        compiler_params=_cparams("parallel", "arbitrary"),
        name="hyena_conv",
    )(z, gate, kf, consts["w1_complex"], consts["w2f"], consts["w2i"], consts["w3"], consts["ta"],
      bias.astype(F32).reshape(1, c))


def _spectrum_kernel(k_ref, w1_ref, w2f_ref, ta_ref, o_ref, x_scr, a_scr, *, n1, n2):
    pitch = _conv_pitch(n2)
    groups = n1 // HY_TW_SPLIT

    def pack_in(t1, carry):
        x_scr[pl.ds(pl.multiple_of(t1 * pitch, 8), n2), :] = (
            k_ref[0, pl.ds(pl.multiple_of(t1 * n2, n2), n2), :].astype(F32))
        return carry

    lax.fori_loop(0, n1, pack_in, 0)

    def stage1(j, carry):
        cols = [x_scr[pl.ds(2 * j + u, n1, stride=pitch), :].astype(BF16) for u in range(2)]
        out = jnp.dot(w1_ref[...], jnp.concatenate(cols, axis=1), preferred_element_type=F32)
        for u in range(2):
            o = out[:, u * LANE:(u + 1) * LANE]
            a_scr[pl.ds(2 * j + u, n1, stride=pitch), :] = _pack_pair(o[:n1], o[n1:])
        return carry

    lax.fori_loop(0, n2 // 2, stage1, 0, unroll=4)

    for kb in range(HY_TW_SPLIT):
        def stage2(jp, carry, kb=kb):
            stacks = []
            for u in range(2):
                ka = 2 * jp + u
                k1 = ka * HY_TW_SPLIT + kb
                ar, ai = _unpack_pair(a_scr[pl.ds(pl.multiple_of(k1 * pitch, 8), n2), :])
                twr, twi = ta_ref[0, ka], ta_ref[1, ka]
                stacks.append(jnp.concatenate([ar * twr - ai * twi, ar * twi + ai * twr], axis=0).astype(BF16))
            spec = jnp.dot(w2f_ref[kb], jnp.concatenate(stacks, axis=1), preferred_element_type=F32)
            for u in range(2):
                k1 = (2 * jp + u) * HY_TW_SPLIT + kb
                o_ref[0, 0, k1] = spec[:n2, u * LANE:(u + 1) * LANE].astype(o_ref.dtype)
                o_ref[0, 1, k1] = spec[n2:, u * LANE:(u + 1) * LANE].astype(o_ref.dtype)
            return carry

        lax.fori_loop(0, groups // 2, stage2, 0, unroll=2)


def _filter_spectrum(kern, consts):
    _, n, c = kern.shape
    n1, n2 = _fft_dims(n // 2)
    pitch = _conv_pitch(n2)
    return pl.pallas_call(
        functools.partial(_spectrum_kernel, n1=n1, n2=n2),
        grid=(HY_ORDER, c // LANE),
        in_specs=[
            pl.BlockSpec((1, n, LANE), lambda o, cc: (o, 0, cc), pipeline_mode=pl.Buffered(1)),
            _resident(consts["w1_real"].shape),
            _resident(consts["w2f"].shape),
            _resident(consts["ta"].shape),
        ],
        out_specs=pl.BlockSpec((1, 2, n1, n2, LANE), lambda o, cc: (o, 0, 0, 0, cc)),
        out_shape=jax.ShapeDtypeStruct((HY_ORDER, 2, n1, n2, c), BF16),
        scratch_shapes=[pltpu.VMEM((n1 * pitch, LANE), F32), pltpu.VMEM((n1 * pitch, LANE), jnp.uint32)],
        compiler_params=_cparams("parallel", "arbitrary"),
        name="hyena_filter_spectrum",
    )(kern, consts["w1_real"], consts["w2f"], consts["ta"])


def _hyena_spectra(seq_len, w1, b1, w2, b2, freq, w3):
    consts = _dft_constants(seq_len)
    return _filter_spectrum(_hyena_kernels(seq_len, w1, b1, w2, b2, freq, w3), consts)


def _hyena(h3d, conv_w, conv_b, kf, hy_bias):
    seq_len = h3d.shape[1]
    consts = _dft_constants(seq_len)
    v, x1, x2 = _short_conv(h3d, conv_w, conv_b)
    z = v
    for n, gate in enumerate((x1, x2)):
        z = _hyena_conv(z, gate, kf, n, hy_bias[n], consts)
    return z


def _layernorm(y, g, b):
    mu = jnp.mean(y, axis=-1, keepdims=True)
    d = y - mu
    var = jnp.mean(d * d, axis=-1, keepdims=True)
    return d * lax.rsqrt(var + LN_EPS) * g + b


def _merge_kernel(g_ref, a_ref, hb_ref, c_ref, x_ref, wa_ref, wb_ref, wc_ref, wo_ref, lg_ref, lb_ref, o_ref):
    d = D_MODEL
    subs = [slice(r, r + MERGE_SUB) for r in range(0, x_ref.shape[0], MERGE_SUB)]
    merged = []
    for r in subs:
        acc = g_ref[r, 0:d].astype(F32) * jnp.dot(a_ref[r, :], wa_ref[...], preferred_element_type=F32)
        acc += g_ref[r, d:2 * d].astype(F32) * jnp.dot(hb_ref[r, :], wb_ref[...], preferred_element_type=F32)
        acc += g_ref[r, 2 * d:3 * d].astype(F32) * jnp.dot(c_ref[r, :], wc_ref[...], preferred_element_type=F32)
        merged.append(acc.astype(BF16))
    for r, mg in zip(subs, merged):
        mix = jnp.dot(mg, wo_ref[...], preferred_element_type=F32)
        o_ref[r, :] = _layernorm(DEEPNORM_ALPHA * x_ref[r, :] + mix, lg_ref[...], lb_ref[...])


def _merge(h2d, a, hb, c, x2d, wa, wb, wc, wo, ln_g, ln_b):
    m = x2d.shape[0]
    tm = MERGE_TM
    row = lambda width: pl.BlockSpec((tm, width), lambda i: (i, 0))
    return pl.pallas_call(
        _merge_kernel,
        grid=(m // tm,),
        in_specs=[
            row(N_BRANCH * D_MODEL), row(NA_WIDTH), row(HY_WIDTH), row(SWA_WIDTH), row(D_MODEL),
            _resident((NA_WIDTH, D_MODEL)), _resident((HY_WIDTH, D_MODEL)), _resident((SWA_WIDTH, D_MODEL)),
            _resident((D_MODEL, D_MODEL)), _resident((1, D_MODEL)), _resident((1, D_MODEL)),
        ],
        out_specs=row(D_MODEL),
        out_shape=jax.ShapeDtypeStruct((m, D_MODEL), F32),
        compiler_params=_cparams("parallel"),
        name="merge_ln",
    )(h2d, a, hb, c, x2d, wa, wb, wc, wo, ln_g, ln_b)


def _mlp_kernel(x_ref, wu_ref, bu_ref, wd_ref, bd_ref, lg_ref, lb_ref, o_ref):
    for r0 in range(0, x_ref.shape[0], MLP_SUB):
        rows = slice(r0, r0 + MLP_SUB)
        x = x_ref[rows, :]
        xb = x.astype(BF16)
        acc = DEEPNORM_ALPHA * x + bd_ref[...]
        for c in range(D_FF // MLP_FF_CHUNK):
            sl = slice(c * MLP_FF_CHUNK, (c + 1) * MLP_FF_CHUNK)
            up = jnp.dot(xb, wu_ref[:, sl], preferred_element_type=F32) + bu_ref[:, sl]
            up = jnp.square(jnp.maximum(up, 0.0))
            acc += jnp.dot(up.astype(BF16), wd_ref[sl, :], preferred_element_type=F32)
        o_ref[rows, :] = _layernorm(acc, lg_ref[...], lb_ref[...])


def _mlp(x2d, wu, bu, wd, bd, ln_g, ln_b):
    m = x2d.shape[0]
    tm = MLP_TM
    row = pl.BlockSpec((tm, D_MODEL), lambda i: (i, 0))
    return pl.pallas_call(
        _mlp_kernel,
        grid=(m // tm,),
        in_specs=[
            row, _resident((D_MODEL, D_FF)), _resident((1, D_FF)), _resident((D_FF, D_MODEL)),
            _resident((1, D_MODEL)), _resident((1, D_MODEL)), _resident((1, D_MODEL)),
        ],
        out_specs=row,
        out_shape=jax.ShapeDtypeStruct((m, D_MODEL), F32),
        compiler_params=_cparams("parallel"),
        name="mlp_ln",
    )(x2d, wu, bu, wd, bd, ln_g, ln_b)


def _permute_in_columns(a):
    q0 = OFF_SWA
    order = [h for j in range(SWA_GROUP) for h in (j, j + SWA_GROUP)]
    swa_q = [a[..., q0 + h * HEAD_DIM:q0 + (h + 1) * HEAD_DIM] for h in order]
    swa_k = a[..., q0 + SWA_WIDTH:q0 + SWA_WIDTH + SWA_KV_WIDTH]
    return jnp.concatenate([a[..., OFF_GATE:], a[..., OFF_HY:OFF_SWA], a[..., :2 * NA_WIDTH]] + swa_q + [swa_k],
                           axis=-1)


def _scale_query_columns(a):
    c = HEAD_DIM ** -0.5 * LOG2E
    return jnp.concatenate([a[..., :NA_WIDTH] * c, a[..., NA_WIDTH:OFF_SWA],
                            a[..., OFF_SWA:OFF_SWA + SWA_WIDTH] * c, a[..., OFF_SWA + SWA_WIDTH:]], axis=-1)


def _values_columns(a):
    v_swa = OFF_SWA + SWA_WIDTH + SWA_KV_WIDTH
    return jnp.concatenate([a[..., 2 * NA_WIDTH:3 * NA_WIDTH], a[..., v_swa:v_swa + SWA_KV_WIDTH]], axis=-1)


def _prepare_layer(l, p):
    row = lambda a: a.astype(F32).reshape(1, -1)
    return dict(
        w_in=_permute_in_columns(_scale_query_columns(p["w_in"][l])).astype(BF16),
        b_in=row(_permute_in_columns(_scale_query_columns(p["b_in"][l]))),
        conv_w=p["hy_conv_w"][l], conv_b=p["hy_conv_b"][l], hy_bias=p["hy_bias"][l],
        filt=(p["hy_filt_w1"][l], p["hy_filt_b1"][l], p["hy_filt_w2"][l], p["hy_filt_b2"][l],
              p["hy_filt_freq"][l], p["hy_filt_w3"][l]),
        w_vt=_values_columns(p["w_in"][l]).T.astype(BF16),
        b_vt=_values_columns(p["b_in"][l]).astype(F32).reshape(VT_WIDTH, 1),
        na_bias=_na_bias_table(p["na_rpb"][l]),
        sink=p["swa_sink"][l],
        wa=p["w_branch_a"][l].astype(BF16), wb=p["w_branch_b"][l].astype(BF16),
        wc=p["w_branch_c"][l].astype(BF16), wo=p["w_out"][l].astype(BF16),
        ln1_g=row(p["ln1_g"][l]), ln1_b=row(p["ln1_b"][l]),
        wu=p["w_up"][l].astype(BF16), bu=row(p["b_up"][l]),
        wd=p["w_down"][l].astype(BF16), bd=row(p["b_down"][l]),
        ln2_g=row(p["ln2_g"][l]), ln2_b=row(p["ln2_b"][l]),
    )


def _encoder_block(x, lp, swa_table):
    b_sz, seq_len, _ = x.shape
    m = b_sz * seq_len
    x2d = x.reshape(m, D_MODEL)
    h2d, vt = _inproj(x2d, lp["w_in"], lp["b_in"], lp["w_vt"], lp["b_vt"])
    h3d = h2d.reshape(b_sz, seq_len, D_TOK)
    a = _na_attention(h3d, vt, lp["na_bias"])
    kf = _hyena_spectra(seq_len, *lp["filt"])
    hb = _hyena(h3d, lp["conv_w"], lp["conv_b"], kf, lp["hy_bias"])
    c = _swa_attention(h3d, vt, lp["sink"], swa_table)
    x1 = _merge(h2d, a.reshape(m, NA_WIDTH), hb.reshape(m, HY_WIDTH), c.reshape(m, SWA_WIDTH), x2d,
                lp["wa"], lp["wb"], lp["wc"], lp["wo"], lp["ln1_g"], lp["ln1_b"])
    x2 = _mlp(x1, lp["wu"], lp["bu"], lp["wd"], lp["bd"], lp["ln2_g"], lp["ln2_b"])
    return x2.reshape(b_sz, seq_len, D_MODEL)


def kernel(x_prompt, x_sample, w_in, b_in, hy_conv_w, hy_conv_b, hy_filt_w1, hy_filt_b1, hy_filt_w2,
           hy_filt_b2, hy_filt_freq, hy_filt_w3, hy_bias, na_rpb, swa_sink, w_branch_a, w_branch_b,
           w_branch_c, w_out, ln1_g, ln1_b, w_up, b_up, w_down, b_down, ln2_g, ln2_b):
    params = dict(w_in=w_in, b_in=b_in, hy_conv_w=hy_conv_w, hy_conv_b=hy_conv_b, hy_filt_w1=hy_filt_w1,
                  hy_filt_b1=hy_filt_b1, hy_filt_w2=hy_filt_w2, hy_filt_b2=hy_filt_b2,
                  hy_filt_freq=hy_filt_freq, hy_filt_w3=hy_filt_w3, hy_bias=hy_bias, na_rpb=na_rpb,
                  swa_sink=swa_sink, w_branch_a=w_branch_a, w_branch_b=w_branch_b, w_branch_c=w_branch_c,
                  w_out=w_out, ln1_g=ln1_g, ln1_b=ln1_b, w_up=w_up, b_up=b_up, w_down=w_down,
                  b_down=b_down, ln2_g=ln2_g, ln2_b=ln2_b)
    swa_table = _swa_table()
    y_prompt = x_prompt
    y_sample = x_sample
    for l in range(DEPTH):
        lp = _prepare_layer(l, params)
        y_prompt = _encoder_block(y_prompt, lp, swa_table)
        y_sample = _encoder_block(y_sample, lp, swa_table)
    return (y_prompt, y_sample)
```

```python
import functools
import math

import numpy as np
import jax
import jax.numpy as jnp
from jax import lax
from jax.experimental import pallas as pl
from jax.experimental.pallas import tpu as pltpu

F32 = jnp.float32
BF16 = jnp.bfloat16

D_MODEL = 1024
DEPTH = 2
HEAD_DIM = 64
GRID_W = 64
NA_HEADS = 8
NA_WIN_ROWS = 8
NA_WIN_COLS = 16
NA_WIDTH = NA_HEADS * HEAD_DIM
HY_WIDTH = D_MODEL // 2
HY_ORDER = 2
HY_SHORT_CONV = 3
HY_POS_BANDS = 16
HY_EMB_DIM = 1 + 2 * HY_POS_BANDS
HY_EMB_PAD = 128
HY_FILTER_HIDDEN = 64
HY_FAST_DECAY_PCT = 0.3
HY_SLOW_DECAY_PCT = 1.5
HY_DECAY_TARGET = 1e-2
SWA_HEADS = 8
SWA_KV_HEADS = 2
SWA_GROUP = SWA_HEADS // SWA_KV_HEADS
SWA_WIDTH = SWA_HEADS * HEAD_DIM
SWA_KV_WIDTH = SWA_KV_HEADS * HEAD_DIM
SWA_WINDOW = 128
SWA_BLOCK = 128
N_BRANCH = 3
D_FF = 4 * D_MODEL
OFF_HY = 3 * NA_WIDTH
OFF_SWA = OFF_HY + 3 * HY_WIDTH
OFF_GATE = OFF_SWA + SWA_WIDTH + 2 * SWA_KV_WIDTH
D_IN = OFF_GATE + N_BRANCH * D_MODEL
DEEPNORM_ALPHA = (2 * DEPTH) ** 0.25
LN_EPS = 1e-5
NEG_BIG = -1e30
LOG2E = 1.4426950408889634

P_GATE = 0
P_HY = N_BRANCH * D_MODEL
P_NA = P_HY + 3 * HY_WIDTH
P_SWA = P_NA + 2 * NA_WIDTH
D_TOK = P_SWA + SWA_WIDTH + SWA_KV_WIDTH
VT_WIDTH = NA_WIDTH + SWA_KV_WIDTH

LANE = 128
SUBLANE_PACK = 16
VMEM_LIMIT_BYTES = 56 * 1024 * 1024

NA_PAIR_ROWS = NA_WIN_ROWS + 2
ATTN_BLOCKS_PER_STEP = 4
INPROJ_TM = 512
INPROJ_TN = 768
MERGE_TM = 1024
MERGE_SUB = 256
MLP_TM = 1024
MLP_SUB = 512
MLP_FF_CHUNK = 1024
CONV_TL = 512
CONV_SUB = 128
FILT_TL = 512
HY_TW_SPLIT = 8


def _cparams(*sem):
    return pltpu.CompilerParams(dimension_semantics=sem, vmem_limit_bytes=VMEM_LIMIT_BYTES)


def _resident(shape):
    nd = len(shape)
    return pl.BlockSpec(shape, lambda *_: (0,) * nd, pipeline_mode=pl.Buffered(1))


def _inproj_kernel(x_ref, w_ref, b_ref, wvt_ref, bvt_ref, o_ref, vt_ref):
    xb = x_ref[...].astype(BF16)
    vt = lax.dot_general(wvt_ref[...], xb, (((1,), (1,)), ((), ())), preferred_element_type=F32)
    vt_ref[...] = (vt + bvt_ref[...]).astype(vt_ref.dtype)
    for c0 in range(0, D_TOK, INPROJ_TN):
        cols = slice(c0, min(c0 + INPROJ_TN, D_TOK))
        acc = jnp.dot(xb, w_ref[:, cols], preferred_element_type=F32) + b_ref[:, cols]
        if cols.stop <= P_HY:
            acc = 1.0 / (1.0 + jnp.exp(-acc))
        o_ref[:, cols] = acc.astype(o_ref.dtype)


def _inproj(x2d, w_bf, b_row, wvt_bf, bvt_col):
    m = x2d.shape[0]
    tm = min(INPROJ_TM, m)
    assert m % tm == 0 and P_HY % INPROJ_TN == 0
    return pl.pallas_call(
        _inproj_kernel,
        grid=(m // tm,),
        in_specs=[
            pl.BlockSpec((tm, D_MODEL), lambda i: (i, 0)),
            _resident((D_MODEL, D_TOK)),
            _resident((1, D_TOK)),
            _resident((VT_WIDTH, D_MODEL)),
            _resident((VT_WIDTH, 1)),
        ],
        out_specs=[pl.BlockSpec((tm, D_TOK), lambda i: (i, 0)),
                   pl.BlockSpec((VT_WIDTH, tm), lambda i: (0, i))],
        out_shape=[jax.ShapeDtypeStruct((m, D_TOK), BF16), jax.ShapeDtypeStruct((VT_WIDTH, m), BF16)],
        compiler_params=_cparams("parallel"),
        name="inproj",
    )(x2d, w_bf, b_row, wvt_bf, bvt_col)


def _na_bias_table(rpb):
    kr = NA_WIN_ROWS
    pad = GRID_W - NA_WIN_COLS
    p = jnp.pad(rpb.astype(F32) * LOG2E, ((0, 0), (0, 0), (pad, pad)))
    cols = jnp.stack([p[:, :, GRID_W - 1 - w:2 * GRID_W - 1 - w] for w in range(GRID_W)], axis=2)
    t = jnp.stack([cols[:, kr - 1 - d:2 * kr - 1 - d] for d in range(kr)], axis=0)
    w = np.arange(GRID_W)[:, None]
    kc = np.arange(GRID_W)[None, :]
    col_start = np.clip(w - NA_WIN_COLS // 2, 0, GRID_W - NA_WIN_COLS)
    valid = (kc >= col_start) & (kc < col_start + NA_WIN_COLS)
    t = jnp.where(valid[None, None, None], t, NEG_BIG)
    t = jnp.transpose(t, (0, 1, 2, 4, 3))
    return t.reshape(kr, NA_HEADS, kr * GRID_W, GRID_W)


def _na_pair_geometry(p, rows):
    base = min(max(2 * p - NA_WIN_ROWS // 2, 0), rows - NA_PAIR_ROWS)
    geo = []
    for j in range(2):
        r = 2 * p + j
        rs = min(max(r - NA_WIN_ROWS // 2, 0), rows - NA_WIN_ROWS)
        assert 0 <= rs - base <= NA_PAIR_ROWS - NA_WIN_ROWS
        geo.append((rs - base, r - rs))
    return base, geo


def _na_pair_table(table, rows):
    n_pairs = rows // 2
    variants = []
    for p in (0, 1, 2, n_pairs - 2, n_pairs - 1):
        _, geo = _na_pair_geometry(p, rows)
        cols = []
        for off, delta in geo:
            after = NA_PAIR_ROWS - NA_WIN_ROWS - off
            cols.append(jnp.pad(table[delta], ((0, 0), (off * GRID_W, after * GRID_W), (0, 0)),
                                constant_values=NEG_BIG))
        variants.append(jnp.concatenate(cols, axis=-1))
    return jnp.stack(variants, axis=0)


def _scores_t_pair(k, q):
    lane = lax.broadcasted_iota(jnp.int32, (1, LANE), 1)
    zero = jnp.zeros((), q.dtype)
    q_both = jnp.concatenate([jnp.where(lane < HEAD_DIM, q, zero), jnp.where(lane >= HEAD_DIM, q, zero)], axis=0)
    s = lax.dot_general(k, q_both, (((1,), (1,)), ((), ())), preferred_element_type=F32)
    return s[:, :q.shape[0]], s[:, q.shape[0]:]


def _weights_t(s, sink=None):
    m = jnp.max(s, axis=0, keepdims=True)
    if sink is None:
        return jnp.exp2(s - m).astype(BF16), None
    m = jnp.maximum(m, sink)
    return jnp.exp2(s - m).astype(BF16), jnp.exp2(sink - m)


def _weighted_values_t(vt, p, sink_term=None):
    ones = jnp.ones((SUBLANE_PACK, vt.shape[1]), vt.dtype)
    acc = jnp.dot(jnp.concatenate([vt, ones], axis=0), p, preferred_element_type=F32)
    total = acc[vt.shape[0]:vt.shape[0] + 1]
    if sink_term is not None:
        total = total + sink_term
    return acc[:vt.shape[0]] / total


def _na_kernel(q_ref, k_ref, vt_ref, *rest, rows, pairs_per_step):
    bias_refs, o_ref = rest[:-1], rest[-1]
    span = NA_PAIR_ROWS * GRID_W
    work = []
    for u in range(pairs_per_step):
        pair = pl.program_id(1) * pairs_per_step + u
        base = jnp.clip(2 * pair - NA_WIN_ROWS // 2, 0, rows - NA_PAIR_ROWS)
        start = pl.multiple_of(base * GRID_W, 2 * GRID_W)
        q = q_ref[0, u * LANE:(u + 1) * LANE, :]
        k = k_ref[0, pl.ds(start, span), :]
        vt = vt_ref[:, pl.ds(start, span)]
        for j in range(NA_HEADS // 2):
            grp = slice(j * LANE, (j + 1) * LANE)
            for h, s in zip((2 * j, 2 * j + 1), _scores_t_pair(k[:, grp], q[:, grp])):
                work.append((vt[h * HEAD_DIM:(h + 1) * HEAD_DIM, :],) + _weights_t(s + bias_refs[u][0, h]))
    outs = [_weighted_values_t(*w) for w in work]
    for u in range(pairs_per_step):
        o_t = jnp.concatenate(outs[u * NA_HEADS:(u + 1) * NA_HEADS], axis=0)
        o_ref[0, u * LANE:(u + 1) * LANE, :] = o_t.T.astype(o_ref.dtype)


def _na_attention(h3d, vt, bias_table):
    b_sz, seq_len, _ = h3d.shape
    rows = seq_len // GRID_W
    n_pairs = rows // 2
    assert rows % 2 == 0 and n_pairs >= 5
    cb = P_NA // NA_WIDTH
    span = NA_PAIR_ROWS * GRID_W
    pair_table = _na_pair_table(bias_table, rows)

    def variant(p):
        return jnp.where(p < 2, p, jnp.where(p > n_pairs - 3, p - (n_pairs - 5), 2))

    pps = ATTN_BLOCKS_PER_STEP
    assert n_pairs % pps == 0 and 2 * GRID_W == LANE
    bias_specs = [pl.BlockSpec((1, NA_HEADS, span, LANE), lambda b, p, u=u: (variant(p * pps + u), 0, 0, 0))
                  for u in range(pps)]
    return pl.pallas_call(
        functools.partial(_na_kernel, rows=rows, pairs_per_step=pps),
        grid=(b_sz, n_pairs // pps),
        in_specs=[
            pl.BlockSpec((1, pps * LANE, NA_WIDTH), lambda b, p: (b, p, cb)),
            pl.BlockSpec((1, seq_len, NA_WIDTH), lambda b, p: (b, 0, cb + 1), pipeline_mode=pl.Buffered(1)),
            pl.BlockSpec((NA_WIDTH, seq_len), lambda b, p: (0, b), pipeline_mode=pl.Buffered(1)),
        ] + bias_specs,
        out_specs=pl.BlockSpec((1, pps * LANE, NA_WIDTH), lambda b, p: (b, p, 0)),
        out_shape=jax.ShapeDtypeStruct((b_sz, seq_len, NA_WIDTH), BF16),
        compiler_params=_cparams("parallel", "arbitrary"),
        name="na_attention",
    )(h3d, h3d, vt, *([pair_table] * pps))


def _swa_table():
    span = SWA_BLOCK + 2 * SWA_WINDOW
    slopes = 2.0 ** (-8.0 * (np.arange(SWA_HEADS, dtype=np.float64) + 1.0) / SWA_HEADS)
    kk = np.arange(span)[:, None]
    t = np.arange(SWA_BLOCK)[None, :]
    variants = []
    for off in (0, SWA_BLOCK, 2 * SWA_BLOCK):
        rel = np.abs(kk - off - t)
        a = -slopes[:, None, None] * rel[None].astype(np.float64) * LOG2E
        variants.append(np.where((rel <= SWA_WINDOW)[None], a, NEG_BIG))
    return jnp.asarray(np.stack(variants), F32)


def _swa_kernel(sink_ref, q_ref, k_ref, vt_ref, *rest, seq_len, blocks_per_step):
    tab_refs, o_ref = rest[:-1], rest[-1]
    span = SWA_BLOCK + 2 * SWA_WINDOW
    work = []
    for u in range(blocks_per_step):
        blk = pl.program_id(1) * blocks_per_step + u
        start = pl.multiple_of(jnp.clip((blk - 1) * SWA_BLOCK, 0, seq_len - span), SWA_BLOCK)
        q = q_ref[0, u * SWA_BLOCK:(u + 1) * SWA_BLOCK, :]
        k = k_ref[0, pl.ds(start, span), :]
        vt = vt_ref[:, pl.ds(start, span)]
        per_head = {}
        for j in range(SWA_GROUP):
            grp = slice(j * LANE, (j + 1) * LANE)
            for g, s in enumerate(_scores_t_pair(k, q[:, grp])):
                h = j + g * SWA_GROUP
                per_head[h] = ((vt[g * HEAD_DIM:(g + 1) * HEAD_DIM, :],)
                               + _weights_t(s + tab_refs[u][0, h], sink_ref[h] * LOG2E))
        work.extend(per_head[h] for h in range(SWA_HEADS))
    outs = [_weighted_values_t(*w) for w in work]
    for u in range(blocks_per_step):
        o_t = jnp.concatenate(outs[u * SWA_HEADS:(u + 1) * SWA_HEADS], axis=0)
        o_ref[0, u * SWA_BLOCK:(u + 1) * SWA_BLOCK, :] = o_t.T.astype(o_ref.dtype)


def _swa_attention(h3d, vt, sink, table):
    b_sz, seq_len, _ = h3d.shape
    nb = seq_len // SWA_BLOCK
    span = SWA_BLOCK + 2 * SWA_WINDOW
    assert SWA_KV_HEADS == 2 and SWA_BLOCK == LANE and nb >= 3
    qb = P_SWA // SWA_WIDTH
    kb = (P_SWA + SWA_WIDTH) // SWA_KV_WIDTH

    def variant(i):
        return jnp.where(i == 0, 0, jnp.where(i == nb - 1, 2, 1))

    bps = ATTN_BLOCKS_PER_STEP
    assert nb % bps == 0
    tab_specs = [pl.BlockSpec((1, SWA_HEADS, span, SWA_BLOCK), lambda b, i, u=u: (variant(i * bps + u), 0, 0, 0))
                 for u in range(bps)]
    return pl.pallas_call(
        functools.partial(_swa_kernel, seq_len=seq_len, blocks_per_step=bps),
        grid=(b_sz, nb // bps),
        in_specs=[
            pl.BlockSpec(memory_space=pltpu.SMEM),
            pl.BlockSpec((1, bps * SWA_BLOCK, SWA_WIDTH), lambda b, i: (b, i, qb)),
            pl.BlockSpec((1, seq_len, SWA_KV_WIDTH), lambda b, i: (b, 0, kb)),
            pl.BlockSpec((SWA_KV_WIDTH, seq_len), lambda b, i: (NA_WIDTH // SWA_KV_WIDTH, b)),
        ] + tab_specs,
        out_specs=pl.BlockSpec((1, bps * SWA_BLOCK, SWA_WIDTH), lambda b, i: (b, i, 0)),
        out_shape=jax.ShapeDtypeStruct((b_sz, seq_len, SWA_WIDTH), BF16),
        compiler_params=_cparams("parallel", "arbitrary"),
        name="swa_attention",
    )(sink.astype(F32), h3d, h3d, vt, *([table] * bps))


def _shift_matrix(sub, halo):
    s = np.zeros((2 * sub, sub + 2 * halo), np.float32)
    t = np.arange(sub)
    s[t, halo + t - 1] = 1.0
    s[sub + t, halo + t + 1] = 1.0
    return jnp.asarray(s).astype(BF16)


def _conv_kernel(x_ref, prev_ref, next_ref, s_ref, w_ref, b_ref, v_ref, x1_ref, x2_ref, *, n_tiles):
    i = pl.program_id(1)
    tl = x_ref.shape[1]
    halo = prev_ref.shape[1]
    sub = s_ref.shape[0] // 2
    prev = (prev_ref[0].astype(F32) * jnp.where(i > 0, 1.0, 0.0)).astype(BF16)
    nxt = (next_ref[0].astype(F32) * jnp.where(i < n_tiles - 1, 1.0, 0.0)).astype(BF16)
    xe = jnp.concatenate([prev, x_ref[0], nxt], axis=0)
    outs = (v_ref, x1_ref, x2_ref)
    for r0 in range(0, tl, sub):
        y = jnp.dot(s_ref[...], xe[r0:r0 + sub + 2 * halo], preferred_element_type=F32)
        xc = xe[halo + r0:halo + r0 + sub].astype(F32)
        u = y[:sub] * w_ref[0:1, :] + xc * w_ref[1:2, :] + y[sub:] * w_ref[2:3, :] + b_ref[...]
        for c in range(3):
            outs[c][0, r0:r0 + sub, :] = u[:, c * HY_WIDTH:(c + 1) * HY_WIDTH].astype(BF16)


def _short_conv(h3d, conv_w, conv_b):
    b_sz, seq_len, _ = h3d.shape
    tl = CONV_TL
    halo = 16
    n_tiles = seq_len // tl
    width = 3 * HY_WIDTH
    cb = P_HY // width
    assert P_HY % width == 0 and seq_len % tl == 0
    per = tl // halo
    out = jax.ShapeDtypeStruct((b_sz, seq_len, HY_WIDTH), BF16)
    ospec = pl.BlockSpec((1, tl, HY_WIDTH), lambda b, i: (b, i, 0))
    return pl.pallas_call(
        functools.partial(_conv_kernel, n_tiles=n_tiles),
        grid=(b_sz, n_tiles),
        in_specs=[
            pl.BlockSpec((1, tl, width), lambda b, i: (b, i, cb)),
            pl.BlockSpec((1, halo, width), lambda b, i: (b, jnp.maximum(i * per - 1, 0), cb)),
            pl.BlockSpec((1, halo, width), lambda b, i: (b, jnp.minimum((i + 1) * per, seq_len // halo - 1), cb)),
            _resident((2 * CONV_SUB, CONV_SUB + 2 * halo)),
            _resident((HY_SHORT_CONV, width)),
            _resident((1, width)),
        ],
        out_specs=[ospec, ospec, ospec],
        out_shape=[out, out, out],
        compiler_params=_cparams("parallel", "arbitrary"),
        name="hyena_short_conv",
    )(h3d, h3d, h3d, _shift_matrix(CONV_SUB, halo), conv_w.astype(F32), conv_b.astype(F32).reshape(1, width))


def _filter_embedding(seq_len):
    t = np.linspace(0.0, 1.0, seq_len, dtype=np.float32).astype(np.float64)[:, None]
    w = (2.0 * math.pi * np.arange(seq_len, dtype=np.float32) / seq_len).astype(np.float32)
    bands = np.linspace(1e-4, HY_POS_BANDS - 1, HY_POS_BANDS, dtype=np.float32)
    ang = (w[:, None] * bands[None, :]).astype(np.float32).astype(np.float64)
    z = np.concatenate([t, np.cos(ang), -np.sin(ang)], axis=-1)
    zp = np.zeros((2 * seq_len, HY_EMB_PAD), np.float32)
    zp[:seq_len, :HY_EMB_DIM] = z
    zp[seq_len + 1:, :HY_EMB_DIM] = z[:0:-1]
    return zp


def _filter_deltas():
    min_decay = math.log(HY_DECAY_TARGET) / HY_SLOW_DECAY_PCT
    max_decay = math.log(HY_DECAY_TARGET) / HY_FAST_DECAY_PCT
    return np.abs(np.linspace(min_decay, max_decay, HY_WIDTH, dtype=np.float32))[None, :]


def _filter_kernel(z_ref, w1_ref, b1_ref, w2_ref, b2_ref, fr_ref, w3_ref, dl_ref, o_ref, *, seq_len):
    hi = lax.Precision.HIGHEST
    tl = z_ref.shape[0]
    z = z_ref[...]
    fr = fr_ref[...]
    h = jnp.sin(fr * (jnp.dot(z, w1_ref[...], precision=hi, preferred_element_type=F32) + b1_ref[...]))
    h = jnp.sin(fr * (jnp.dot(h, w2_ref[...], precision=hi, preferred_element_type=F32) + b2_ref[...]))
    t = z[:, 0:1]
    window = jnp.exp(-t * dl_ref[...])
    first_row = pl.program_id(0) * tl
    row = first_row + lax.broadcasted_iota(jnp.int32, (tl, 1), 0)

    def emit(direction):
        for o in range(HY_ORDER):
            c0 = (2 * o + direction) * HY_WIDTH
            f = jnp.dot(h, w3_ref[:, c0:c0 + HY_WIDTH], precision=hi, preferred_element_type=F32) * window
            if direction == 1:
                f = jnp.where(row > seq_len, f, 0.0)
            o_ref[o] = f.astype(o_ref.dtype)

    pl.when(first_row < seq_len)(lambda: emit(0))
    pl.when(first_row >= seq_len)(lambda: emit(1))


def _hyena_kernels(seq_len, w1, b1, w2, b2, freq, w3):
    tl = min(FILT_TL, seq_len)
    assert seq_len % tl == 0
    z = jnp.asarray(_filter_embedding(seq_len))
    w1p = jnp.zeros((HY_EMB_PAD, HY_FILTER_HIDDEN), F32).at[:HY_EMB_DIM].set(w1.astype(F32))
    hid = HY_FILTER_HIDDEN
    return pl.pallas_call(
        functools.partial(_filter_kernel, seq_len=seq_len),
        grid=(2 * seq_len // tl,),
        in_specs=[
            pl.BlockSpec((tl, HY_EMB_PAD), lambda i: (i, 0)),
            _resident((HY_EMB_PAD, hid)), _resident((1, hid)),
            _resident((hid, hid)), _resident((1, hid)), _resident((1, hid)),
            _resident((hid, 2 * HY_ORDER * HY_WIDTH)), _resident((1, HY_WIDTH)),
        ],
        out_specs=pl.BlockSpec((HY_ORDER, tl, HY_WIDTH), lambda i: (0, i, 0)),
        out_shape=jax.ShapeDtypeStruct((HY_ORDER, 2 * seq_len, HY_WIDTH), BF16),
        compiler_params=_cparams("parallel"),
        name="hyena_filters",
    )(z, w1p, b1.astype(F32).reshape(1, hid), w2.astype(F32), b2.astype(F32).reshape(1, hid),
      freq.astype(F32).reshape(1, hid), w3.astype(F32), jnp.asarray(_filter_deltas()))


def _fft_dims(seq_len):
    n = 2 * seq_len
    n1 = 1 << ((n.bit_length() - 1) // 2)
    n2 = n // n1
    assert n1 * n2 == n and n1 == n2, "sequence length must give a square transform"
    return n1, n2


def _stack_complex(fr, fi):
    return np.block([[fr, -fi], [fi, fr]])


@functools.lru_cache(maxsize=None)
def _dft_constants(seq_len):
    n1, n2 = _fft_dims(seq_len)
    n = n1 * n2
    k1 = np.arange(n1)[:, None].astype(np.float64)
    t1 = np.arange(n1 // 2)[None, :].astype(np.float64)
    a1 = -2.0 * math.pi * k1 * t1 / n1
    f1r, f1i = np.cos(a1), np.sin(a1)
    w1_complex = _stack_complex(f1r, f1i)
    a1_full = -2.0 * math.pi * k1 * np.arange(n1)[None, :].astype(np.float64) / n1
    w1_real = np.concatenate([np.cos(a1_full), np.sin(a1_full)], axis=0)
    k2 = np.arange(n2)[:, None].astype(np.float64)
    t2 = np.arange(n2)[None, :].astype(np.float64)
    a2 = -2.0 * math.pi * k2 * t2 / n2
    f2r, f2i = np.cos(a2), np.sin(a2)
    g1r, g1i = f1r.T / n, -f1i.T / n
    w3 = _stack_complex(g1r, g1i)
    kb = np.arange(HY_TW_SPLIT)[:, None].astype(np.float64)
    ab = -2.0 * math.pi * kb * t2 / n
    tbr, tbi = np.cos(ab), np.sin(ab)
    w2f = np.stack([_stack_complex(f2r * tbr[b] - f2i * tbi[b], f2r * tbi[b] + f2i * tbr[b])
                    for b in range(HY_TW_SPLIT)])
    w2i = np.stack([_stack_complex(tbr[b][:, None] * f2r - tbi[b][:, None] * f2i,
                                   -(tbr[b][:, None] * f2i + tbi[b][:, None] * f2r))
                    for b in range(HY_TW_SPLIT)])
    ka = (np.arange(n1 // HY_TW_SPLIT) * HY_TW_SPLIT)[:, None].astype(np.float64)
    aa = -2.0 * math.pi * ka * t2 / n
    ta = np.stack([np.cos(aa), np.sin(aa)], axis=0)[..., None]
    ta = np.broadcast_to(ta, (2, n1 // HY_TW_SPLIT, n2, LANE)).astype(np.float32)
    as_bf = lambda a: jnp.asarray(a, F32).astype(BF16)
    return dict(w1_complex=as_bf(w1_complex), w1_real=as_bf(w1_real), w2f=as_bf(w2f), w2i=as_bf(w2i),
                ta=jnp.asarray(ta), w3=as_bf(w3))


def _pack_pair(a, b):
    return pltpu.pack_elementwise([a, b], packed_dtype=BF16)


def _unpack_pair(w):
    return (pltpu.unpack_elementwise(w, index=0, packed_dtype=BF16, unpacked_dtype=F32),
            pltpu.unpack_elementwise(w, index=1, packed_dtype=BF16, unpacked_dtype=F32))


def _conv_pitch(n2):
    return n2 + 8


def _hyena_conv_kernel(z_ref, g_ref, kf_ref, w1_ref, w2f_ref, w2i_ref, w3_ref, ta_ref, bias_ref, o_ref,
                       x_scr, a_scr, *, n1, n2):
    pitch = _conv_pitch(n2)
    half = n1 // 2
    groups = n1 // HY_TW_SPLIT

    def pack_in(t1, carry):
        rows = pl.ds(pl.multiple_of(t1 * n2, n2), n2)
        x_scr[pl.ds(pl.multiple_of(t1 * pitch, 8), n2), :] = _pack_pair(
            z_ref[0, rows, :].astype(F32), z_ref[1, rows, :].astype(F32))
        return carry

    lax.fori_loop(0, half, pack_in, 0)

    def stage1(j, carry):
        cols = []
        for u in range(2):
            xr, xi = _unpack_pair(x_scr[pl.ds(2 * j + u, half, stride=pitch), :])
            cols.append(jnp.concatenate([xr, xi], axis=0).astype(BF16))
        out = jnp.dot(w1_ref[...], jnp.concatenate(cols, axis=1), preferred_element_type=F32)
        for u in range(2):
            o = out[:, u * LANE:(u + 1) * LANE]
            a_scr[pl.ds(2 * j + u, n1, stride=pitch), :] = _pack_pair(o[:n1], o[n1:])
        return carry

    lax.fori_loop(0, n2 // 2, stage1, 0, unroll=8)

    for kb in range(HY_TW_SPLIT):
        def stage2(jp, carry, kb=kb):
            rows, tws, stacks = [], [], []
            for u in range(2):
                ka = 2 * jp + u
                k1 = ka * HY_TW_SPLIT + kb
                rows.append(pl.ds(pl.multiple_of(k1 * pitch, 8), n2))
                ar, ai = _unpack_pair(a_scr[rows[u], :])
                twr, twi = ta_ref[0, ka], ta_ref[1, ka]
                tws.append((twr, twi))
                stacks.append(jnp.concatenate([ar * twr - ai * twi, ar * twi + ai * twr], axis=0).astype(BF16))
            spec = jnp.dot(w2f_ref[kb], jnp.concatenate(stacks, axis=1), preferred_element_type=F32)
            prods = []
            for u in range(2):
                k1 = (2 * jp + u) * HY_TW_SPLIT + kb
                sr, si = spec[:n2, u * LANE:(u + 1) * LANE], spec[n2:, u * LANE:(u + 1) * LANE]
                kr = kf_ref[0, 0, k1].astype(F32)
                ki = kf_ref[0, 1, k1].astype(F32)
                prods.append(jnp.concatenate([sr * kr - si * ki, sr * ki + si * kr], axis=0).astype(BF16))
            back = jnp.dot(w2i_ref[kb], jnp.concatenate(prods, axis=1), preferred_element_type=F32)
            for u in range(2):
                br, bi = back[:n2, u * LANE:(u + 1) * LANE], back[n2:, u * LANE:(u + 1) * LANE]
                twr, twi = tws[u]
                a_scr[rows[u], :] = _pack_pair(br * twr + bi * twi, bi * twr - br * twi)
            return carry

        lax.fori_loop(0, groups // 2, stage2, 0, unroll=4)

    def stage3(j, carry):
        cols = []
        for u in range(2):
            br, bi = _unpack_pair(a_scr[pl.ds(2 * j + u, n1, stride=pitch), :])
            cols.append(jnp.concatenate([br, bi], axis=0).astype(BF16))
        y = jnp.dot(w3_ref[...], jnp.concatenate(cols, axis=1), preferred_element_type=F32)
        for u in range(2):
            sl = pl.ds(2 * j + u, half, stride=pitch)
            za, zb = _unpack_pair(x_scr[sl, :])
            yu = y[:, u * LANE:(u + 1) * LANE]
            x_scr[sl, :] = _pack_pair(yu[:half] + bias_ref[...] * za, yu[half:] + bias_ref[...] * zb)
        return carry

    lax.fori_loop(0, n2 // 2, stage3, 0, unroll=8)

    def gate_out(t1, carry):
        rows = pl.ds(pl.multiple_of(t1 * n2, n2), n2)
        ta, tb = _unpack_pair(x_scr[pl.ds(pl.multiple_of(t1 * pitch, 8), n2), :])
        o_ref[0, rows, :] = (g_ref[0, rows, :].astype(F32) * ta).astype(o_ref.dtype)
        o_ref[1, rows, :] = (g_ref[1, rows, :].astype(F32) * tb).astype(o_ref.dtype)
        return carry

    lax.fori_loop(0, half, gate_out, 0)


def _hyena_conv(z, gate, kf, order, bias, consts):
    s, seq_len, c = z.shape
    n1, n2 = _fft_dims(seq_len)
    pitch = _conv_pitch(n2)
    assert s % 2 == 0 and c % LANE == 0 and (n1 // HY_TW_SPLIT) % 2 == 0
    seq_blk = pl.BlockSpec((2, seq_len, LANE), lambda cc, p: (p, 0, cc))
    return pl.pallas_call(
        functools.partial(_hyena_conv_kernel, n1=n1, n2=n2),
        grid=(c // LANE, s // 2),
        in_specs=[
            seq_blk,
            seq_blk,
            pl.BlockSpec((1, 2, n1, n2, LANE), lambda cc, p: (order, 0, 0, 0, cc), pipeline_mode=pl.Buffered(1)),
            _resident(consts["w1_complex"].shape),
            _resident(consts["w2f"].shape),
            _resident(consts["w2i"].shape),
            _resident(consts["w3"].shape),
            _resident(consts["ta"].shape),
            pl.BlockSpec((1, LANE), lambda cc, p: (0, cc)),
        ],
        out_specs=seq_blk,
        out_shape=jax.ShapeDtypeStruct(z.shape, BF16),
        scratch_shapes=[pltpu.VMEM((n1 // 2 * pitch, LANE), jnp.uint32),
                        pltpu.VMEM((n1 * pitch, LANE), jnp.uint32)],
        compiler_params=_cparams("parallel", "arbitrary"),
        name="hyena_conv",
    )(z, gate, kf, consts["w1_complex"], consts["w2f"], consts["w2i"], consts["w3"], consts["ta"],
      bias.astype(F32).reshape(1, c))


def _spectrum_kernel(k_ref, w1_ref, w2f_ref, ta_ref, o_ref, x_scr, a_scr, *, n1, n2):
    pitch = _conv_pitch(n2)
    groups = n1 // HY_TW_SPLIT

    def pack_in(t1, carry):
        x_scr[pl.ds(pl.multiple_of(t1 * pitch, 8), n2), :] = (
            k_ref[0, pl.ds(pl.multiple_of(t1 * n2, n2), n2), :].astype(F32))
        return carry

    lax.fori_loop(0, n1, pack_in, 0)

    def stage1(j, carry):
        cols = [x_scr[pl.ds(2 * j + u, n1, stride=pitch), :].astype(BF16) for u in range(2)]
        out = jnp.dot(w1_ref[...], jnp.concatenate(cols, axis=1), preferred_element_type=F32)
        for u in range(2):
            o = out[:, u * LANE:(u + 1) * LANE]
            a_scr[pl.ds(2 * j + u, n1, stride=pitch), :] = _pack_pair(o[:n1], o[n1:])
        return carry

    lax.fori_loop(0, n2 // 2, stage1, 0, unroll=4)

    for kb in range(HY_TW_SPLIT):
        def stage2(jp, carry, kb=kb):
            stacks = []
            for u in range(2):
                ka = 2 * jp + u
                k1 = ka * HY_TW_SPLIT + kb
                ar, ai = _unpack_pair(a_scr[pl.ds(pl.multiple_of(k1 * pitch, 8), n2), :])
                twr, twi = ta_ref[0, ka], ta_ref[1, ka]
                stacks.append(jnp.concatenate([ar * twr - ai * twi, ar * twi + ai * twr], axis=0).astype(BF16))
            spec = jnp.dot(w2f_ref[kb], jnp.concatenate(stacks, axis=1), preferred_element_type=F32)
            for u in range(2):
                k1 = (2 * jp + u) * HY_TW_SPLIT + kb
                o_ref[0, 0, k1] = spec[:n2, u * LANE:(u + 1) * LANE].astype(o_ref.dtype)
                o_ref[0, 1, k1] = spec[n2:, u * LANE:(u + 1) * LANE].astype(o_ref.dtype)
            return carry

        lax.fori_loop(0, groups // 2, stage2, 0, unroll=2)


def _filter_spectrum(kern, consts):
    _, n, c = kern.shape
    n1, n2 = _fft_dims(n // 2)
    pitch = _conv_pitch(n2)
    return pl.pallas_call(
        functools.partial(_spectrum_kernel, n1=n1, n2=n2),
        grid=(HY_ORDER, c // LANE),
        in_specs=[
            pl.BlockSpec((1, n, LANE), lambda o, cc: (o, 0, cc), pipeline_mode=pl.Buffered(1)),
            _resident(consts["w1_real"].shape),
            _resident(consts["w2f"].shape),
            _resident(consts["ta"].shape),
        ],
        out_specs=pl.BlockSpec((1, 2, n1, n2, LANE), lambda o, cc: (o, 0, 0, 0, cc)),
        out_shape=jax.ShapeDtypeStruct((HY_ORDER, 2, n1, n2, c), BF16),
        scratch_shapes=[pltpu.VMEM((n1 * pitch, LANE), F32), pltpu.VMEM((n1 * pitch, LANE), jnp.uint32)],
        compiler_params=_cparams("parallel", "arbitrary"),
        name="hyena_filter_spectrum",
    )(kern, consts["w1_real"], consts["w2f"], consts["ta"])


def _hyena_spectra(seq_len, w1, b1, w2, b2, freq, w3):
    consts = _dft_constants(seq_len)
    return _filter_spectrum(_hyena_kernels(seq_len, w1, b1, w2, b2, freq, w3), consts)


def _hyena(h3d, conv_w, conv_b, kf, hy_bias):
    seq_len = h3d.shape[1]
    consts = _dft_constants(seq_len)
    v, x1, x2 = _short_conv(h3d, conv_w, conv_b)
    z = v
    for n, gate in enumerate((x1, x2)):
        z = _hyena_conv(z, gate, kf, n, hy_bias[n], consts)
    return z


def _layernorm(y, g, b):
    mu = jnp.mean(y, axis=-1, keepdims=True)
    d = y - mu
    var = jnp.mean(d * d, axis=-1, keepdims=True)
    return d * lax.rsqrt(var + LN_EPS) * g + b


def _merge_kernel(g_ref, a_ref, hb_ref, c_ref, x_ref, wa_ref, wb_ref, wc_ref, wo_ref, lg_ref, lb_ref, o_ref):
    d = D_MODEL
    subs = [slice(r, r + MERGE_SUB) for r in range(0, x_ref.shape[0], MERGE_SUB)]
    merged = []
    for r in subs:
        acc = g_ref[r, 0:d].astype(F32) * jnp.dot(a_ref[r, :], wa_ref[...], preferred_element_type=F32)
        acc += g_ref[r, d:2 * d].astype(F32) * jnp.dot(hb_ref[r, :], wb_ref[...], preferred_element_type=F32)
        acc += g_ref[r, 2 * d:3 * d].astype(F32) * jnp.dot(c_ref[r, :], wc_ref[...], preferred_element_type=F32)
        merged.append(acc.astype(BF16))
    for r, mg in zip(subs, merged):
        mix = jnp.dot(mg, wo_ref[...], preferred_element_type=F32)
        o_ref[r, :] = _layernorm(DEEPNORM_ALPHA * x_ref[r, :] + mix, lg_ref[...], lb_ref[...])


def _merge(h2d, a, hb, c, x2d, wa, wb, wc, wo, ln_g, ln_b):
    m = x2d.shape[0]
    tm = MERGE_TM
    row = lambda width: pl.BlockSpec((tm, width), lambda i: (i, 0))
    return pl.pallas_call(
        _merge_kernel,
        grid=(m // tm,),
        in_specs=[
            row(N_BRANCH * D_MODEL), row(NA_WIDTH), row(HY_WIDTH), row(SWA_WIDTH), row(D_MODEL),
            _resident((NA_WIDTH, D_MODEL)), _resident((HY_WIDTH, D_MODEL)), _resident((SWA_WIDTH, D_MODEL)),
            _resident((D_MODEL, D_MODEL)), _resident((1, D_MODEL)), _resident((1, D_MODEL)),
        ],
        out_specs=row(D_MODEL),
        out_shape=jax.ShapeDtypeStruct((m, D_MODEL), F32),
        compiler_params=_cparams("parallel"),
        name="merge_ln",
    )(h2d, a, hb, c, x2d, wa, wb, wc, wo, ln_g, ln_b)


def _mlp_kernel(x_ref, wu_ref, bu_ref, wd_ref, bd_ref, lg_ref, lb_ref, o_ref):
    for r0 in range(0, x_ref.shape[0], MLP_SUB):
        rows = slice(r0, r0 + MLP_SUB)
        x = x_ref[rows, :]
        xb = x.astype(BF16)
        acc = DEEPNORM_ALPHA * x + bd_ref[...]
        for c in range(D_FF // MLP_FF_CHUNK):
            sl = slice(c * MLP_FF_CHUNK, (c + 1) * MLP_FF_CHUNK)
            up = jnp.dot(xb, wu_ref[:, sl], preferred_element_type=F32) + bu_ref[:, sl]
            up = jnp.square(jnp.maximum(up, 0.0))
            acc += jnp.dot(up.astype(BF16), wd_ref[sl, :], preferred_element_type=F32)
        o_ref[rows, :] = _layernorm(acc, lg_ref[...], lb_ref[...])


def _mlp(x2d, wu, bu, wd, bd, ln_g, ln_b):
    m = x2d.shape[0]
    tm = MLP_TM
    row = pl.BlockSpec((tm, D_MODEL), lambda i: (i, 0))
    return pl.pallas_call(
        _mlp_kernel,
        grid=(m // tm,),
        in_specs=[
            row, _resident((D_MODEL, D_FF)), _resident((1, D_FF)), _resident((D_FF, D_MODEL)),
            _resident((1, D_MODEL)), _resident((1, D_MODEL)), _resident((1, D_MODEL)),
        ],
        out_specs=row,
        out_shape=jax.ShapeDtypeStruct((m, D_MODEL), F32),
        compiler_params=_cparams("parallel"),
        name="mlp_ln",
    )(x2d, wu, bu, wd, bd, ln_g, ln_b)


def _permute_in_columns(a):
    q0 = OFF_SWA
    order = [h for j in range(SWA_GROUP) for h in (j, j + SWA_GROUP)]
    swa_q = [a[..., q0 + h * HEAD_DIM:q0 + (h + 1) * HEAD_DIM] for h in order]
    swa_k = a[..., q0 + SWA_WIDTH:q0 + SWA_WIDTH + SWA_KV_WIDTH]
    return jnp.concatenate([a[..., OFF_GATE:], a[..., OFF_HY:OFF_SWA], a[..., :2 * NA_WIDTH]] + swa_q + [swa_k],
                           axis=-1)


def _scale_query_columns(a):
    c = HEAD_DIM ** -0.5 * LOG2E
    return jnp.concatenate([a[..., :NA_WIDTH] * c, a[..., NA_WIDTH:OFF_SWA],
                            a[..., OFF_SWA:OFF_SWA + SWA_WIDTH] * c, a[..., OFF_SWA + SWA_WIDTH:]], axis=-1)


def _values_columns(a):
    v_swa = OFF_SWA + SWA_WIDTH + SWA_KV_WIDTH
    return jnp.concatenate([a[..., 2 * NA_WIDTH:3 * NA_WIDTH], a[..., v_swa:v_swa + SWA_KV_WIDTH]], axis=-1)


def _prepare_layer(l, p):
    row = lambda a: a.astype(F32).reshape(1, -1)
    return dict(
        w_in=_permute_in_columns(_scale_query_columns(p["w_in"][l])).astype(BF16),
        b_in=row(_permute_in_columns(_scale_query_columns(p["b_in"][l]))),
        conv_w=p["hy_conv_w"][l], conv_b=p["hy_conv_b"][l], hy_bias=p["hy_bias"][l],
        filt=(p["hy_filt_w1"][l], p["hy_filt_b1"][l], p["hy_filt_w2"][l], p["hy_filt_b2"][l],
              p["hy_filt_freq"][l], p["hy_filt_w3"][l]),
        w_vt=_values_columns(p["w_in"][l]).T.astype(BF16),
        b_vt=_values_columns(p["b_in"][l]).astype(F32).reshape(VT_WIDTH, 1),
        na_bias=_na_bias_table(p["na_rpb"][l]),
        sink=p["swa_sink"][l],
        wa=p["w_branch_a"][l].astype(BF16), wb=p["w_branch_b"][l].astype(BF16),
        wc=p["w_branch_c"][l].astype(BF16), wo=p["w_out"][l].astype(BF16),
        ln1_g=row(p["ln1_g"][l]), ln1_b=row(p["ln1_b"][l]),
        wu=p["w_up"][l].astype(BF16), bu=row(p["b_up"][l]),
        wd=p["w_down"][l].astype(BF16), bd=row(p["b_down"][l]),
        ln2_g=row(p["ln2_g"][l]), ln2_b=row(p["ln2_b"][l]),
    )


def _encoder_block(x, lp, swa_table):
    b_sz, seq_len, _ = x.shape
    m = b_sz * seq_len
    x2d = x.reshape(m, D_MODEL)
    h2d, vt = _inproj(x2d, lp["w_in"], lp["b_in"], lp["w_vt"], lp["b_vt"])
    h3d = h2d.reshape(b_sz, seq_len, D_TOK)
    a = _na_attention(h3d, vt, lp["na_bias"])
    kf = _hyena_spectra(seq_len, *lp["filt"])
    hb = _hyena(h3d, lp["conv_w"], lp["conv_b"], kf, lp["hy_bias"])
    c = _swa_attention(h3d, vt, lp["sink"], swa_table)
    x1 = _merge(h2d, a.reshape(m, NA_WIDTH), hb.reshape(m, HY_WIDTH), c.reshape(m, SWA_WIDTH), x2d,
                lp["wa"], lp["wb"], lp["wc"], lp["wo"], lp["ln1_g"], lp["ln1_b"])
    x2 = _mlp(x1, lp["wu"], lp["bu"], lp["wd"], lp["bd"], lp["ln2_g"], lp["ln2_b"])
    return x2.reshape(b_sz, seq_len, D_MODEL)


def kernel(x_prompt, x_sample, w_in, b_in, hy_conv_w, hy_conv_b, hy_filt_w1, hy_filt_b1, hy_filt_w2,
           hy_filt_b2, hy_filt_freq, hy_filt_w3, hy_bias, na_rpb, swa_sink, w_branch_a, w_branch_b,
           w_branch_c, w_out, ln1_g, ln1_b, w_up, b_up, w_down, b_down, ln2_g, ln2_b):
    params = dict(w_in=w_in, b_in=b_in, hy_conv_w=hy_conv_w, hy_conv_b=hy_conv_b, hy_filt_w1=hy_filt_w1,
                  hy_filt_b1=hy_filt_b1, hy_filt_w2=hy_filt_w2, hy_filt_b2=hy_filt_b2,
                  hy_filt_freq=hy_filt_freq, hy_filt_w3=hy_filt_w3, hy_bias=hy_bias, na_rpb=na_rpb,
                  swa_sink=swa_sink, w_branch_a=w_branch_a, w_branch_b=w_branch_b, w_branch_c=w_branch_c,
                  w_out=w_out, ln1_g=ln1_g, ln1_b=ln1_b, w_up=w_up, b_up=b_up, w_down=w_down,
                  b_down=b_down, ln2_g=ln2_g, ln2_b=ln2_b)
    swa_table = _swa_table()
    y_prompt = x_prompt
    y_sample = x_sample
    for l in range(DEPTH):
        lp = _prepare_layer(l, params)
        y_prompt = _encoder_block(y_prompt, lp, swa_table)
        y_sample = _encoder_block(y_sample, lp, swa_table)
    return (y_prompt, y_sample)
```

```python
import functools
import math

import numpy as np
import jax
import jax.numpy as jnp
from jax import lax
from jax.experimental import pallas as pl
from jax.experimental.pallas import tpu as pltpu

F32 = jnp.float32
BF16 = jnp.bfloat16

D_MODEL = 1024
DEPTH = 2
HEAD_DIM = 64
GRID_W = 64
NA_HEADS = 8
NA_WIN_ROWS = 8
NA_WIN_COLS = 16
NA_WIDTH = NA_HEADS * HEAD_DIM
HY_WIDTH = D_MODEL // 2
HY_ORDER = 2
HY_SHORT_CONV = 3
HY_POS_BANDS = 16
HY_EMB_DIM = 1 + 2 * HY_POS_BANDS
HY_EMB_PAD = 128
HY_FILTER_HIDDEN = 64
HY_FAST_DECAY_PCT = 0.3
HY_SLOW_DECAY_PCT = 1.5
HY_DECAY_TARGET = 1e-2
SWA_HEADS = 8
SWA_KV_HEADS = 2
SWA_GROUP = SWA_HEADS // SWA_KV_HEADS
SWA_WIDTH = SWA_HEADS * HEAD_DIM
SWA_KV_WIDTH = SWA_KV_HEADS * HEAD_DIM
SWA_WINDOW = 128
SWA_BLOCK = 128
N_BRANCH = 3
D_FF = 4 * D_MODEL
OFF_HY = 3 * NA_WIDTH
OFF_SWA = OFF_HY + 3 * HY_WIDTH
OFF_GATE = OFF_SWA + SWA_WIDTH + 2 * SWA_KV_WIDTH
D_IN = OFF_GATE + N_BRANCH * D_MODEL
DEEPNORM_ALPHA = (2 * DEPTH) ** 0.25
LN_EPS = 1e-5
NEG_BIG = -1e30
LOG2E = 1.4426950408889634

P_GATE = 0
P_HY = N_BRANCH * D_MODEL
P_NA = P_HY + 3 * HY_WIDTH
P_SWA = P_NA + 2 * NA_WIDTH
D_TOK = P_SWA + SWA_WIDTH + SWA_KV_WIDTH
VT_WIDTH = NA_WIDTH + SWA_KV_WIDTH

LANE = 128
SUBLANE_PACK = 16
VMEM_LIMIT_BYTES = 56 * 1024 * 1024

NA_PAIR_ROWS = NA_WIN_ROWS + 2
ATTN_BLOCKS_PER_STEP = 4
INPROJ_TM = 512
INPROJ_TN = 768
MERGE_TM = 1024
MERGE_SUB = 256
MLP_TM = 1024
MLP_SUB = 512
MLP_FF_CHUNK = 1024
CONV_TL = 512
CONV_SUB = 128
FILT_TL = 512
HY_TW_SPLIT = 8


def _cparams(*sem):
    return pltpu.CompilerParams(dimension_semantics=sem, vmem_limit_bytes=VMEM_LIMIT_BYTES)


def _resident(shape):
    nd = len(shape)
    return pl.BlockSpec(shape, lambda *_: (0,) * nd, pipeline_mode=pl.Buffered(1))


def _inproj_kernel(x_ref, w_ref, b_ref, wvt_ref, bvt_ref, o_ref, vt_ref):
    xb = x_ref[...].astype(BF16)
    vt = lax.dot_general(wvt_ref[...], xb, (((1,), (1,)), ((), ())), preferred_element_type=F32)
    vt_ref[...] = (vt + bvt_ref[...]).astype(vt_ref.dtype)
    for c0 in range(0, D_TOK, INPROJ_TN):
        cols = slice(c0, min(c0 + INPROJ_TN, D_TOK))
        acc = jnp.dot(xb, w_ref[:, cols], preferred_element_type=F32) + b_ref[:, cols]
        if cols.stop <= P_HY:
            acc = 1.0 / (1.0 + jnp.exp(-acc))
        o_ref[:, cols] = acc.astype(o_ref.dtype)


def _inproj(x2d, w_bf, b_row, wvt_bf, bvt_col):
    m = x2d.shape[0]
    tm = min(INPROJ_TM, m)
    assert m % tm == 0 and P_HY % INPROJ_TN == 0
    return pl.pallas_call(
        _inproj_kernel,
        grid=(m // tm,),
        in_specs=[
            pl.BlockSpec((tm, D_MODEL), lambda i: (i, 0)),
            _resident((D_MODEL, D_TOK)),
            _resident((1, D_TOK)),
            _resident((VT_WIDTH, D_MODEL)),
            _resident((VT_WIDTH, 1)),
        ],
        out_specs=[pl.BlockSpec((tm, D_TOK), lambda i: (i, 0)),
                   pl.BlockSpec((VT_WIDTH, tm), lambda i: (0, i))],
        out_shape=[jax.ShapeDtypeStruct((m, D_TOK), BF16), jax.ShapeDtypeStruct((VT_WIDTH, m), BF16)],
        compiler_params=_cparams("parallel"),
        name="inproj",
    )(x2d, w_bf, b_row, wvt_bf, bvt_col)


def _na_bias_table(rpb):
    kr = NA_WIN_ROWS
    pad = GRID_W - NA_WIN_COLS
    p = jnp.pad(rpb.astype(F32) * LOG2E, ((0, 0), (0, 0), (pad, pad)))
    cols = jnp.stack([p[:, :, GRID_W - 1 - w:2 * GRID_W - 1 - w] for w in range(GRID_W)], axis=2)
    t = jnp.stack([cols[:, kr - 1 - d:2 * kr - 1 - d] for d in range(kr)], axis=0)
    w = np.arange(GRID_W)[:, None]
    kc = np.arange(GRID_W)[None, :]
    col_start = np.clip(w - NA_WIN_COLS // 2, 0, GRID_W - NA_WIN_COLS)
    valid = (kc >= col_start) & (kc < col_start + NA_WIN_COLS)
    t = jnp.where(valid[None, None, None], t, NEG_BIG)
    t = jnp.transpose(t, (0, 1, 2, 4, 3))
    return t.reshape(kr, NA_HEADS, kr * GRID_W, GRID_W)


def _na_pair_geometry(p, rows):
    base = min(max(2 * p - NA_WIN_ROWS // 2, 0), rows - NA_PAIR_ROWS)
    geo = []
    for j in range(2):
        r = 2 * p + j
        rs = min(max(r - NA_WIN_ROWS // 2, 0), rows - NA_WIN_ROWS)
        assert 0 <= rs - base <= NA_PAIR_ROWS - NA_WIN_ROWS
        geo.append((rs - base, r - rs))
    return base, geo


def _na_pair_table(table, rows):
    n_pairs = rows // 2
    variants = []
    for p in (0, 1, 2, n_pairs - 2, n_pairs - 1):
        _, geo = _na_pair_geometry(p, rows)
        cols = []
        for off, delta in geo:
            after = NA_PAIR_ROWS - NA_WIN_ROWS - off
            cols.append(jnp.pad(table[delta], ((0, 0), (off * GRID_W, after * GRID_W), (0, 0)),
                                constant_values=NEG_BIG))
        variants.append(jnp.concatenate(cols, axis=-1))
    return jnp.stack(variants, axis=0)


def _scores_t_pair(k, q):
    lane = lax.broadcasted_iota(jnp.int32, (1, LANE), 1)
    zero = jnp.zeros((), q.dtype)
    q_both = jnp.concatenate([jnp.where(lane < HEAD_DIM, q, zero), jnp.where(lane >= HEAD_DIM, q, zero)], axis=0)
    s = lax.dot_general(k, q_both, (((1,), (1,)), ((), ())), preferred_element_type=F32)
    return s[:, :q.shape[0]], s[:, q.shape[0]:]


def _weights_t(s, sink=None):
    m = jnp.max(s, axis=0, keepdims=True)
    if sink is None:
        return jnp.exp2(s - m).astype(BF16), None
    m = jnp.maximum(m, sink)
    return jnp.exp2(s - m).astype(BF16), jnp.exp2(sink - m)


def _weighted_values_t(vt, p, sink_term=None):
    ones = jnp.ones((SUBLANE_PACK, vt.shape[1]), vt.dtype)
    acc = jnp.dot(jnp.concatenate([vt, ones], axis=0), p, preferred_element_type=F32)
    total = acc[vt.shape[0]:vt.shape[0] + 1]
    if sink_term is not None:
        total = total + sink_term
    return acc[:vt.shape[0]] / total


def _na_kernel(q_ref, k_ref, vt_ref, *rest, rows, pairs_per_step):
    bias_refs, o_ref = rest[:-1], rest[-1]
    span = NA_PAIR_ROWS * GRID_W
    work = []
    for u in range(pairs_per_step):
        pair = pl.program_id(1) * pairs_per_step + u
        base = jnp.clip(2 * pair - NA_WIN_ROWS // 2, 0, rows - NA_PAIR_ROWS)
        start = pl.multiple_of(base * GRID_W, 2 * GRID_W)
        q = q_ref[0, u * LANE:(u + 1) * LANE, :]
        k = k_ref[0, pl.ds(start, span), :]
        vt = vt_ref[:, pl.ds(start, span)]
        for j in range(NA_HEADS // 2):
            grp = slice(j * LANE, (j + 1) * LANE)
            for h, s in zip((2 * j, 2 * j + 1), _scores_t_pair(k[:, grp], q[:, grp])):
                work.append((vt[h * HEAD_DIM:(h + 1) * HEAD_DIM, :],) + _weights_t(s + bias_refs[u][0, h]))
    outs = [_weighted_values_t(*w) for w in work]
    for u in range(pairs_per_step):
        o_t = jnp.concatenate(outs[u * NA_HEADS:(u + 1) * NA_HEADS], axis=0)
        o_ref[0, u * LANE:(u + 1) * LANE, :] = o_t.T.astype(o_ref.dtype)


def _na_attention(h3d, vt, bias_table):
    b_sz, seq_len, _ = h3d.shape
    rows = seq_len // GRID_W
    n_pairs = rows // 2
    assert rows % 2 == 0 and n_pairs >= 5
    cb = P_NA // NA_WIDTH
    span = NA_PAIR_ROWS * GRID_W
    pair_table = _na_pair_table(bias_table, rows)

    def variant(p):
        return jnp.where(p < 2, p, jnp.where(p > n_pairs - 3, p - (n_pairs - 5), 2))

    pps = ATTN_BLOCKS_PER_STEP
    assert n_pairs % pps == 0 and 2 * GRID_W == LANE
    bias_specs = [pl.BlockSpec((1, NA_HEADS, span, LANE), lambda b, p, u=u: (variant(p * pps + u), 0, 0, 0))
                  for u in range(pps)]
    return pl.pallas_call(
        functools.partial(_na_kernel, rows=rows, pairs_per_step=pps),
        grid=(b_sz, n_pairs // pps),
        in_specs=[
            pl.BlockSpec((1, pps * LANE, NA_WIDTH), lambda b, p: (b, p, cb)),
            pl.BlockSpec((1, seq_len, NA_WIDTH), lambda b, p: (b, 0, cb + 1), pipeline_mode=pl.Buffered(1)),
            pl.BlockSpec((NA_WIDTH, seq_len), lambda b, p: (0, b), pipeline_mode=pl.Buffered(1)),
        ] + bias_specs,
        out_specs=pl.BlockSpec((1, pps * LANE, NA_WIDTH), lambda b, p: (b, p, 0)),
        out_shape=jax.ShapeDtypeStruct((b_sz, seq_len, NA_WIDTH), BF16),
        compiler_params=_cparams("parallel", "arbitrary"),
        name="na_attention",
    )(h3d, h3d, vt, *([pair_table] * pps))


def _swa_table():
    span = SWA_BLOCK + 2 * SWA_WINDOW
    slopes = 2.0 ** (-8.0 * (np.arange(SWA_HEADS, dtype=np.float64) + 1.0) / SWA_HEADS)
    kk = np.arange(span)[:, None]
    t = np.arange(SWA_BLOCK)[None, :]
    variants = []
    for off in (0, SWA_BLOCK, 2 * SWA_BLOCK):
        rel = np.abs(kk - off - t)
        a = -slopes[:, None, None] * rel[None].astype(np.float64) * LOG2E
        variants.append(np.where((rel <= SWA_WINDOW)[None], a, NEG_BIG))
    return jnp.asarray(np.stack(variants), F32)


def _swa_kernel(sink_ref, q_ref, k_ref, vt_ref, *rest, seq_len, blocks_per_step):
    tab_refs, o_ref = rest[:-1], rest[-1]
    span = SWA_BLOCK + 2 * SWA_WINDOW
    work = []
    for u in range(blocks_per_step):
        blk = pl.program_id(1) * blocks_per_step + u
        start = pl.multiple_of(jnp.clip((blk - 1) * SWA_BLOCK, 0, seq_len - span), SWA_BLOCK)
        q = q_ref[0, u * SWA_BLOCK:(u + 1) * SWA_BLOCK, :]
        k = k_ref[0, pl.ds(start, span), :]
        vt = vt_ref[:, pl.ds(start, span)]
        per_head = {}
        for j in range(SWA_GROUP):
            grp = slice(j * LANE, (j + 1) * LANE)
            for g, s in enumerate(_scores_t_pair(k, q[:, grp])):
                h = j + g * SWA_GROUP
                per_head[h] = ((vt[g * HEAD_DIM:(g + 1) * HEAD_DIM, :],)
                               + _weights_t(s + tab_refs[u][0, h], sink_ref[h] * LOG2E))
        work.extend(per_head[h] for h in range(SWA_HEADS))
    outs = [_weighted_values_t(*w) for w in work]
    for u in range(blocks_per_step):
        o_t = jnp.concatenate(outs[u * SWA_HEADS:(u + 1) * SWA_HEADS], axis=0)
        o_ref[0, u * SWA_BLOCK:(u + 1) * SWA_BLOCK, :] = o_t.T.astype(o_ref.dtype)


def _swa_attention(h3d, vt, sink, table):
    b_sz, seq_len, _ = h3d.shape
    nb = seq_len // SWA_BLOCK
    span = SWA_BLOCK + 2 * SWA_WINDOW
    assert SWA_KV_HEADS == 2 and SWA_BLOCK == LANE and nb >= 3
    qb = P_SWA // SWA_WIDTH
    kb = (P_SWA + SWA_WIDTH) // SWA_KV_WIDTH

    def variant(i):
        return jnp.where(i == 0, 0, jnp.where(i == nb - 1, 2, 1))

    bps = ATTN_BLOCKS_PER_STEP
    assert nb % bps == 0
    tab_specs = [pl.BlockSpec((1, SWA_HEADS, span, SWA_BLOCK), lambda b, i, u=u: (variant(i * bps + u), 0, 0, 0))
                 for u in range(bps)]
    return pl.pallas_call(
        functools.partial(_swa_kernel, seq_len=seq_len, blocks_per_step=bps),
        grid=(b_sz, nb // bps),
        in_specs=[
            pl.BlockSpec(memory_space=pltpu.SMEM),
            pl.BlockSpec((1, bps * SWA_BLOCK, SWA_WIDTH), lambda b, i: (b, i, qb)),
            pl.BlockSpec((1, seq_len, SWA_KV_WIDTH), lambda b, i: (b, 0, kb)),
            pl.BlockSpec((SWA_KV_WIDTH, seq_len), lambda b, i: (NA_WIDTH // SWA_KV_WIDTH, b)),
        ] + tab_specs,
        out_specs=pl.BlockSpec((1, bps * SWA_BLOCK, SWA_WIDTH), lambda b, i: (b, i, 0)),
        out_shape=jax.ShapeDtypeStruct((b_sz, seq_len, SWA_WIDTH), BF16),
        compiler_params=_cparams("parallel", "arbitrary"),
        name="swa_attention",
    )(sink.astype(F32), h3d, h3d, vt, *([table] * bps))


def _shift_matrix(sub, halo):
    s = np.zeros((2 * sub, sub + 2 * halo), np.float32)
    t = np.arange(sub)
    s[t, halo + t - 1] = 1.0
    s[sub + t, halo + t + 1] = 1.0
    return jnp.asarray(s).astype(BF16)


def _conv_kernel(x_ref, prev_ref, next_ref, s_ref, w_ref, b_ref, v_ref, x1_ref, x2_ref, *, n_tiles):
    i = pl.program_id(1)
    tl = x_ref.shape[1]
    halo = prev_ref.shape[1]
    sub = s_ref.shape[0] // 2
    prev = (prev_ref[0].astype(F32) * jnp.where(i > 0, 1.0, 0.0)).astype(BF16)
    nxt = (next_ref[0].astype(F32) * jnp.where(i < n_tiles - 1, 1.0, 0.0)).astype(BF16)
    xe = jnp.concatenate([prev, x_ref[0], nxt], axis=0)
    outs = (v_ref, x1_ref, x2_ref)
    for r0 in range(0, tl, sub):
        y = jnp.dot(s_ref[...], xe[r0:r0 + sub + 2 * halo], preferred_element_type=F32)
        xc = xe[halo + r0:halo + r0 + sub].astype(F32)
        u = y[:sub] * w_ref[0:1, :] + xc * w_ref[1:2, :] + y[sub:] * w_ref[2:3, :] + b_ref[...]
        for c in range(3):
            outs[c][0, r0:r0 + sub, :] = u[:, c * HY_WIDTH:(c + 1) * HY_WIDTH].astype(BF16)


def _short_conv(h3d, conv_w, conv_b):
    b_sz, seq_len, _ = h3d.shape
    tl = CONV_TL
    halo = 16
    n_tiles = seq_len // tl
    width = 3 * HY_WIDTH
    cb = P_HY // width
    assert P_HY % width == 0 and seq_len % tl == 0
    per = tl // halo
    out = jax.ShapeDtypeStruct((b_sz, seq_len, HY_WIDTH), BF16)
    ospec = pl.BlockSpec((1, tl, HY_WIDTH), lambda b, i: (b, i, 0))
    return pl.pallas_call(
        functools.partial(_conv_kernel, n_tiles=n_tiles),
        grid=(b_sz, n_tiles),
        in_specs=[
            pl.BlockSpec((1, tl, width), lambda b, i: (b, i, cb)),
            pl.BlockSpec((1, halo, width), lambda b, i: (b, jnp.maximum(i * per - 1, 0), cb)),
            pl.BlockSpec((1, halo, width), lambda b, i: (b, jnp.minimum((i + 1) * per, seq_len // halo - 1), cb)),
            _resident((2 * CONV_SUB, CONV_SUB + 2 * halo)),
            _resident((HY_SHORT_CONV, width)),
            _resident((1, width)),
        ],
        out_specs=[ospec, ospec, ospec],
        out_shape=[out, out, out],
        compiler_params=_cparams("parallel", "arbitrary"),
        name="hyena_short_conv",
    )(h3d, h3d, h3d, _shift_matrix(CONV_SUB, halo), conv_w.astype(F32), conv_b.astype(F32).reshape(1, width))


def _filter_embedding(seq_len):
    t = np.linspace(0.0, 1.0, seq_len, dtype=np.float32).astype(np.float64)[:, None]
    w = (2.0 * math.pi * np.arange(seq_len, dtype=np.float32) / seq_len).astype(np.float32)
    bands = np.linspace(1e-4, HY_POS_BANDS - 1, HY_POS_BANDS, dtype=np.float32)
    ang = (w[:, None] * bands[None, :]).astype(np.float32).astype(np.float64)
    z = np.concatenate([t, np.cos(ang), -np.sin(ang)], axis=-1)
    zp = np.zeros((2 * seq_len, HY_EMB_PAD), np.float32)
    zp[:seq_len, :HY_EMB_DIM] = z
    zp[seq_len + 1:, :HY_EMB_DIM] = z[:0:-1]
    return zp


def _filter_deltas():
    min_decay = math.log(HY_DECAY_TARGET) / HY_SLOW_DECAY_PCT
    max_decay = math.log(HY_DECAY_TARGET) / HY_FAST_DECAY_PCT
    return np.abs(np.linspace(min_decay, max_decay, HY_WIDTH, dtype=np.float32))[None, :]


def _filter_kernel(z_ref, w1_ref, b1_ref, w2_ref, b2_ref, fr_ref, w3_ref, dl_ref, o_ref, *, seq_len):
    hi = lax.Precision.HIGHEST
    tl = z_ref.shape[0]
    z = z_ref[...]
    fr = fr_ref[...]
    h = jnp.sin(fr * (jnp.dot(z, w1_ref[...], precision=hi, preferred_element_type=F32) + b1_ref[...]))
    h = jnp.sin(fr * (jnp.dot(h, w2_ref[...], precision=hi, preferred_element_type=F32) + b2_ref[...]))
    t = z[:, 0:1]
    window = jnp.exp(-t * dl_ref[...])
    first_row = pl.program_id(0) * tl
    row = first_row + lax.broadcasted_iota(jnp.int32, (tl, 1), 0)

    def emit(direction):
        for o in range(HY_ORDER):
            c0 = (2 * o + direction) * HY_WIDTH
            f = jnp.dot(h, w3_ref[:, c0:c0 + HY_WIDTH], precision=hi, preferred_element_type=F32) * window
            if direction == 1:
                f = jnp.where(row > seq_len, f, 0.0)
            o_ref[o] = f.astype(o_ref.dtype)

    pl.when(first_row < seq_len)(lambda: emit(0))
    pl.when(first_row >= seq_len)(lambda: emit(1))


def _hyena_kernels(seq_len, w1, b1, w2, b2, freq, w3):
    tl = min(FILT_TL, seq_len)
    assert seq_len % tl == 0
    z = jnp.asarray(_filter_embedding(seq_len))
    w1p = jnp.zeros((HY_EMB_PAD, HY_FILTER_HIDDEN), F32).at[:HY_EMB_DIM].set(w1.astype(F32))
    hid = HY_FILTER_HIDDEN
    return pl.pallas_call(
        functools.partial(_filter_kernel, seq_len=seq_len),
        grid=(2 * seq_len // tl,),
        in_specs=[
            pl.BlockSpec((tl, HY_EMB_PAD), lambda i: (i, 0)),
            _resident((HY_EMB_PAD, hid)), _resident((1, hid)),
            _resident((hid, hid)), _resident((1, hid)), _resident((1, hid)),
            _resident((hid, 2 * HY_ORDER * HY_WIDTH)), _resident((1, HY_WIDTH)),
        ],
        out_specs=pl.BlockSpec((HY_ORDER, tl, HY_WIDTH), lambda i: (0, i, 0)),
        out_shape=jax.ShapeDtypeStruct((HY_ORDER, 2 * seq_len, HY_WIDTH), BF16),
        compiler_params=_cparams("parallel"),
        name="hyena_filters",
    )(z, w1p, b1.astype(F32).reshape(1, hid), w2.astype(F32), b2.astype(F32).reshape(1, hid),
      freq.astype(F32).reshape(1, hid), w3.astype(F32), jnp.asarray(_filter_deltas()))


def _fft_dims(seq_len):
    n = 2 * seq_len
    n1 = 1 << ((n.bit_length() - 1) // 2)
    n2 = n // n1
    assert n1 * n2 == n and n1 == n2, "sequence length must give a square transform"
    return n1, n2


def _stack_complex(fr, fi):
    return np.block([[fr, -fi], [fi, fr]])


@functools.lru_cache(maxsize=None)
def _dft_constants(seq_len):
    n1, n2 = _fft_dims(seq_len)
    n = n1 * n2
    k1 = np.arange(n1)[:, None].astype(np.float64)
    t1 = np.arange(n1 // 2)[None, :].astype(np.float64)
    a1 = -2.0 * math.pi * k1 * t1 / n1
    f1r, f1i = np.cos(a1), np.sin(a1)
    w1_complex = _stack_complex(f1r, f1i)
    a1_full = -2.0 * math.pi * k1 * np.arange(n1)[None, :].astype(np.float64) / n1
    w1_real = np.concatenate([np.cos(a1_full), np.sin(a1_full)], axis=0)
    k2 = np.arange(n2)[:, None].astype(np.float64)
    t2 = np.arange(n2)[None, :].astype(np.float64)
    a2 = -2.0 * math.pi * k2 * t2 / n2
    f2r, f2i = np.cos(a2), np.sin(a2)
    g1r, g1i = f1r.T / n, -f1i.T / n
    w3 = _stack_complex(g1r, g1i)
    kb = np.arange(HY_TW_SPLIT)[:, None].astype(np.float64)
    ab = -2.0 * math.pi * kb * t2 / n
    tbr, tbi = np.cos(ab), np.sin(ab)
    w2f = np.stack([_stack_complex(f2r * tbr[b] - f2i * tbi[b], f2r * tbi[b] + f2i * tbr[b])
                    for b in range(HY_TW_SPLIT)])
    w2i = np.stack([_stack_complex(tbr[b][:, None] * f2r - tbi[b][:, None] * f2i,
                                   -(tbr[b][:, None] * f2i + tbi[b][:, None] * f2r))
                    for b in range(HY_TW_SPLIT)])
    ka = (np.arange(n1 // HY_TW_SPLIT) * HY_TW_SPLIT)[:, None].astype(np.float64)
    aa = -2.0 * math.pi * ka * t2 / n
    ta = np.stack([np.cos(aa), np.sin(aa)], axis=0)[..., None]
    ta = np.broadcast_to(ta, (2, n1 // HY_TW_SPLIT, n2, LANE)).astype(np.float32)
    as_bf = lambda a: jnp.asarray(a, F32).astype(BF16)
    return dict(w1_complex=as_bf(w1_complex), w1_real=as_bf(w1_real), w2f=as_bf(w2f), w2i=as_bf(w2i),
                ta=jnp.asarray(ta), w3=as_bf(w3))


def _pack_pair(a, b):
    return pltpu.pack_elementwise([a, b], packed_dtype=BF16)


def _unpack_pair(w):
    return (pltpu.unpack_elementwise(w, index=0, packed_dtype=BF16, unpacked_dtype=F32),
            pltpu.unpack_elementwise(w, index=1, packed_dtype=BF16, unpacked_dtype=F32))


def _conv_pitch(n2):
    return n2 + 8


def _hyena_conv_kernel(z_ref, g_ref, kf_ref, w1_ref, w2f_ref, w2i_ref, w3_ref, ta_ref, bias_ref, o_ref,
                       x_scr, a_scr, *, n1, n2):
    pitch = _conv_pitch(n2)
    half = n1 // 2
    groups = n1 // HY_TW_SPLIT

    def pack_in(t1, carry):
        rows = pl.ds(pl.multiple_of(t1 * n2, n2), n2)
        x_scr[pl.ds(pl.multiple_of(t1 * pitch, 8), n2), :] = _pack_pair(
            z_ref[0, rows, :].astype(F32), z_ref[1, rows, :].astype(F32))
        return carry

    lax.fori_loop(0, half, pack_in, 0)

    def stage1(j, carry):
        cols = []
        for u in range(2):
            xr, xi = _unpack_pair(x_scr[pl.ds(2 * j + u, half, stride=pitch), :])
            cols.append(jnp.concatenate([xr, xi], axis=0).astype(BF16))
        out = jnp.dot(w1_ref[...], jnp.concatenate(cols, axis=1), preferred_element_type=F32)
        for u in range(2):
            o = out[:, u * LANE:(u + 1) * LANE]
            a_scr[pl.ds(2 * j + u, n1, stride=pitch), :] = _pack_pair(o[:n1], o[n1:])
        return carry

    lax.fori_loop(0, n2 // 2, stage1, 0, unroll=8)

    for kb in range(HY_TW_SPLIT):
        def stage2(jp, carry, kb=kb):
            rows, tws, stacks = [], [], []
            for u in range(2):
                ka = 2 * jp + u
                k1 = ka * HY_TW_SPLIT + kb
                rows.append(pl.ds(pl.multiple_of(k1 * pitch, 8), n2))
                ar, ai = _unpack_pair(a_scr[rows[u], :])
                twr, twi = ta_ref[0, ka], ta_ref[1, ka]
                tws.append((twr, twi))
                stacks.append(jnp.concatenate([ar * twr - ai * twi, ar * twi + ai * twr], axis=0).astype(BF16))
            spec = jnp.dot(w2f_ref[kb], jnp.concatenate(stacks, axis=1), preferred_element_type=F32)
            prods = []
            for u in range(2):
                k1 = (2 * jp + u) * HY_TW_SPLIT + kb
                sr, si = spec[:n2, u * LANE:(u + 1) * LANE], spec[n2:, u * LANE:(u + 1) * LANE]
                kr = kf_ref[0, 0, k1].astype(F32)
                ki = kf_ref[0, 1, k1].astype(F32)
                prods.append(jnp.concatenate([sr * kr - si * ki, sr * ki + si * kr], axis=0).astype(BF16))
            back = jnp.dot(w2i_ref[kb], jnp.concatenate(prods, axis=1), preferred_element_type=F32)
            for u in range(2):
                br, bi = back[:n2, u * LANE:(u + 1) * LANE], back[n2:, u * LANE:(u + 1) * LANE]
                twr, twi = tws[u]
                a_scr[rows[u], :] = _pack_pair(br * twr + bi * twi, bi * twr - br * twi)
            return carry

        lax.fori_loop(0, groups // 2, stage2, 0, unroll=8)

    def stage3(j, carry):
        cols = []
        for u in range(2):
            br, bi = _unpack_pair(a_scr[pl.ds(2 * j + u, n1, stride=pitch), :])
            cols.append(jnp.concatenate([br, bi], axis=0).astype(BF16))
        y = jnp.dot(w3_ref[...], jnp.concatenate(cols, axis=1), preferred_element_type=F32)
        for u in range(2):
            sl = pl.ds(2 * j + u, half, stride=pitch)
            za, zb = _unpack_pair(x_scr[sl, :])
            yu = y[:, u * LANE:(u + 1) * LANE]
            x_scr[sl, :] = _pack_pair(yu[:half] + bias_ref[...] * za, yu[half:] + bias_ref[...] * zb)
        return carry

    lax.fori_loop(0, n2 // 2, stage3, 0, unroll=8)

    def gate_out(t1, carry):
        rows = pl.ds(pl.multiple_of(t1 * n2, n2), n2)
        ta, tb = _unpack_pair(x_scr[pl.ds(pl.multiple_of(t1 * pitch, 8), n2), :])
        o_ref[0, rows, :] = (g_ref[0, rows, :].astype(F32) * ta).astype(o_ref.dtype)
        o_ref[1, rows, :] = (g_ref[1, rows, :].astype(F32) * tb).astype(o_ref.dtype)
        return carry

    lax.fori_loop(0, half, gate_out, 0)


def _hyena_conv(z, gate, kf, order, bias, consts):
    s, seq_len, c = z.shape
    n1, n2 = _fft_dims(seq_len)
    pitch = _conv_pitch(n2)
    assert s % 2 == 0 and c % LANE == 0 and (n1 // HY_TW_SPLIT) % 2 == 0
    seq_blk = pl.BlockSpec((2, seq_len, LANE), lambda cc, p: (p, 0, cc))
    return pl.pallas_call(
        functools.partial(_hyena_conv_kernel, n1=n1, n2=n2),
        grid=(c // LANE, s // 2),
        in_specs=[
            seq_blk,
            seq_blk,
            pl.BlockSpec((1, 2, n1, n2, LANE), lambda cc, p: (order, 0, 0, 0, cc), pipeline_mode=pl.Buffered(1)),
            _resident(consts["w1_complex"].shape),
            _resident(consts["w2f"].shape),
            _resident(consts["w2i"].shape),
            _resident(consts["w3"].shape),
            _resident(consts["ta"].shape),
            pl.BlockSpec((1, LANE), lambda cc, p: (0, cc)),
        ],
        out_specs=seq_blk,
        out_shape=jax.ShapeDtypeStruct(z.shape, BF16),
        scratch_shapes=[pltpu.VMEM((n1 // 2 * pitch, LANE), jnp.uint32),
                        pltpu.VMEM((n1 * pitch, LANE), jnp.uint32)],
        compiler_params=_cparams("parallel", "arbitrary"),
        name="hyena_conv",
    )(z, gate, kf, consts["w1_complex"], consts["w2f"], consts["w2i"], consts["w3"], consts["ta"],
      bias.astype(F32).reshape(1, c))


def _spectrum_kernel(k_ref, w1_ref, w2f_ref, ta_ref, o_ref, x_scr, a_scr, *, n1, n2):
    pitch = _conv_pitch(n2)
    groups = n1 // HY_TW_SPLIT

    def pack_in(t1, carry):
        x_scr[pl.ds(pl.multiple_of(t1 * pitch, 8), n2), :] = (
            k_ref[0, pl.ds(pl.multiple_of(t1 * n2, n2), n2), :].astype(F32))
        return carry

    lax.fori_loop(0, n1, pack_in, 0)

    def stage1(j, carry):
        cols = [x_scr[pl.ds(2 * j + u, n1, stride=pitch), :].astype(BF16) for u in range(2)]
        out = jnp.dot(w1_ref[...], jnp.concatenate(cols, axis=1), preferred_element_type=F32)
        for u in range(2):
            o = out[:, u * LANE:(u + 1) * LANE]
            a_scr[pl.ds(2 * j + u, n1, stride=pitch), :] = _pack_pair(o[:n1], o[n1:])
        return carry

    lax.fori_loop(0, n2 // 2, stage1, 0, unroll=4)

    for kb in range(HY_TW_SPLIT):
        def stage2(jp, carry, kb=kb):
            stacks = []
            for u in range(2):
                ka = 2 * jp + u
                k1 = ka * HY_TW_SPLIT + kb
                ar, ai = _unpack_pair(a_scr[pl.ds(pl.multiple_of(k1 * pitch, 8), n2), :])
                twr, twi = ta_ref[0, ka], ta_ref[1, ka]
                stacks.append(jnp.concatenate([ar * twr - ai * twi, ar * twi + ai * twr], axis=0).astype(BF16))
            spec = jnp.dot(w2f_ref[kb], jnp.concatenate(stacks, axis=1), preferred_element_type=F32)
            for u in range(2):
                k1 = (2 * jp + u) * HY_TW_SPLIT + kb
                o_ref[0, 0, k1] = spec[:n2, u * LANE:(u + 1) * LANE].astype(o_ref.dtype)
                o_ref[0, 1, k1] = spec[n2:, u * LANE:(u + 1) * LANE].astype(o_ref.dtype)
            return carry

        lax.fori_loop(0, groups // 2, stage2, 0, unroll=2)


def _filter_spectrum(kern, consts):
    _, n, c = kern.shape
    n1, n2 = _fft_dims(n // 2)
    pitch = _conv_pitch(n2)
    return pl.pallas_call(
        functools.partial(_spectrum_kernel, n1=n1, n2=n2),
        grid=(HY_ORDER, c // LANE),
        in_specs=[
            pl.BlockSpec((1, n, LANE), lambda o, cc: (o, 0, cc), pipeline_mode=pl.Buffered(1)),
            _resident(consts["w1_real"].shape),
            _resident(consts["w2f"].shape),
            _resident(consts["ta"].shape),
        ],
        out_specs=pl.BlockSpec((1, 2, n1, n2, LANE), lambda o, cc: (o, 0, 0, 0, cc)),
        out_shape=jax.ShapeDtypeStruct((HY_ORDER, 2, n1, n2, c), BF16),
        scratch_shapes=[pltpu.VMEM((n1 * pitch, LANE), F32), pltpu.VMEM((n1 * pitch, LANE), jnp.uint32)],
        compiler_params=_cparams("parallel", "arbitrary"),
        name="hyena_filter_spectrum",
    )(kern, consts["w1_real"], consts["w2f"], consts["ta"])


def _hyena_spectra(seq_len, w1, b1, w2, b2, freq, w3):
    consts = _dft_constants(seq_len)
    return _filter_spectrum(_hyena_kernels(seq_len, w1, b1, w2, b2, freq, w3), consts)


def _hyena(h3d, conv_w, conv_b, kf, hy_bias):
    seq_len = h3d.shape[1]
    consts = _dft_constants(seq_len)
    v, x1, x2 = _short_conv(h3d, conv_w, conv_b)
    z = v
    for n, gate in enumerate((x1, x2)):
        z = _hyena_conv(z, gate, kf, n, hy_bias[n], consts)
    return z


def _layernorm(y, g, b):
    mu = jnp.mean(y, axis=-1, keepdims=True)
    d = y - mu
    var = jnp.mean(d * d, axis=-1, keepdims=True)
    return d * lax.rsqrt(var + LN_EPS) * g + b


def _merge_kernel(g_ref, a_ref, hb_ref, c_ref, x_ref, wa_ref, wb_ref, wc_ref, wo_ref, lg_ref, lb_ref, o_ref):
    d = D_MODEL
    subs = [slice(r, r + MERGE_SUB) for r in range(0, x_ref.shape[0], MERGE_SUB)]
    merged = []
    for r in subs:
        acc = g_ref[r, 0:d].astype(F32) * jnp.dot(a_ref[r, :], wa_ref[...], preferred_element_type=F32)
        acc += g_ref[r, d:2 * d].astype(F32) * jnp.dot(hb_ref[r, :], wb_ref[...], preferred_element_type=F32)
        acc += g_ref[r, 2 * d:3 * d].astype(F32) * jnp.dot(c_ref[r, :], wc_ref[...], preferred_element_type=F32)
        merged.append(acc.astype(BF16))
    for r, mg in zip(subs, merged):
        mix = jnp.dot(mg, wo_ref[...], preferred_element_type=F32)
        o_ref[r, :] = _layernorm(DEEPNORM_ALPHA * x_ref[r, :] + mix, lg_ref[...], lb_ref[...])


def _merge(h2d, a, hb, c, x2d, wa, wb, wc, wo, ln_g, ln_b):
    m = x2d.shape[0]
    tm = MERGE_TM
    row = lambda width: pl.BlockSpec((tm, width), lambda i: (i, 0))
    return pl.pallas_call(
        _merge_kernel,
        grid=(m // tm,),
        in_specs=[
            row(N_BRANCH * D_MODEL), row(NA_WIDTH), row(HY_WIDTH), row(SWA_WIDTH), row(D_MODEL),
            _resident((NA_WIDTH, D_MODEL)), _resident((HY_WIDTH, D_MODEL)), _resident((SWA_WIDTH, D_MODEL)),
            _resident((D_MODEL, D_MODEL)), _resident((1, D_MODEL)), _resident((1, D_MODEL)),
        ],
        out_specs=row(D_MODEL),
        out_shape=jax.ShapeDtypeStruct((m, D_MODEL), F32),
        compiler_params=_cparams("parallel"),
        name="merge_ln",
    )(h2d, a, hb, c, x2d, wa, wb, wc, wo, ln_g, ln_b)


def _mlp_kernel(x_ref, wu_ref, bu_ref, wd_ref, bd_ref, lg_ref, lb_ref, o_ref):
    for r0 in range(0, x_ref.shape[0], MLP_SUB):
        rows = slice(r0, r0 + MLP_SUB)
        x = x_ref[rows, :]
        xb = x.astype(BF16)
        acc = DEEPNORM_ALPHA * x + bd_ref[...]
        for c in range(D_FF // MLP_FF_CHUNK):
            sl = slice(c * MLP_FF_CHUNK, (c + 1) * MLP_FF_CHUNK)
            up = jnp.dot(xb, wu_ref[:, sl], preferred_element_type=F32) + bu_ref[:, sl]
            up = jnp.square(jnp.maximum(up, 0.0))
            acc += jnp.dot(up.astype(BF16), wd_ref[sl, :], preferred_element_type=F32)
        o_ref[rows, :] = _layernorm(acc, lg_ref[...], lb_ref[...])


def _mlp(x2d, wu, bu, wd, bd, ln_g, ln_b):
    m = x2d.shape[0]
    tm = MLP_TM
    row = pl.BlockSpec((tm, D_MODEL), lambda i: (i, 0))
    return pl.pallas_call(
        _mlp_kernel,
        grid=(m // tm,),
        in_specs=[
            row, _resident((D_MODEL, D_FF)), _resident((1, D_FF)), _resident((D_FF, D_MODEL)),
            _resident((1, D_MODEL)), _resident((1, D_MODEL)), _resident((1, D_MODEL)),
        ],
        out_specs=row,
        out_shape=jax.ShapeDtypeStruct((m, D_MODEL), F32),
        compiler_params=_cparams("parallel"),
        name="mlp_ln",
    )(x2d, wu, bu, wd, bd, ln_g, ln_b)


def _permute_in_columns(a):
    q0 = OFF_SWA
    order = [h for j in range(SWA_GROUP) for h in (j, j + SWA_GROUP)]
    swa_q = [a[..., q0 + h * HEAD_DIM:q0 + (h + 1) * HEAD_DIM] for h in order]
    swa_k = a[..., q0 + SWA_WIDTH:q0 + SWA_WIDTH + SWA_KV_WIDTH]
    return jnp.concatenate([a[..., OFF_GATE:], a[..., OFF_HY:OFF_SWA], a[..., :2 * NA_WIDTH]] + swa_q + [swa_k],
                           axis=-1)


def _scale_query_columns(a):
    c = HEAD_DIM ** -0.5 * LOG2E
    return jnp.concatenate([a[..., :NA_WIDTH] * c, a[..., NA_WIDTH:OFF_SWA],
                            a[..., OFF_SWA:OFF_SWA + SWA_WIDTH] * c, a[..., OFF_SWA + SWA_WIDTH:]], axis=-1)


def _values_columns(a):
    v_swa = OFF_SWA + SWA_WIDTH + SWA_KV_WIDTH
    return jnp.concatenate([a[..., 2 * NA_WIDTH:3 * NA_WIDTH], a[..., v_swa:v_swa + SWA_KV_WIDTH]], axis=-1)


def _prepare_layer(l, p):
    row = lambda a: a.astype(F32).reshape(1, -1)
    return dict(
        w_in=_permute_in_columns(_scale_query_columns(p["w_in"][l])).astype(BF16),
        b_in=row(_permute_in_columns(_scale_query_columns(p["b_in"][l]))),
        conv_w=p["hy_conv_w"][l], conv_b=p["hy_conv_b"][l], hy_bias=p["hy_bias"][l],
        filt=(p["hy_filt_w1"][l], p["hy_filt_b1"][l], p["hy_filt_w2"][l], p["hy_filt_b2"][l],
              p["hy_filt_freq"][l], p["hy_filt_w3"][l]),
        w_vt=_values_columns(p["w_in"][l]).T.astype(BF16),
        b_vt=_values_columns(p["b_in"][l]).astype(F32).reshape(VT_WIDTH, 1),
        na_bias=_na_bias_table(p["na_rpb"][l]),
        sink=p["swa_sink"][l],
        wa=p["w_branch_a"][l].astype(BF16), wb=p["w_branch_b"][l].astype(BF16),
        wc=p["w_branch_c"][l].astype(BF16), wo=p["w_out"][l].astype(BF16),
        ln1_g=row(p["ln1_g"][l]), ln1_b=row(p["ln1_b"][l]),
        wu=p["w_up"][l].astype(BF16), bu=row(p["b_up"][l]),
        wd=p["w_down"][l].astype(BF16), bd=row(p["b_down"][l]),
        ln2_g=row(p["ln2_g"][l]), ln2_b=row(p["ln2_b"][l]),
    )


def _encoder_block(x, lp, swa_table):
    b_sz, seq_len, _ = x.shape
    m = b_sz * seq_len
    x2d = x.reshape(m, D_MODEL)
    h2d, vt = _inproj(x2d, lp["w_in"], lp["b_in"], lp["w_vt"], lp["b_vt"])
    h3d = h2d.reshape(b_sz, seq_len, D_TOK)
    a = _na_attention(h3d, vt, lp["na_bias"])
    kf = _hyena_spectra(seq_len, *lp["filt"])
    hb = _hyena(h3d, lp["conv_w"], lp["conv_b"], kf, lp["hy_bias"])
    c = _swa_attention(h3d, vt, lp["sink"], swa_table)
    x1 = _merge(h2d, a.reshape(m, NA_WIDTH), hb.reshape(m, HY_WIDTH), c.reshape(m, SWA_WIDTH), x2d,
                lp["wa"], lp["wb"], lp["wc"], lp["wo"], lp["ln1_g"], lp["ln1_b"])
    x2 = _mlp(x1, lp["wu"], lp["bu"], lp["wd"], lp["bd"], lp["ln2_g"], lp["ln2_b"])
    return x2.reshape(b_sz, seq_len, D_MODEL)


def kernel(x_prompt, x_sample, w_in, b_in, hy_conv_w, hy_conv_b, hy_filt_w1, hy_filt_b1, hy_filt_w2,
           hy_filt_b2, hy_filt_freq, hy_filt_w3, hy_bias, na_rpb, swa_sink, w_branch_a, w_branch_b,
           w_branch_c, w_out, ln1_g, ln1_b, w_up, b_up, w_down, b_down, ln2_g, ln2_b):
    params = dict(w_in=w_in, b_in=b_in, hy_conv_w=hy_conv_w, hy_conv_b=hy_conv_b, hy_filt_w1=hy_filt_w1,
                  hy_filt_b1=hy_filt_b1, hy_filt_w2=hy_filt_w2, hy_filt_b2=hy_filt_b2,
                  hy_filt_freq=hy_filt_freq, hy_filt_w3=hy_filt_w3, hy_bias=hy_bias, na_rpb=na_rpb,
                  swa_sink=swa_sink, w_branch_a=w_branch_a, w_branch_b=w_branch_b, w_branch_c=w_branch_c,
                  w_out=w_out, ln1_g=ln1_g, ln1_b=ln1_b, w_up=w_up, b_up=b_up, w_down=w_down,
                  b_down=b_down, ln2_g=ln2_g, ln2_b=ln2_b)
    swa_table = _swa_table()
    y_prompt = x_prompt
    y_sample = x_sample
    for l in range(DEPTH):
        lp = _prepare_layer(l, params)
        y_prompt = _encoder_block(y_prompt, lp, swa_table)
        y_sample = _encoder_block(y_sample, lp, swa_table)
    return (y_prompt, y_sample)
```

```python
import functools
import math

import numpy as np
import jax
import jax.numpy as jnp
from jax import lax
from jax.experimental import pallas as pl
from jax.experimental.pallas import tpu as pltpu

F32 = jnp.float32
BF16 = jnp.bfloat16

D_MODEL = 1024
DEPTH = 2
HEAD_DIM = 64
GRID_W = 64
NA_HEADS = 8
NA_WIN_ROWS = 8
NA_WIN_COLS = 16
NA_WIDTH = NA_HEADS * HEAD_DIM
HY_WIDTH = D_MODEL // 2
HY_ORDER = 2
HY_SHORT_CONV = 3
HY_POS_BANDS = 16
HY_EMB_DIM = 1 + 2 * HY_POS_BANDS
HY_EMB_PAD = 128
HY_FILTER_HIDDEN = 64
HY_FAST_DECAY_PCT = 0.3
HY_SLOW_DECAY_PCT = 1.5
HY_DECAY_TARGET = 1e-2
SWA_HEADS = 8
SWA_KV_HEADS = 2
SWA_GROUP = SWA_HEADS // SWA_KV_HEADS
SWA_WIDTH = SWA_HEADS * HEAD_DIM
SWA_KV_WIDTH = SWA_KV_HEADS * HEAD_DIM
SWA_WINDOW = 128
SWA_BLOCK = 128
N_BRANCH = 3
D_FF = 4 * D_MODEL
OFF_HY = 3 * NA_WIDTH
OFF_SWA = OFF_HY + 3 * HY_WIDTH
OFF_GATE = OFF_SWA + SWA_WIDTH + 2 * SWA_KV_WIDTH
D_IN = OFF_GATE + N_BRANCH * D_MODEL
DEEPNORM_ALPHA = (2 * DEPTH) ** 0.25
LN_EPS = 1e-5
NEG_BIG = -1e30
LOG2E = 1.4426950408889634

P_GATE = 0
P_HY = N_BRANCH * D_MODEL
P_NA = P_HY + 3 * HY_WIDTH
P_SWA = P_NA + 2 * NA_WIDTH
D_TOK = P_SWA + SWA_WIDTH + SWA_KV_WIDTH
VT_WIDTH = NA_WIDTH + SWA_KV_WIDTH

LANE = 128
SUBLANE_PACK = 16
VMEM_LIMIT_BYTES = 56 * 1024 * 1024

NA_PAIR_ROWS = NA_WIN_ROWS + 2
ATTN_BLOCKS_PER_STEP = 4
INPROJ_TM = 512
INPROJ_TN = 768
MERGE_TM = 1024
MERGE_SUB = 256
MERGE_MLP_TM = 512
MLP_TM = 1024
MLP_SUB = 512
MLP_FF_CHUNK = 1024
CONV_TL = 512
CONV_SUB = 128
FILT_TL = 512
HY_TW_SPLIT = 8


def _cparams(*sem):
    return pltpu.CompilerParams(dimension_semantics=sem, vmem_limit_bytes=VMEM_LIMIT_BYTES)


def _resident(shape):
    nd = len(shape)
    return pl.BlockSpec(shape, lambda *_: (0,) * nd, pipeline_mode=pl.Buffered(1))


def _inproj_kernel(x_ref, w_ref, b_ref, wvt_ref, bvt_ref, o_ref, vt_ref):
    xb = x_ref[...].astype(BF16)
    vt = lax.dot_general(wvt_ref[...], xb, (((1,), (1,)), ((), ())), preferred_element_type=F32)
    vt_ref[...] = (vt + bvt_ref[...]).astype(vt_ref.dtype)
    for c0 in range(0, D_TOK, INPROJ_TN):
        cols = slice(c0, min(c0 + INPROJ_TN, D_TOK))
        acc = jnp.dot(xb, w_ref[:, cols], preferred_element_type=F32) + b_ref[:, cols]
        if cols.stop <= P_HY:
            acc = 1.0 / (1.0 + jnp.exp(-acc))
        o_ref[:, cols] = acc.astype(o_ref.dtype)


def _inproj(x2d, w_bf, b_row, wvt_bf, bvt_col):
    m = x2d.shape[0]
    tm = min(INPROJ_TM, m)
    assert m % tm == 0 and P_HY % INPROJ_TN == 0
    return pl.pallas_call(
        _inproj_kernel,
        grid=(m // tm,),
        in_specs=[
            pl.BlockSpec((tm, D_MODEL), lambda i: (i, 0)),
            _resident((D_MODEL, D_TOK)),
            _resident((1, D_TOK)),
            _resident((VT_WIDTH, D_MODEL)),
            _resident((VT_WIDTH, 1)),
        ],
        out_specs=[pl.BlockSpec((tm, D_TOK), lambda i: (i, 0)),
                   pl.BlockSpec((VT_WIDTH, tm), lambda i: (0, i))],
        out_shape=[jax.ShapeDtypeStruct((m, D_TOK), BF16), jax.ShapeDtypeStruct((VT_WIDTH, m), BF16)],
        compiler_params=_cparams("parallel"),
        name="inproj",
    )(x2d, w_bf, b_row, wvt_bf, bvt_col)


def _na_bias_table(rpb):
    kr = NA_WIN_ROWS
    pad = GRID_W - NA_WIN_COLS
    p = jnp.pad(rpb.astype(F32) * LOG2E, ((0, 0), (0, 0), (pad, pad)))
    cols = jnp.stack([p[:, :, GRID_W - 1 - w:2 * GRID_W - 1 - w] for w in range(GRID_W)], axis=2)
    t = jnp.stack([cols[:, kr - 1 - d:2 * kr - 1 - d] for d in range(kr)], axis=0)
    w = np.arange(GRID_W)[:, None]
    kc = np.arange(GRID_W)[None, :]
    col_start = np.clip(w - NA_WIN_COLS // 2, 0, GRID_W - NA_WIN_COLS)
    valid = (kc >= col_start) & (kc < col_start + NA_WIN_COLS)
    t = jnp.where(valid[None, None, None], t, NEG_BIG)
    t = jnp.transpose(t, (0, 1, 2, 4, 3))
    return t.reshape(kr, NA_HEADS, kr * GRID_W, GRID_W)


def _na_pair_geometry(p, rows):
    base = min(max(2 * p - NA_WIN_ROWS // 2, 0), rows - NA_PAIR_ROWS)
    geo = []
    for j in range(2):
        r = 2 * p + j
        rs = min(max(r - NA_WIN_ROWS // 2, 0), rows - NA_WIN_ROWS)
        assert 0 <= rs - base <= NA_PAIR_ROWS - NA_WIN_ROWS
        geo.append((rs - base, r - rs))
    return base, geo


def _na_pair_table(table, rows):
    n_pairs = rows // 2
    variants = []
    for p in (0, 1, 2, n_pairs - 2, n_pairs - 1):
        _, geo = _na_pair_geometry(p, rows)
        cols = []
        for off, delta in geo:
            after = NA_PAIR_ROWS - NA_WIN_ROWS - off
            cols.append(jnp.pad(table[delta], ((0, 0), (off * GRID_W, after * GRID_W), (0, 0)),
                                constant_values=NEG_BIG))
        variants.append(jnp.concatenate(cols, axis=-1))
    return jnp.stack(variants, axis=0)


def _scores_t_pair(k, q):
    lane = lax.broadcasted_iota(jnp.int32, (1, LANE), 1)
    zero = jnp.zeros((), q.dtype)
    q_both = jnp.concatenate([jnp.where(lane < HEAD_DIM, q, zero), jnp.where(lane >= HEAD_DIM, q, zero)], axis=0)
    s = lax.dot_general(k, q_both, (((1,), (1,)), ((), ())), preferred_element_type=F32)
    return s[:, :q.shape[0]], s[:, q.shape[0]:]


def _weights_t(s, sink=None):
    m = jnp.max(s, axis=0, keepdims=True)
    if sink is None:
        return jnp.exp2(s - m).astype(BF16), None
    m = jnp.maximum(m, sink)
    return jnp.exp2(s - m).astype(BF16), jnp.exp2(sink - m)


def _weighted_values_t(vt, p, sink_term=None):
    ones = jnp.ones((SUBLANE_PACK, vt.shape[1]), vt.dtype)
    acc = jnp.dot(jnp.concatenate([vt, ones], axis=0), p, preferred_element_type=F32)
    total = acc[vt.shape[0]:vt.shape[0] + 1]
    if sink_term is not None:
        total = total + sink_term
    return acc[:vt.shape[0]] / total


def _na_kernel(q_ref, k_ref, vt_ref, *rest, rows, pairs_per_step):
    bias_refs, o_ref = rest[:-1], rest[-1]
    span = NA_PAIR_ROWS * GRID_W
    work = []
    for u in range(pairs_per_step):
        pair = pl.program_id(1) * pairs_per_step + u
        base = jnp.clip(2 * pair - NA_WIN_ROWS // 2, 0, rows - NA_PAIR_ROWS)
        start = pl.multiple_of(base * GRID_W, 2 * GRID_W)
        q = q_ref[0, u * LANE:(u + 1) * LANE, :]
        k = k_ref[0, pl.ds(start, span), :]
        vt = vt_ref[:, pl.ds(start, span)]
        for j in range(NA_HEADS // 2):
            grp = slice(j * LANE, (j + 1) * LANE)
            for h, s in zip((2 * j, 2 * j + 1), _scores_t_pair(k[:, grp], q[:, grp])):
                work.append((vt[h * HEAD_DIM:(h + 1) * HEAD_DIM, :],) + _weights_t(s + bias_refs[u][0, h]))
    outs = [_weighted_values_t(*w) for w in work]
    for u in range(pairs_per_step):
        o_t = jnp.concatenate(outs[u * NA_HEADS:(u + 1) * NA_HEADS], axis=0)
        o_ref[0, u * LANE:(u + 1) * LANE, :] = o_t.T.astype(o_ref.dtype)


def _na_attention(h3d, vt, bias_table):
    b_sz, seq_len, _ = h3d.shape
    rows = seq_len // GRID_W
    n_pairs = rows // 2
    assert rows % 2 == 0 and n_pairs >= 5
    cb = P_NA // NA_WIDTH
    span = NA_PAIR_ROWS * GRID_W
    pair_table = _na_pair_table(bias_table, rows)

    def variant(p):
        return jnp.where(p < 2, p, jnp.where(p > n_pairs - 3, p - (n_pairs - 5), 2))

    pps = ATTN_BLOCKS_PER_STEP
    assert n_pairs % pps == 0 and 2 * GRID_W == LANE
    bias_specs = [pl.BlockSpec((1, NA_HEADS, span, LANE), lambda b, p, u=u: (variant(p * pps + u), 0, 0, 0))
                  for u in range(pps)]
    return pl.pallas_call(
        functools.partial(_na_kernel, rows=rows, pairs_per_step=pps),
        grid=(b_sz, n_pairs // pps),
        in_specs=[
            pl.BlockSpec((1, pps * LANE, NA_WIDTH), lambda b, p: (b, p, cb)),
            pl.BlockSpec((1, seq_len, NA_WIDTH), lambda b, p: (b, 0, cb + 1), pipeline_mode=pl.Buffered(1)),
            pl.BlockSpec((NA_WIDTH, seq_len), lambda b, p: (0, b), pipeline_mode=pl.Buffered(1)),
        ] + bias_specs,
        out_specs=pl.BlockSpec((1, pps * LANE, NA_WIDTH), lambda b, p: (b, p, 0)),
        out_shape=jax.ShapeDtypeStruct((b_sz, seq_len, NA_WIDTH), BF16),
        compiler_params=_cparams("parallel", "arbitrary"),
        name="na_attention",
    )(h3d, h3d, vt, *([pair_table] * pps))


def _swa_table():
    span = SWA_BLOCK + 2 * SWA_WINDOW
    slopes = 2.0 ** (-8.0 * (np.arange(SWA_HEADS, dtype=np.float64) + 1.0) / SWA_HEADS)
    kk = np.arange(span)[:, None]
    t = np.arange(SWA_BLOCK)[None, :]
    variants = []
    for off in (0, SWA_BLOCK, 2 * SWA_BLOCK):
        rel = np.abs(kk - off - t)
        a = -slopes[:, None, None] * rel[None].astype(np.float64) * LOG2E
        variants.append(np.where((rel <= SWA_WINDOW)[None], a, NEG_BIG))
    return jnp.asarray(np.stack(variants), F32)


def _swa_kernel(sink_ref, q_ref, k_ref, vt_ref, *rest, seq_len, blocks_per_step):
    tab_refs, o_ref = rest[:-1], rest[-1]
    span = SWA_BLOCK + 2 * SWA_WINDOW
    work = []
    for u in range(blocks_per_step):
        blk = pl.program_id(1) * blocks_per_step + u
        start = pl.multiple_of(jnp.clip((blk - 1) * SWA_BLOCK, 0, seq_len - span), SWA_BLOCK)
        q = q_ref[0, u * SWA_BLOCK:(u + 1) * SWA_BLOCK, :]
        k = k_ref[0, pl.ds(start, span), :]
        vt = vt_ref[:, pl.ds(start, span)]
        per_head = {}
        for j in range(SWA_GROUP):
            grp = slice(j * LANE, (j + 1) * LANE)
            for g, s in enumerate(_scores_t_pair(k, q[:, grp])):
                h = j + g * SWA_GROUP
                per_head[h] = ((vt[g * HEAD_DIM:(g + 1) * HEAD_DIM, :],)
                               + _weights_t(s + tab_refs[u][0, h], sink_ref[h] * LOG2E))
        work.extend(per_head[h] for h in range(SWA_HEADS))
    outs = [_weighted_values_t(*w) for w in work]
    for u in range(blocks_per_step):
        o_t = jnp.concatenate(outs[u * SWA_HEADS:(u + 1) * SWA_HEADS], axis=0)
        o_ref[0, u * SWA_BLOCK:(u + 1) * SWA_BLOCK, :] = o_t.T.astype(o_ref.dtype)


def _swa_attention(h3d, vt, sink, table):
    b_sz, seq_len, _ = h3d.shape
    nb = seq_len // SWA_BLOCK
    span = SWA_BLOCK + 2 * SWA_WINDOW
    assert SWA_KV_HEADS == 2 and SWA_BLOCK == LANE and nb >= 3
    qb = P_SWA // SWA_WIDTH
    kb = (P_SWA + SWA_WIDTH) // SWA_KV_WIDTH

    def variant(i):
        return jnp.where(i == 0, 0, jnp.where(i == nb - 1, 2, 1))

    bps = ATTN_BLOCKS_PER_STEP
    assert nb % bps == 0
    tab_specs = [pl.BlockSpec((1, SWA_HEADS, span, SWA_BLOCK), lambda b, i, u=u: (variant(i * bps + u), 0, 0, 0))
                 for u in range(bps)]
    return pl.pallas_call(
        functools.partial(_swa_kernel, seq_len=seq_len, blocks_per_step=bps),
        grid=(b_sz, nb // bps),
        in_specs=[
            pl.BlockSpec(memory_space=pltpu.SMEM),
            pl.BlockSpec((1, bps * SWA_BLOCK, SWA_WIDTH), lambda b, i: (b, i, qb)),
            pl.BlockSpec((1, seq_len, SWA_KV_WIDTH), lambda b, i: (b, 0, kb)),
            pl.BlockSpec((SWA_KV_WIDTH, seq_len), lambda b, i: (NA_WIDTH // SWA_KV_WIDTH, b)),
        ] + tab_specs,
        out_specs=pl.BlockSpec((1, bps * SWA_BLOCK, SWA_WIDTH), lambda b, i: (b, i, 0)),
        out_shape=jax.ShapeDtypeStruct((b_sz, seq_len, SWA_WIDTH), BF16),
        compiler_params=_cparams("parallel", "arbitrary"),
        name="swa_attention",
    )(sink.astype(F32), h3d, h3d, vt, *([table] * bps))


def _shift_matrix(sub, halo):
    s = np.zeros((2 * sub, sub + 2 * halo), np.float32)
    t = np.arange(sub)
    s[t, halo + t - 1] = 1.0
    s[sub + t, halo + t + 1] = 1.0
    return jnp.asarray(s).astype(BF16)


def _conv_kernel(x_ref, prev_ref, next_ref, s_ref, w_ref, b_ref, v_ref, x1_ref, x2_ref, *, n_tiles):
    i = pl.program_id(1)
    tl = x_ref.shape[1]
    halo = prev_ref.shape[1]
    sub = s_ref.shape[0] // 2
    prev = (prev_ref[0].astype(F32) * jnp.where(i > 0, 1.0, 0.0)).astype(BF16)
    nxt = (next_ref[0].astype(F32) * jnp.where(i < n_tiles - 1, 1.0, 0.0)).astype(BF16)
    xe = jnp.concatenate([prev, x_ref[0], nxt], axis=0)
    outs = (v_ref, x1_ref, x2_ref)
    for r0 in range(0, tl, sub):
        y = jnp.dot(s_ref[...], xe[r0:r0 + sub + 2 * halo], preferred_element_type=F32)
        xc = xe[halo + r0:halo + r0 + sub].astype(F32)
        u = y[:sub] * w_ref[0:1, :] + xc * w_ref[1:2, :] + y[sub:] * w_ref[2:3, :] + b_ref[...]
        for c in range(3):
            outs[c][0, r0:r0 + sub, :] = u[:, c * HY_WIDTH:(c + 1) * HY_WIDTH].astype(BF16)


def _short_conv(h3d, conv_w, conv_b):
    b_sz, seq_len, _ = h3d.shape
    tl = CONV_TL
    halo = 16
    n_tiles = seq_len // tl
    width = 3 * HY_WIDTH
    cb = P_HY // width
    assert P_HY % width == 0 and seq_len % tl == 0
    per = tl // halo
    out = jax.ShapeDtypeStruct((b_sz, seq_len, HY_WIDTH), BF16)
    ospec = pl.BlockSpec((1, tl, HY_WIDTH), lambda b, i: (b, i, 0))
    return pl.pallas_call(
        functools.partial(_conv_kernel, n_tiles=n_tiles),
        grid=(b_sz, n_tiles),
        in_specs=[
            pl.BlockSpec((1, tl, width), lambda b, i: (b, i, cb)),
            pl.BlockSpec((1, halo, width), lambda b, i: (b, jnp.maximum(i * per - 1, 0), cb)),
            pl.BlockSpec((1, halo, width), lambda b, i: (b, jnp.minimum((i + 1) * per, seq_len // halo - 1), cb)),
            _resident((2 * CONV_SUB, CONV_SUB + 2 * halo)),
            _resident((HY_SHORT_CONV, width)),
            _resident((1, width)),
        ],
        out_specs=[ospec, ospec, ospec],
        out_shape=[out, out, out],
        compiler_params=_cparams("parallel", "arbitrary"),
        name="hyena_short_conv",
    )(h3d, h3d, h3d, _shift_matrix(CONV_SUB, halo), conv_w.astype(F32), conv_b.astype(F32).reshape(1, width))


def _filter_embedding(seq_len):
    t = np.linspace(0.0, 1.0, seq_len, dtype=np.float32).astype(np.float64)[:, None]
    w = (2.0 * math.pi * np.arange(seq_len, dtype=np.float32) / seq_len).astype(np.float32)
    bands = np.linspace(1e-4, HY_POS_BANDS - 1, HY_POS_BANDS, dtype=np.float32)
    ang = (w[:, None] * bands[None, :]).astype(np.float32).astype(np.float64)
    z = np.concatenate([t, np.cos(ang), -np.sin(ang)], axis=-1)
    zp = np.zeros((2 * seq_len, HY_EMB_PAD), np.float32)
    zp[:seq_len, :HY_EMB_DIM] = z
    zp[seq_len + 1:, :HY_EMB_DIM] = z[:0:-1]
    return zp


def _filter_deltas():
    min_decay = math.log(HY_DECAY_TARGET) / HY_SLOW_DECAY_PCT
    max_decay = math.log(HY_DECAY_TARGET) / HY_FAST_DECAY_PCT
    return np.abs(np.linspace(min_decay, max_decay, HY_WIDTH, dtype=np.float32))[None, :]


def _filter_kernel(z_ref, w1_ref, b1_ref, w2_ref, b2_ref, fr_ref, w3_ref, dl_ref, o_ref, *, seq_len):
    hi = lax.Precision.HIGHEST
    tl = z_ref.shape[0]
    z = z_ref[...]
    fr = fr_ref[...]
    h = jnp.sin(fr * (jnp.dot(z, w1_ref[...], precision=hi, preferred_element_type=F32) + b1_ref[...]))
    h = jnp.sin(fr * (jnp.dot(h, w2_ref[...], precision=hi, preferred_element_type=F32) + b2_ref[...]))
    t = z[:, 0:1]
    window = jnp.exp(-t * dl_ref[...])
    first_row = pl.program_id(0) * tl
    row = first_row + lax.broadcasted_iota(jnp.int32, (tl, 1), 0)

    def emit(direction):
        for o in range(HY_ORDER):
            c0 = (2 * o + direction) * HY_WIDTH
            f = jnp.dot(h, w3_ref[:, c0:c0 + HY_WIDTH], precision=hi, preferred_element_type=F32) * window
            if direction == 1:
                f = jnp.where(row > seq_len, f, 0.0)
            o_ref[o] = f.astype(o_ref.dtype)

    pl.when(first_row < seq_len)(lambda: emit(0))
    pl.when(first_row >= seq_len)(lambda: emit(1))


def _hyena_kernels(seq_len, w1, b1, w2, b2, freq, w3):
    tl = min(FILT_TL, seq_len)
    assert seq_len % tl == 0
    z = jnp.asarray(_filter_embedding(seq_len))
    w1p = jnp.zeros((HY_EMB_PAD, HY_FILTER_HIDDEN), F32).at[:HY_EMB_DIM].set(w1.astype(F32))
    hid = HY_FILTER_HIDDEN
    return pl.pallas_call(
        functools.partial(_filter_kernel, seq_len=seq_len),
        grid=(2 * seq_len // tl,),
        in_specs=[
            pl.BlockSpec((tl, HY_EMB_PAD), lambda i: (i, 0)),
            _resident((HY_EMB_PAD, hid)), _resident((1, hid)),
            _resident((hid, hid)), _resident((1, hid)), _resident((1, hid)),
            _resident((hid, 2 * HY_ORDER * HY_WIDTH)), _resident((1, HY_WIDTH)),
        ],
        out_specs=pl.BlockSpec((HY_ORDER, tl, HY_WIDTH), lambda i: (0, i, 0)),
        out_shape=jax.ShapeDtypeStruct((HY_ORDER, 2 * seq_len, HY_WIDTH), BF16),
        compiler_params=_cparams("parallel"),
        name="hyena_filters",
    )(z, w1p, b1.astype(F32).reshape(1, hid), w2.astype(F32), b2.astype(F32).reshape(1, hid),
      freq.astype(F32).reshape(1, hid), w3.astype(F32), jnp.asarray(_filter_deltas()))


def _fft_dims(seq_len):
    n = 2 * seq_len
    n1 = 1 << ((n.bit_length() - 1) // 2)
    n2 = n // n1
    assert n1 * n2 == n and n1 == n2, "sequence length must give a square transform"
    return n1, n2


def _stack_complex(fr, fi):
    return np.block([[fr, -fi], [fi, fr]])


@functools.lru_cache(maxsize=None)
def _dft_constants(seq_len):
    n1, n2 = _fft_dims(seq_len)
    n = n1 * n2
    k1 = np.arange(n1)[:, None].astype(np.float64)
    t1 = np.arange(n1 // 2)[None, :].astype(np.float64)
    a1 = -2.0 * math.pi * k1 * t1 / n1
    f1r, f1i = np.cos(a1), np.sin(a1)
    w1_complex = _stack_complex(f1r, f1i)
    a1_full = -2.0 * math.pi * k1 * np.arange(n1)[None, :].astype(np.float64) / n1
    w1_real = np.concatenate([np.cos(a1_full), np.sin(a1_full)], axis=0)
    k2 = np.arange(n2)[:, None].astype(np.float64)
    t2 = np.arange(n2)[None, :].astype(np.float64)
    a2 = -2.0 * math.pi * k2 * t2 / n2
    f2r, f2i = np.cos(a2), np.sin(a2)
    g1r, g1i = f1r.T / n, -f1i.T / n
    w3 = _stack_complex(g1r, g1i)
    kb = np.arange(HY_TW_SPLIT)[:, None].astype(np.float64)
    ab = -2.0 * math.pi * kb * t2 / n
    tbr, tbi = np.cos(ab), np.sin(ab)
    w2f = np.stack([_stack_complex(f2r * tbr[b] - f2i * tbi[b], f2r * tbi[b] + f2i * tbr[b])
                    for b in range(HY_TW_SPLIT)])
    w2i = np.stack([_stack_complex(tbr[b][:, None] * f2r - tbi[b][:, None] * f2i,
                                   -(tbr[b][:, None] * f2i + tbi[b][:, None] * f2r))
                    for b in range(HY_TW_SPLIT)])
    ka = (np.arange(n1 // HY_TW_SPLIT) * HY_TW_SPLIT)[:, None].astype(np.float64)
    aa = -2.0 * math.pi * ka * t2 / n
    ta = np.stack([np.cos(aa), np.sin(aa)], axis=0)[..., None]
    ta = np.broadcast_to(ta, (2, n1 // HY_TW_SPLIT, n2, LANE)).astype(np.float32)
    as_bf = lambda a: jnp.asarray(a, F32).astype(BF16)
    return dict(w1_complex=as_bf(w1_complex), w1_real=as_bf(w1_real), w2f=as_bf(w2f), w2i=as_bf(w2i),
                ta=jnp.asarray(ta), w3=as_bf(w3))


def _pack_pair(a, b):
    return pltpu.pack_elementwise([a, b], packed_dtype=BF16)


def _unpack_pair(w):
    return (pltpu.unpack_elementwise(w, index=0, packed_dtype=BF16, unpacked_dtype=F32),
            pltpu.unpack_elementwise(w, index=1, packed_dtype=BF16, unpacked_dtype=F32))


def _conv_pitch(n2):
    return n2 + 8


def _hyena_conv_kernel(z_ref, g_ref, kf_ref, w1_ref, w2f_ref, w2i_ref, w3_ref, ta_ref, bias_ref, o_ref,
                       x_scr, a_scr, *, n1, n2):
    pitch = _conv_pitch(n2)
    half = n1 // 2
    groups = n1 // HY_TW_SPLIT

    def pack_in(t1, carry):
        rows = pl.ds(pl.multiple_of(t1 * n2, n2), n2)
        x_scr[pl.ds(pl.multiple_of(t1 * pitch, 8), n2), :] = _pack_pair(
            z_ref[0, rows, :].astype(F32), z_ref[1, rows, :].astype(F32))
        return carry

    lax.fori_loop(0, half, pack_in, 0)

    def stage1(j, carry):
        cols = []
        for u in range(2):
            xr, xi = _unpack_pair(x_scr[pl.ds(2 * j + u, half, stride=pitch), :])
            cols.append(jnp.concatenate([xr, xi], axis=0).astype(BF16))
        out = jnp.dot(w1_ref[...], jnp.concatenate(cols, axis=1), preferred_element_type=F32)
        for u in range(2):
            o = out[:, u * LANE:(u + 1) * LANE]
            a_scr[pl.ds(2 * j + u, n1, stride=pitch), :] = _pack_pair(o[:n1], o[n1:])
        return carry

    lax.fori_loop(0, n2 // 2, stage1, 0, unroll=8)

    for kb in range(HY_TW_SPLIT):
        def stage2(jp, carry, kb=kb):
            rows, tws, stacks = [], [], []
            for u in range(2):
                ka = 2 * jp + u
                k1 = ka * HY_TW_SPLIT + kb
                rows.append(pl.ds(pl.multiple_of(k1 * pitch, 8), n2))
                ar, ai = _unpack_pair(a_scr[rows[u], :])
                twr, twi = ta_ref[0, ka], ta_ref[1, ka]
                tws.append((twr, twi))
                stacks.append(jnp.concatenate([ar * twr - ai * twi, ar * twi + ai * twr], axis=0).astype(BF16))
            spec = jnp.dot(w2f_ref[kb], jnp.concatenate(stacks, axis=1), preferred_element_type=F32)
            prods = []
            for u in range(2):
                k1 = (2 * jp + u) * HY_TW_SPLIT + kb
                sr, si = spec[:n2, u * LANE:(u + 1) * LANE], spec[n2:, u * LANE:(u + 1) * LANE]
                kr = kf_ref[0, 0, k1].astype(F32)
                ki = kf_ref[0, 1, k1].astype(F32)
                prods.append(jnp.concatenate([sr * kr - si * ki, sr * ki + si * kr], axis=0).astype(BF16))
            back = jnp.dot(w2i_ref[kb], jnp.concatenate(prods, axis=1), preferred_element_type=F32)
            for u in range(2):
                br, bi = back[:n2, u * LANE:(u + 1) * LANE], back[n2:, u * LANE:(u + 1) * LANE]
                twr, twi = tws[u]
                a_scr[rows[u], :] = _pack_pair(br * twr + bi * twi, bi * twr - br * twi)
            return carry

        lax.fori_loop(0, groups // 2, stage2, 0, unroll=8)

    def stage3(j, carry):
        cols = []
        for u in range(2):
            br, bi = _unpack_pair(a_scr[pl.ds(2 * j + u, n1, stride=pitch), :])
            cols.append(jnp.concatenate([br, bi], axis=0).astype(BF16))
        y = jnp.dot(w3_ref[...], jnp.concatenate(cols, axis=1), preferred_element_type=F32)
        for u in range(2):
            sl = pl.ds(2 * j + u, half, stride=pitch)
            za, zb = _unpack_pair(x_scr[sl, :])
            yu = y[:, u * LANE:(u + 1) * LANE]
            x_scr[sl, :] = _pack_pair(yu[:half] + bias_ref[...] * za, yu[half:] + bias_ref[...] * zb)
        return carry

    lax.fori_loop(0, n2 // 2, stage3, 0, unroll=8)

    def gate_out(t1, carry):
        rows = pl.ds(pl.multiple_of(t1 * n2, n2), n2)
        ta, tb = _unpack_pair(x_scr[pl.ds(pl.multiple_of(t1 * pitch, 8), n2), :])
        o_ref[0, rows, :] = (g_ref[0, rows, :].astype(F32) * ta).astype(o_ref.dtype)
        o_ref[1, rows, :] = (g_ref[1, rows, :].astype(F32) * tb).astype(o_ref.dtype)
        return carry

    lax.fori_loop(0, half, gate_out, 0)


def _hyena_conv(z, gate, kf, order, bias, consts):
    s, seq_len, c = z.shape
    n1, n2 = _fft_dims(seq_len)
    pitch = _conv_pitch(n2)
    assert s % 2 == 0 and c % LANE == 0 and (n1 // HY_TW_SPLIT) % 2 == 0
    seq_blk = pl.BlockSpec((2, seq_len, LANE), lambda cc, p: (p, 0, cc))
    return pl.pallas_call(
        functools.partial(_hyena_conv_kernel, n1=n1, n2=n2),
        grid=(c // LANE, s // 2),
        in_specs=[
            seq_blk,
            seq_blk,
            pl.BlockSpec((1, 2, n1, n2, LANE), lambda cc, p: (order, 0, 0, 0, cc), pipeline_mode=pl.Buffered(1)),
            _resident(consts["w1_complex"].shape),
            _resident(consts["w2f"].shape),
            _resident(consts["w2i"].shape),
            _resident(consts["w3"].shape),
            _resident(consts["ta"].shape),
            pl.BlockSpec((1, LANE), lambda cc, p: (0, cc)),
        ],
        out_specs=seq_blk,
        out_shape=jax.ShapeDtypeStruct(z.shape, BF16),
        scratch_shapes=[pltpu.VMEM((n1 // 2 * pitch, LANE), jnp.uint32),
                        pltpu.VMEM((n1 * pitch, LANE), jnp.uint32)],
        compiler_params=_cparams("parallel", "arbitrary"),
        name="hyena_conv",
    )(z, gate, kf, consts["w1_complex"], consts["w2f"], consts["w2i"], consts["w3"], consts["ta"],
      bias.astype(F32).reshape(1, c))


def _spectrum_kernel(k_ref, w1_ref, w2f_ref, ta_ref, o_ref, x_scr, a_scr, *, n1, n2):
    pitch = _conv_pitch(n2)
    groups = n1 // HY_TW_SPLIT

    def pack_in(t1, carry):
        x_scr[pl.ds(pl.multiple_of(t1 * pitch, 8), n2), :] = (
            k_ref[0, pl.ds(pl.multiple_of(t1 * n2, n2), n2), :].astype(F32))
        return carry

    lax.fori_loop(0, n1, pack_in, 0)

    def stage1(j, carry):
        cols = [x_scr[pl.ds(2 * j + u, n1, stride=pitch), :].astype(BF16) for u in range(2)]
        out = jnp.dot(w1_ref[...], jnp.concatenate(cols, axis=1), preferred_element_type=F32)
        for u in range(2):
            o = out[:, u * LANE:(u + 1) * LANE]
            a_scr[pl.ds(2 * j + u, n1, stride=pitch), :] = _pack_pair(o[:n1], o[n1:])
        return carry

    lax.fori_loop(0, n2 // 2, stage1, 0, unroll=4)

    for kb in range(HY_TW_SPLIT):
        def stage2(jp, carry, kb=kb):
            stacks = []
            for u in range(2):
                ka = 2 * jp + u
                k1 = ka * HY_TW_SPLIT + kb
                ar, ai = _unpack_pair(a_scr[pl.ds(pl.multiple_of(k1 * pitch, 8), n2), :])
                twr, twi = ta_ref[0, ka], ta_ref[1, ka]
                stacks.append(jnp.concatenate([ar * twr - ai * twi, ar * twi + ai * twr], axis=0).astype(BF16))
            spec = jnp.dot(w2f_ref[kb], jnp.concatenate(stacks, axis=1), preferred_element_type=F32)
            for u in range(2):
                k1 = (2 * jp + u) * HY_TW_SPLIT + kb
                o_ref[0, 0, k1] = spec[:n2, u * LANE:(u + 1) * LANE].astype(o_ref.dtype)
                o_ref[0, 1, k1] = spec[n2:, u * LANE:(u + 1) * LANE].astype(o_ref.dtype)
            return carry

        lax.fori_loop(0, groups // 2, stage2, 0, unroll=2)


def _filter_spectrum(kern, consts):
    _, n, c = kern.shape
    n1, n2 = _fft_dims(n // 2)
    pitch = _conv_pitch(n2)
    return pl.pallas_call(
        functools.partial(_spectrum_kernel, n1=n1, n2=n2),
        grid=(HY_ORDER, c // LANE),
        in_specs=[
            pl.BlockSpec((1, n, LANE), lambda o, cc: (o, 0, cc), pipeline_mode=pl.Buffered(1)),
            _resident(consts["w1_real"].shape),
            _resident(consts["w2f"].shape),
            _resident(consts["ta"].shape),
        ],
        out_specs=pl.BlockSpec((1, 2, n1, n2, LANE), lambda o, cc: (o, 0, 0, 0, cc)),
        out_shape=jax.ShapeDtypeStruct((HY_ORDER, 2, n1, n2, c), BF16),
        scratch_shapes=[pltpu.VMEM((n1 * pitch, LANE), F32), pltpu.VMEM((n1 * pitch, LANE), jnp.uint32)],
        compiler_params=_cparams("parallel", "arbitrary"),
        name="hyena_filter_spectrum",
    )(kern, consts["w1_real"], consts["w2f"], consts["ta"])


def _hyena_spectra(seq_len, w1, b1, w2, b2, freq, w3):
    consts = _dft_constants(seq_len)
    return _filter_spectrum(_hyena_kernels(seq_len, w1, b1, w2, b2, freq, w3), consts)


def _hyena(h3d, conv_w, conv_b, kf, hy_bias):
    seq_len = h3d.shape[1]
    consts = _dft_constants(seq_len)
    v, x1, x2 = _short_conv(h3d, conv_w, conv_b)
    z = v
    for n, gate in enumerate((x1, x2)):
        z = _hyena_conv(z, gate, kf, n, hy_bias[n], consts)
    return z


def _layernorm(y, g, b):
    mu = jnp.mean(y, axis=-1, keepdims=True)
    d = y - mu
    var = jnp.mean(d * d, axis=-1, keepdims=True)
    return d * lax.rsqrt(var + LN_EPS) * g + b


def _merge_kernel(g_ref, a_ref, hb_ref, c_ref, x_ref, wa_ref, wb_ref, wc_ref, wo_ref, lg_ref, lb_ref, o_ref):
    d = D_MODEL
    subs = [slice(r, r + MERGE_SUB) for r in range(0, x_ref.shape[0], MERGE_SUB)]
    merged = []
    for r in subs:
        acc = g_ref[r, 0:d].astype(F32) * jnp.dot(a_ref[r, :], wa_ref[...], preferred_element_type=F32)
        acc += g_ref[r, d:2 * d].astype(F32) * jnp.dot(hb_ref[r, :], wb_ref[...], preferred_element_type=F32)
        acc += g_ref[r, 2 * d:3 * d].astype(F32) * jnp.dot(c_ref[r, :], wc_ref[...], preferred_element_type=F32)
        merged.append(acc.astype(BF16))
    for r, mg in zip(subs, merged):
        mix = jnp.dot(mg, wo_ref[...], preferred_element_type=F32)
        o_ref[r, :] = _layernorm(DEEPNORM_ALPHA * x_ref[r, :] + mix, lg_ref[...], lb_ref[...])


def _merge(h2d, a, hb, c, x2d, wa, wb, wc, wo, ln_g, ln_b):
    m = x2d.shape[0]
    tm = MERGE_TM
    row = lambda width: pl.BlockSpec((tm, width), lambda i: (i, 0))
    return pl.pallas_call(
        _merge_kernel,
        grid=(m // tm,),
        in_specs=[
            row(N_BRANCH * D_MODEL), row(NA_WIDTH), row(HY_WIDTH), row(SWA_WIDTH), row(D_MODEL),
            _resident((NA_WIDTH, D_MODEL)), _resident((HY_WIDTH, D_MODEL)), _resident((SWA_WIDTH, D_MODEL)),
            _resident((D_MODEL, D_MODEL)), _resident((1, D_MODEL)), _resident((1, D_MODEL)),
        ],
        out_specs=row(D_MODEL),
        out_shape=jax.ShapeDtypeStruct((m, D_MODEL), F32),
        compiler_params=_cparams("parallel"),
        name="merge_ln",
    )(h2d, a, hb, c, x2d, wa, wb, wc, wo, ln_g, ln_b)


def _mlp_kernel(x_ref, wu_ref, bu_ref, wd_ref, bd_ref, lg_ref, lb_ref, o_ref):
    for r0 in range(0, x_ref.shape[0], MLP_SUB):
        rows = slice(r0, r0 + MLP_SUB)
        x = x_ref[rows, :]
        xb = x.astype(BF16)
        acc = DEEPNORM_ALPHA * x + bd_ref[...]
        for c in range(D_FF // MLP_FF_CHUNK):
            sl = slice(c * MLP_FF_CHUNK, (c + 1) * MLP_FF_CHUNK)
            up = jnp.dot(xb, wu_ref[:, sl], preferred_element_type=F32) + bu_ref[:, sl]
            up = jnp.square(jnp.maximum(up, 0.0))
            acc += jnp.dot(up.astype(BF16), wd_ref[sl, :], preferred_element_type=F32)
        o_ref[rows, :] = _layernorm(acc, lg_ref[...], lb_ref[...])


def _mlp(x2d, wu, bu, wd, bd, ln_g, ln_b):
    m = x2d.shape[0]
    tm = MLP_TM
    row = pl.BlockSpec((tm, D_MODEL), lambda i: (i, 0))
    return pl.pallas_call(
        _mlp_kernel,
        grid=(m // tm,),
        in_specs=[
            row, _resident((D_MODEL, D_FF)), _resident((1, D_FF)), _resident((D_FF, D_MODEL)),
            _resident((1, D_MODEL)), _resident((1, D_MODEL)), _resident((1, D_MODEL)),
        ],
        out_specs=row,
        out_shape=jax.ShapeDtypeStruct((m, D_MODEL), F32),
        compiler_params=_cparams("parallel"),
        name="mlp_ln",
    )(x2d, wu, bu, wd, bd, ln_g, ln_b)


def _merge_mlp_kernel(g_ref, a_ref, hb_ref, c_ref, x_ref, wa_ref, wb_ref, wc_ref, wo_ref, l1g_ref, l1b_ref,
                      wu_ref, bu_ref, wd_ref, bd_ref, l2g_ref, l2b_ref, o_ref):
    d = D_MODEL
    subs = [slice(r, r + MERGE_SUB) for r in range(0, x_ref.shape[0], MERGE_SUB)]
    merged = []
    for r in subs:
        acc = g_ref[r, 0:d].astype(F32) * jnp.dot(a_ref[r, :], wa_ref[...], preferred_element_type=F32)
        acc += g_ref[r, d:2 * d].astype(F32) * jnp.dot(hb_ref[r, :], wb_ref[...], preferred_element_type=F32)
        acc += g_ref[r, 2 * d:3 * d].astype(F32) * jnp.dot(c_ref[r, :], wc_ref[...], preferred_element_type=F32)
        merged.append(acc.astype(BF16))
    x1 = jnp.concatenate(
        [_layernorm(DEEPNORM_ALPHA * x_ref[r, :] + jnp.dot(mg, wo_ref[...], preferred_element_type=F32),
                    l1g_ref[...], l1b_ref[...]) for r, mg in zip(subs, merged)], axis=0)
    xb = x1.astype(BF16)
    acc = DEEPNORM_ALPHA * x1 + bd_ref[...]
    for c in range(D_FF // MLP_FF_CHUNK):
        sl = slice(c * MLP_FF_CHUNK, (c + 1) * MLP_FF_CHUNK)
        up = jnp.dot(xb, wu_ref[:, sl], preferred_element_type=F32) + bu_ref[:, sl]
        up = jnp.square(jnp.maximum(up, 0.0))
        acc += jnp.dot(up.astype(BF16), wd_ref[sl, :], preferred_element_type=F32)
    o_ref[...] = _layernorm(acc, l2g_ref[...], l2b_ref[...])


def _merge_mlp(h2d, a, hb, c, x2d, lp):
    m = x2d.shape[0]
    tm = MERGE_MLP_TM
    row = lambda width: pl.BlockSpec((tm, width), lambda i: (i, 0))
    vec = _resident((1, D_MODEL))
    return pl.pallas_call(
        _merge_mlp_kernel,
        grid=(m // tm,),
        in_specs=[
            row(N_BRANCH * D_MODEL), row(NA_WIDTH), row(HY_WIDTH), row(SWA_WIDTH), row(D_MODEL),
            _resident((NA_WIDTH, D_MODEL)), _resident((HY_WIDTH, D_MODEL)), _resident((SWA_WIDTH, D_MODEL)),
            _resident((D_MODEL, D_MODEL)), vec, vec,
            _resident((D_MODEL, D_FF)), _resident((1, D_FF)), _resident((D_FF, D_MODEL)), vec, vec, vec,
        ],
        out_specs=row(D_MODEL),
        out_shape=jax.ShapeDtypeStruct((m, D_MODEL), F32),
        compiler_params=_cparams("parallel"),
        name="merge_mlp_ln",
    )(h2d, a, hb, c, x2d, lp["wa"], lp["wb"], lp["wc"], lp["wo"], lp["ln1_g"], lp["ln1_b"],
      lp["wu"], lp["bu"], lp["wd"], lp["bd"], lp["ln2_g"], lp["ln2_b"])


def _permute_in_columns(a):
    q0 = OFF_SWA
    order = [h for j in range(SWA_GROUP) for h in (j, j + SWA_GROUP)]
    swa_q = [a[..., q0 + h * HEAD_DIM:q0 + (h + 1) * HEAD_DIM] for h in order]
    swa_k = a[..., q0 + SWA_WIDTH:q0 + SWA_WIDTH + SWA_KV_WIDTH]
    return jnp.concatenate([a[..., OFF_GATE:], a[..., OFF_HY:OFF_SWA], a[..., :2 * NA_WIDTH]] + swa_q + [swa_k],
                           axis=-1)


def _scale_query_columns(a):
    c = HEAD_DIM ** -0.5 * LOG2E
    return jnp.concatenate([a[..., :NA_WIDTH] * c, a[..., NA_WIDTH:OFF_SWA],
                            a[..., OFF_SWA:OFF_SWA + SWA_WIDTH] * c, a[..., OFF_SWA + SWA_WIDTH:]], axis=-1)


def _values_columns(a):
    v_swa = OFF_SWA + SWA_WIDTH + SWA_KV_WIDTH
    return jnp.concatenate([a[..., 2 * NA_WIDTH:3 * NA_WIDTH], a[..., v_swa:v_swa + SWA_KV_WIDTH]], axis=-1)


def _prepare_layer(l, p):
    row = lambda a: a.astype(F32).reshape(1, -1)
    return dict(
        w_in=_permute_in_columns(_scale_query_columns(p["w_in"][l])).astype(BF16),
        b_in=row(_permute_in_columns(_scale_query_columns(p["b_in"][l]))),
        conv_w=p["hy_conv_w"][l], conv_b=p["hy_conv_b"][l], hy_bias=p["hy_bias"][l],
        filt=(p["hy_filt_w1"][l], p["hy_filt_b1"][l], p["hy_filt_w2"][l], p["hy_filt_b2"][l],
              p["hy_filt_freq"][l], p["hy_filt_w3"][l]),
        w_vt=_values_columns(p["w_in"][l]).T.astype(BF16),
        b_vt=_values_columns(p["b_in"][l]).astype(F32).reshape(VT_WIDTH, 1),
        na_bias=_na_bias_table(p["na_rpb"][l]),
        sink=p["swa_sink"][l],
        wa=p["w_branch_a"][l].astype(BF16), wb=p["w_branch_b"][l].astype(BF16),
        wc=p["w_branch_c"][l].astype(BF16), wo=p["w_out"][l].astype(BF16),
        ln1_g=row(p["ln1_g"][l]), ln1_b=row(p["ln1_b"][l]),
        wu=p["w_up"][l].astype(BF16), bu=row(p["b_up"][l]),
        wd=p["w_down"][l].astype(BF16), bd=row(p["b_down"][l]),
        ln2_g=row(p["ln2_g"][l]), ln2_b=row(p["ln2_b"][l]),
    )


def _encoder_block(x, lp, swa_table):
    b_sz, seq_len, _ = x.shape
    m = b_sz * seq_len
    x2d = x.reshape(m, D_MODEL)
    h2d, vt = _inproj(x2d, lp["w_in"], lp["b_in"], lp["w_vt"], lp["b_vt"])
    h3d = h2d.reshape(b_sz, seq_len, D_TOK)
    a = _na_attention(h3d, vt, lp["na_bias"])
    kf = _hyena_spectra(seq_len, *lp["filt"])
    hb = _hyena(h3d, lp["conv_w"], lp["conv_b"], kf, lp["hy_bias"])
    c = _swa_attention(h3d, vt, lp["sink"], swa_table)
    x2 = _merge_mlp(h2d, a.reshape(m, NA_WIDTH), hb.reshape(m, HY_WIDTH), c.reshape(m, SWA_WIDTH), x2d, lp)
    return x2.reshape(b_sz, seq_len, D_MODEL)


def kernel(x_prompt, x_sample, w_in, b_in, hy_conv_w, hy_conv_b, hy_filt_w1, hy_filt_b1, hy_filt_w2,
           hy_filt_b2, hy_filt_freq, hy_filt_w3, hy_bias, na_rpb, swa_sink, w_branch_a, w_branch_b,
           w_branch_c, w_out, ln1_g, ln1_b, w_up, b_up, w_down, b_down, ln2_g, ln2_b):
    params = dict(w_in=w_in, b_in=b_in, hy_conv_w=hy_conv_w, hy_conv_b=hy_conv_b, hy_filt_w1=hy_filt_w1,
                  hy_filt_b1=hy_filt_b1, hy_filt_w2=hy_filt_w2, hy_filt_b2=hy_filt_b2,
                  hy_filt_freq=hy_filt_freq, hy_filt_w3=hy_filt_w3, hy_bias=hy_bias, na_rpb=na_rpb,
                  swa_sink=swa_sink, w_branch_a=w_branch_a, w_branch_b=w_branch_b, w_branch_c=w_branch_c,
                  w_out=w_out, ln1_g=ln1_g, ln1_b=ln1_b, w_up=w_up, b_up=b_up, w_down=w_down,
                  b_down=b_down, ln2_g=ln2_g, ln2_b=ln2_b)
    swa_table = _swa_table()
    y_prompt = x_prompt
    y_sample = x_sample
    for l in range(DEPTH):
        lp = _prepare_layer(l, params)
        y_prompt = _encoder_block(y_prompt, lp, swa_table)
        y_sample = _encoder_block(y_sample, lp, swa_table)
    return (y_prompt, y_sample)
```
